```python
import jax
import jax.numpy as jnp
from jax import lax
import numpy as np

D_MODEL = 2048
BATCH = 2
SEQ = 16384
DEPTH = 4
DEC_BATCH = 16
DEC_SEQ = 16
PAST_LEN = 2048

CHUNK = 64
N_MEM = 256
N_BRANCH = 5
BRANCH_W = D_MODEL // 2
EPS = 1e-6
RWKV_HEAD = 64
RWKV_H = BRANCH_W // RWKV_HEAD
RWKV_LORA = 64
RWKV_SHIFT_W = 3 * BRANCH_W + 2 * RWKV_LORA
RWKV_DECAY_SCALE = 0.606531
GLA_H = 4
GLA_DK = BRANCH_W // 2 // GLA_H
GLA_DV = BRANCH_W // GLA_H
GLA_LORA = 16
GLA_GATE_NORM = 16.0
RET_H = 4
RET_DK = BRANCH_W // 2 // RET_H
RET_DV = BRANCH_W // RET_H
ROPE_BASE = 10000.0
ML_H = 4
ML_D = BRANCH_W // ML_H
ML_CONV = 4
ML_QK_BLOCK = 4
ML_NBLK = BRANCH_W // ML_QK_BLOCK
XA_H = 4
XA_D = BRANCH_W // XA_H

IN_LAYOUT = (
    ('rwkv_shift', RWKV_SHIFT_W),
    ('rwkv_z', BRANCH_W),
    ('gla_q', GLA_H * GLA_DK),
    ('gla_k', GLA_H * GLA_DK),
    ('gla_v', BRANCH_W),
    ('gla_gd', GLA_LORA),
    ('gla_z', BRANCH_W),
    ('ret_q', RET_H * RET_DK),
    ('ret_k', RET_H * RET_DK),
    ('ret_v', BRANCH_W),
    ('ret_z', BRANCH_W),
    ('ml_x', BRANCH_W),
    ('ml_z', BRANCH_W),
    ('xa_q', BRANCH_W),
    ('xa_z', BRANCH_W),
    ('gates', N_BRANCH * D_MODEL),
)
N_IN = RWKV_SHIFT_W + 2 * GLA_H * GLA_DK + GLA_LORA + 2 * RET_H * RET_DK + 9 * BRANCH_W + N_BRANCH * D_MODEL
Z_NAMES = ('rwkv_z', 'gla_z', 'ret_z', 'ml_z', 'xa_z')
STATE_KEYS = ('rwkv', 'rwkv_shift', 'gla', 'ret', 'ml_c', 'ml_n', 'ml_m', 'ml_conv')

kernel_name = 'hybrid_streaming_encoder_step'


def _cols(name):
    start = 0
    for nm, size in IN_LAYOUT:
        if nm == name:
            return start, start + size
        start += size
    raise KeyError(name)


def _proj(h, w_in, name):
    a, b = _cols(name)
    return h @ w_in[:, a:b]


def _rmsnorm(x, g):
    x32 = x.astype(jnp.float32)
    y = x32 * lax.rsqrt(jnp.mean(x32 * x32, axis=-1, keepdims=True) + EPS)
    return y.astype(x.dtype) * g


def _head_ln(o, g):
    o32 = o.astype(jnp.float32)
    mu = jnp.mean(o32, axis=-1, keepdims=True)
    c = o32 - mu
    y = c * lax.rsqrt(jnp.mean(c * c, axis=-1, keepdims=True) + EPS)
    return y.astype(o.dtype) * g.reshape(o.shape[-2:])


def _head_rms(o):
    o32 = o.astype(jnp.float32)
    return (o32 * lax.rsqrt(jnp.mean(o32 * o32, axis=-1, keepdims=True) + EPS)).astype(o.dtype)


def _heads(t, n):
    return t.reshape(t.shape[:-1] + (n, t.shape[-1] // n))


def _causal(L):
    return jnp.tril(jnp.ones((L, L), dtype=bool))


def _rope(x, pos):
    half = x.shape[-1] // 2
    inv = ROPE_BASE ** (-jnp.linspace(0.0, 1.0, half, dtype=jnp.float32))
    ang = pos[:, None] * inv[None, :]
    cos = jnp.cos(ang)[None, :, None, :].astype(x.dtype)
    sin = jnp.sin(ang)[None, :, None, :].astype(x.dtype)
    x1, x2 = x[..., :half], x[..., half:]
    return jnp.concatenate([x1 * cos - x2 * sin, x1 * sin + x2 * cos], axis=-1)


def _run_chunks(step, carry, inputs):
    t_len = inputs[0].shape[2]
    if t_len <= CHUNK:
        return step(carry, inputs)
    n = t_len // CHUNK

    def split(a):
        a = a.reshape(a.shape[:2] + (n, CHUNK) + a.shape[3:])
        return jnp.moveaxis(a, 2, 0)

    carry, out = lax.scan(step, carry, tuple(split(a) for a in inputs))
    out = jnp.moveaxis(out, 0, 2)
    return carry, out.reshape(out.shape[:2] + (t_len,) + out.shape[4:])


def _rwkv_step(S, inp):
    r, w, k, v, kk, a = inp
    removal = jnp.einsum('bhvk,bhk->bhv', S, kk)
    S_new = (S * w[:, :, None, :] - removal[..., None] * (kk * a)[:, :, None, :]
             + v[..., None] * k[:, :, None, :]).astype(S.dtype)
    return S_new, jnp.einsum('bhvk,bhk->bhv', S_new, r)


def _rwkv_branch(h, p, shift_prev, S0):
    B, T, _ = h.shape
    raw = _proj(h, p['w_in'], 'rwkv_shift')
    prev = jnp.concatenate([shift_prev.astype(raw.dtype), raw[:, :-1]], axis=1)
    mixed = raw + (prev - raw) * p['rwkv_mu']
    r, k, v, wd, ad = jnp.split(mixed, [BRANCH_W, 2 * BRANCH_W, 3 * BRANCH_W, 3 * BRANCH_W + RWKV_LORA], axis=-1)
    w = jnp.exp(-RWKV_DECAY_SCALE * jax.nn.sigmoid(p['rwkv_w0'] + jnp.tanh(wd) @ p['rwkv_w2']))
    a = jax.nn.sigmoid(p['rwkv_a0'] + ad @ p['rwkv_a2'])
    kk = _heads(k * p['rwkv_k_k'], RWKV_H)
    kk32 = kk.astype(jnp.float32)
    kk = (kk32 / jnp.maximum(jnp.sqrt(jnp.sum(kk32 * kk32, axis=-1, keepdims=True)), 1e-12)).astype(k.dtype)
    k = k * (1.0 + (a - 1.0) * p['rwkv_k_a'])
    rh, kh, vh, wh, ah = (_heads(t, RWKV_H) for t in (r, k, v, w, a))
    seq = lambda t: jnp.moveaxis(t, 1, 0)
    S, ys = lax.scan(_rwkv_step, S0, (seq(rh), seq(wh), seq(kh), seq(vh), seq(kk), seq(ah)))
    y = _head_ln(jnp.moveaxis(ys, 0, 1), p['rwkv_ln_g'])
    bonus = jnp.sum(rh * kh * p['rwkv_r_k'].reshape(RWKV_H, RWKV_HEAD), axis=-1, keepdims=True) * vh
    return (y + bonus).reshape(B, T, BRANCH_W), raw[:, -1:], S


def _gla_chunk(S, inp):
    q, k, v, g = inp
    L = q.shape[2]
    b = jnp.cumsum(g, axis=2)
    b_end = b[:, :, -1:, :]
    qi = q * jnp.exp(b).astype(q.dtype)
    ki = k * jnp.exp(-b).astype(k.dtype)
    A = jnp.where(_causal(L), jnp.einsum('bhtd,bhsd->bhts', qi, ki), 0.0)
    o = jnp.einsum('bhts,bhse->bhte', A, v) + jnp.einsum('bhtd,bhde->bhte', qi, S)
    k_end = k * jnp.exp(b_end - b).astype(k.dtype)
    S_new = jnp.exp(b_end[:, :, 0, :]).astype(S.dtype)[..., None] * S + jnp.einsum('bhsd,bhse->bhde', k_end, v)
    return S_new.astype(S.dtype), o


def _gla_branch(h, p, S0):
    B, T, _ = h.shape
    w_in = p['w_in']
    q = _heads(_proj(h, w_in, 'gla_q'), GLA_H) * (GLA_DK ** -0.5)
    k = _heads(_proj(h, w_in, 'gla_k'), GLA_H)
    v = _heads(_proj(h, w_in, 'gla_v'), GLA_H)
    gk = _proj(h, w_in, 'gla_gd') @ p['gla_g2'] + p['gla_gb']
    g = _heads(jax.nn.log_sigmoid(gk.astype(jnp.float32)) / GLA_GATE_NORM, GLA_H)
    S, o = _run_chunks(_gla_chunk, S0, tuple(jnp.swapaxes(t, 1, 2) for t in (q, k, v, g)))
    o = _head_rms(jnp.swapaxes(o, 1, 2)) * p['gla_norm_g'].reshape(GLA_H, GLA_DV)
    return o.reshape(B, T, BRANCH_W), S


def _ret_log_decay():
    return jnp.log(1.0 - 2.0 ** (-5.0 - jnp.arange(RET_H, dtype=jnp.float32)))


def _ret_chunk(S, inp):
    q, k, v = inp
    L = q.shape[2]
    lg = _ret_log_decay()
    t = jnp.arange(L, dtype=jnp.float32)
    rel = t[:, None] - t[None, :]
    decay = jnp.where(rel >= 0, jnp.exp(lg[:, None, None] * jnp.maximum(rel, 0.0)), 0.0).astype(q.dtype)
    A = jnp.einsum('bhtd,bhsd->bhts', q, k) * decay
    o = (jnp.einsum('bhts,bhse->bhte', A, v)
         + jnp.exp(lg[:, None] * (t + 1.0)).astype(q.dtype)[..., None] * jnp.einsum('bhtd,bhde->bhte', q, S))
    k_end = k * jnp.exp(lg[:, None] * (L - 1.0 - t)).astype(k.dtype)[..., None]
    S_new = jnp.exp(lg * L).astype(S.dtype)[:, None, None] * S + jnp.einsum('bhsd,bhse->bhde', k_end, v)
    return S_new.astype(S.dtype), o


def _ret_branch(h, pos, p, S0):
    B, T, _ = h.shape
    w_in = p['w_in']
    q = _rope(_heads(_proj(h, w_in, 'ret_q'), RET_H), pos) * (RET_DK ** -0.5)
    k = _rope(_heads(_proj(h, w_in, 'ret_k'), RET_H), pos)
    v = _heads(_proj(h, w_in, 'ret_v'), RET_H)
    S, o = _run_chunks(_ret_chunk, S0, tuple(jnp.swapaxes(t, 1, 2) for t in (q, k, v)))
    return _head_rms(jnp.swapaxes(o, 1, 2)).reshape(B, T, BRANCH_W), S


def _mlstm_chunk(carry, inp):
    C, n, m = carry
    q, k, v, ig, lf = inp
    L = q.shape[2]
    b = jnp.cumsum(lf, axis=2)
    m_prev = m.astype(jnp.float32)
    dlog = jnp.where(_causal(L), b[..., :, None] - b[..., None, :] + ig[..., None, :], -jnp.inf)
    from_state = b + m_prev[..., None]
    m_t = jnp.maximum(from_state, jnp.max(dlog, axis=-1))
    w_intra = jnp.exp(dlog - m_t[..., None]).astype(q.dtype)
    w_state = jnp.exp(from_state - m_t).astype(q.dtype)
    s = jnp.einsum('bhtd,bhsd->bhts', q, k) * w_intra
    num = jnp.einsum('bhts,bhse->bhte', s, v) + w_state[..., None] * jnp.einsum('bhtd,bhde->bhte', q, C)
    den = jnp.sum(s, axis=-1) + w_state * jnp.einsum('bhtd,bhd->bht', q, n)
    out = num / jnp.maximum(jnp.abs(den), jnp.exp(-m_t).astype(q.dtype))[..., None]
    m_new = m_t[..., -1]
    carry_decay = jnp.exp(b[..., -1] + m_prev - m_new).astype(C.dtype)
    w_k = jnp.exp(b[..., -1:] - b + ig - m_new[..., None]).astype(k.dtype)
    C_new = carry_decay[..., None, None] * C + jnp.einsum('bhsd,bhse->bhde', k * w_k[..., None], v)
    n_new = carry_decay[..., None] * n + jnp.einsum('bhsd,bhs->bhd', k, w_k)
    return (C_new.astype(C.dtype), n_new.astype(n.dtype), m_new.astype(m.dtype)), out


def _mlstm_branch(h, p, conv_prev, carry0):
    B, T, _ = h.shape
    xm = _proj(h, p['w_in'], 'ml_x')
    xpad = jnp.concatenate([conv_prev.astype(xm.dtype), xm], axis=1)
    conv = sum(xpad[:, j:j + T] * p['ml_conv_w'][j] for j in range(ML_CONV)) + p['ml_conv_b']
    xc = jax.nn.silu(conv)

    def blockdiag(t, w):
        return jnp.einsum('btnc,ncd->btnd', t.reshape(B, T, ML_NBLK, ML_QK_BLOCK), w).reshape(B, T, BRANCH_W)

    q = blockdiag(xc, p['ml_wq'])
    k = blockdiag(xc, p['ml_wk'])
    v = blockdiag(xm, p['ml_wv'])
    gates = (jnp.concatenate([q, k, v], axis=-1) @ p['ml_w_if'] + p['ml_b_if']).astype(jnp.float32)
    ig = gates[..., :ML_H]
    lf = jax.nn.log_sigmoid(gates[..., ML_H:])
    qh, kh, vh = (jnp.swapaxes(_heads(t, ML_H), 1, 2) for t in (q, k * (ML_D ** -0.5), v))
    carry, hh = _run_chunks(_mlstm_chunk, carry0, (qh, kh, vh, jnp.swapaxes(ig, 1, 2), jnp.swapaxes(lf, 1, 2)))
    y = _head_ln(jnp.swapaxes(hh, 1, 2), p['ml_norm_g']).reshape(B, T, BRANCH_W) + p['ml_skip'] * xc
    return y, carry, xpad[:, -(ML_CONV - 1):]


def _mem_kv(mem, p):
    kv = _rmsnorm(mem, p['mem_norm_g']) @ p['w_mem_kv']
    k, v = jnp.split(kv, 2, axis=-1)
    return _heads(k, XA_H), _heads(v, XA_H)


def _xattn_branch(h, p, mem_k, mem_v):
    B, T, _ = h.shape
    q = _heads(_proj(h, p['w_in'], 'xa_q'), XA_H)
    s = jnp.einsum('bthd,bmhd->bhtm', q, mem_k.astype(q.dtype)).astype(jnp.float32) * (XA_D ** -0.5)
    prob = jax.nn.softmax(s, axis=-1).astype(q.dtype)
    return jnp.einsum('bhtm,bmhd->bthd', prob, mem_v.astype(q.dtype)).reshape(B, T, BRANCH_W)


def _zero_state(batch, dtype):
    return {
        'rwkv': jnp.zeros((batch, RWKV_H, RWKV_HEAD, RWKV_HEAD), dtype),
        'rwkv_shift': jnp.zeros((batch, 1, RWKV_SHIFT_W), dtype),
        'gla': jnp.zeros((batch, GLA_H, GLA_DK, GLA_DV), dtype),
        'ret': jnp.zeros((batch, RET_H, RET_DK, RET_DV), dtype),
        'ml_c': jnp.zeros((batch, ML_H, ML_D, ML_D), dtype),
        'ml_n': jnp.zeros((batch, ML_H, ML_D), dtype),
        'ml_m': jnp.zeros((batch, ML_H), dtype),
        'ml_conv': jnp.zeros((batch, ML_CONV - 1, BRANCH_W), dtype),
    }


def _layer(x, pos, st, mem_k, mem_v, p):
    h = _rmsnorm(x, p['norm_g'])
    w_in = p['w_in']
    o_a, shift_new, s_rwkv = _rwkv_branch(h, p, st['rwkv_shift'], st['rwkv'])
    o_b, s_gla = _gla_branch(h, p, st['gla'])
    o_c, s_ret = _ret_branch(h, pos, p, st['ret'])
    o_d, (c_new, n_new, m_new), conv_new = _mlstm_branch(h, p, st['ml_conv'], (st['ml_c'], st['ml_n'], st['ml_m']))
    o_x = _xattn_branch(h, p, mem_k, mem_v)
    g0, _ = _cols('gates')
    merged = None
    for i, o in enumerate((o_a, o_b, o_c, o_d, o_x)):
        y_b = (o * jax.nn.silu(_proj(h, w_in, Z_NAMES[i]))) @ p['w_branch'][i]
        gate = jax.nn.sigmoid(h @ w_in[:, g0 + i * D_MODEL:g0 + (i + 1) * D_MODEL])
        merged = gate * y_b if merged is None else merged + gate * y_b
    x = x + merged @ p['w_out']
    new = {'rwkv': s_rwkv, 'rwkv_shift': shift_new, 'gla': s_gla, 'ret': s_ret,
           'ml_c': c_new, 'ml_n': n_new, 'ml_m': m_new, 'ml_conv': conv_new}
    return x, new


def setup_inputs(seed: int = 0) -> dict:
    key = jax.random.key(seed)
    ks = iter(jax.random.split(key, 64))
    f32 = jnp.float32

    def nrm(shape, scale=1.0):
        return jax.random.normal(next(ks), shape, f32) * scale

    def gain(shape):
        return 1.0 + nrm(shape, 0.02)

    b_if = jnp.concatenate([nrm((DEPTH, ML_H), 0.1),
                            jnp.linspace(3.0, 6.0, ML_H, dtype=f32)[None, :] + nrm((DEPTH, ML_H), 0.1)], axis=-1)
    return {
        'x_prompt': nrm((BATCH, SEQ, D_MODEL)),
        'x_sample': nrm((DEC_BATCH, DEC_SEQ, D_MODEL)),
        'mem_prompt': nrm((BATCH, N_MEM, D_MODEL)),
        'cache_mem_k': nrm((DEPTH, DEC_BATCH, N_MEM, XA_H, XA_D)),
        'cache_mem_v': nrm((DEPTH, DEC_BATCH, N_MEM, XA_H, XA_D)),
        'state_rwkv': nrm((DEPTH, DEC_BATCH, RWKV_H, RWKV_HEAD, RWKV_HEAD), 0.3),
        'state_rwkv_shift': nrm((DEPTH, DEC_BATCH, 1, RWKV_SHIFT_W)),
        'state_gla': nrm((DEPTH, DEC_BATCH, GLA_H, GLA_DK, GLA_DV), 0.3),
        'state_ret': nrm((DEPTH, DEC_BATCH, RET_H, RET_DK, RET_DV), 0.3),
        'state_mlstm_c': nrm((DEPTH, DEC_BATCH, ML_H, ML_D, ML_D), 0.1),
        'state_mlstm_n': nrm((DEPTH, DEC_BATCH, ML_H, ML_D), 0.1),
        'state_mlstm_m': nrm((DEPTH, DEC_BATCH, ML_H)),
        'state_mlstm_conv': nrm((DEPTH, DEC_BATCH, ML_CONV - 1, BRANCH_W)),
        'norm_g': gain((DEPTH, D_MODEL)),
        'mem_norm_g': gain((DEPTH, D_MODEL)),
        'w_in': nrm((DEPTH, D_MODEL, N_IN), D_MODEL ** -0.5),
        'w_mem_kv': nrm((DEPTH, D_MODEL, 2 * XA_H * XA_D), D_MODEL ** -0.5),
        'rwkv_mu': jax.random.uniform(next(ks), (DEPTH, RWKV_SHIFT_W), f32),
        'rwkv_w0': nrm((DEPTH, BRANCH_W), 0.5),
        'rwkv_w2': nrm((DEPTH, RWKV_LORA, BRANCH_W), 0.5 * RWKV_LORA ** -0.5),
        'rwkv_a0': nrm((DEPTH, BRANCH_W), 0.5),
        'rwkv_a2': nrm((DEPTH, RWKV_LORA, BRANCH_W), 0.5 * RWKV_LORA ** -0.5),
        'rwkv_k_k': 0.85 + nrm((DEPTH, BRANCH_W), 0.05),
        'rwkv_k_a': 1.0 + nrm((DEPTH, BRANCH_W), 0.05),
        'rwkv_r_k': nrm((DEPTH, BRANCH_W), 0.1),
        'rwkv_ln_g': gain((DEPTH, BRANCH_W)),
        'gla_g2': nrm((DEPTH, GLA_LORA, GLA_H * GLA_DK), GLA_LORA ** -0.5),
        'gla_gb': nrm((DEPTH, GLA_H * GLA_DK), 0.1),
        'gla_norm_g': gain((DEPTH, BRANCH_W)),
        'ml_conv_w': nrm((DEPTH, ML_CONV, BRANCH_W), ML_CONV ** -0.5),
        'ml_conv_b': nrm((DEPTH, BRANCH_W), 0.01),
        'ml_wq': nrm((DEPTH, ML_NBLK, ML_QK_BLOCK, ML_QK_BLOCK), ML_QK_BLOCK ** -0.5),
        'ml_wk': nrm((DEPTH, ML_NBLK, ML_QK_BLOCK, ML_QK_BLOCK), ML_QK_BLOCK ** -0.5),
        'ml_wv': nrm((DEPTH, ML_NBLK, ML_QK_BLOCK, ML_QK_BLOCK), ML_QK_BLOCK ** -0.5),
        'ml_w_if': nrm((DEPTH, 3 * BRANCH_W, 2 * ML_H), (3 * BRANCH_W) ** -0.5),
        'ml_b_if': b_if,
        'ml_skip': gain((DEPTH, BRANCH_W)),
        'ml_norm_g': gain((DEPTH, BRANCH_W)),
        'w_branch': nrm((DEPTH, N_BRANCH, BRANCH_W, D_MODEL), BRANCH_W ** -0.5),
        'w_out': nrm((DEPTH, D_MODEL, D_MODEL), D_MODEL ** -0.5),
        'final_norm_g': gain((D_MODEL,)),
    }


def reference(x_prompt, x_sample, mem_prompt, cache_mem_k, cache_mem_v, state_rwkv, state_rwkv_shift,
              state_gla, state_ret, state_mlstm_c, state_mlstm_n, state_mlstm_m, state_mlstm_conv,
              norm_g, mem_norm_g, w_in, w_mem_kv, rwkv_mu, rwkv_w0, rwkv_w2, rwkv_a0, rwkv_a2,
              rwkv_k_k, rwkv_k_a, rwkv_r_k, rwkv_ln_g, gla_g2, gla_gb, gla_norm_g,
              ml_conv_w, ml_conv_b, ml_wq, ml_wk, ml_wv, ml_w_if, ml_b_if, ml_skip, ml_norm_g,
              w_branch, w_out, final_norm_g):
    pos_p = jnp.arange(x_prompt.shape[1], dtype=jnp.float32)
    pos_s = PAST_LEN + jnp.arange(x_sample.shape[1], dtype=jnp.float32)
    cache_in = {'rwkv': state_rwkv, 'rwkv_shift': state_rwkv_shift, 'gla': state_gla, 'ret': state_ret,
                'ml_c': state_mlstm_c, 'ml_n': state_mlstm_n, 'ml_m': state_mlstm_m, 'ml_conv': state_mlstm_conv}
    new_p = {name: [] for name in STATE_KEYS + ('mem_k', 'mem_v')}
    new_s = {name: [] for name in STATE_KEYS}
    yp, ys = x_prompt, x_sample
    for l in range(DEPTH):
        p = {'norm_g': norm_g[l], 'mem_norm_g': mem_norm_g[l], 'w_in': w_in[l], 'w_mem_kv': w_mem_kv[l],
             'rwkv_mu': rwkv_mu[l], 'rwkv_w0': rwkv_w0[l], 'rwkv_w2': rwkv_w2[l], 'rwkv_a0': rwkv_a0[l],
             'rwkv_a2': rwkv_a2[l], 'rwkv_k_k': rwkv_k_k[l], 'rwkv_k_a': rwkv_k_a[l], 'rwkv_r_k': rwkv_r_k[l],
             'rwkv_ln_g': rwkv_ln_g[l], 'gla_g2': gla_g2[l], 'gla_gb': gla_gb[l], 'gla_norm_g': gla_norm_g[l],
             'ml_conv_w': ml_conv_w[l], 'ml_conv_b': ml_conv_b[l], 'ml_wq': ml_wq[l], 'ml_wk': ml_wk[l],
             'ml_wv': ml_wv[l], 'ml_w_if': ml_w_if[l], 'ml_b_if': ml_b_if[l], 'ml_skip': ml_skip[l],
             'ml_norm_g': ml_norm_g[l], 'w_branch': w_branch[l], 'w_out': w_out[l]}
        mk, mv = _mem_kv(mem_prompt, p)
        yp, stp = _layer(yp, pos_p, _zero_state(x_prompt.shape[0], x_prompt.dtype), mk, mv, p)
        ys, sts = _layer(ys, pos_s, {name: cache_in[name][l] for name in STATE_KEYS},
                         cache_mem_k[l], cache_mem_v[l], p)
        for name in STATE_KEYS:
            new_p[name].append(stp[name])
            new_s[name].append(sts[name])
        new_p['mem_k'].append(mk)
        new_p['mem_v'].append(mv)
    y_prompt = _rmsnorm(yp, final_norm_g)
    y_sample = _rmsnorm(ys, final_norm_g)
    P = {name: jnp.stack(lst, axis=0) for name, lst in new_p.items()}
    S = {name: jnp.stack(lst, axis=0) for name, lst in new_s.items()}
    return (y_prompt, y_sample,
            P['rwkv'], P['rwkv_shift'], P['gla'], P['ret'], P['ml_c'], P['ml_n'], P['ml_m'], P['ml_conv'],
            P['mem_k'], P['mem_v'],
            S['rwkv'], S['rwkv_shift'], S['gla'], S['ret'], S['ml_c'], S['ml_n'], S['ml_m'], S['ml_conv'])
```

```python
import functools
import math

import jax
import jax.numpy as jnp
from jax import lax
from jax.experimental import pallas as pl
from jax.experimental.pallas import tpu as pltpu

F32 = jnp.float32
BF16 = jnp.bfloat16

D_MODEL = 2048
BW = 1024
EPS = 1e-6
CHUNK = 64
N_BRANCH = 5
PAST_LEN = 2048
RWKV_H, RWKV_D, RWKV_LORA = 16, 64, 64
RWKV_SHIFT_W = 3 * BW + 2 * RWKV_LORA
RWKV_DECAY_SCALE = 0.606531
GLA_H, GLA_DK, GLA_DV, GLA_LORA = 4, 128, 256, 16
GLA_GATE_NORM = 16.0
RET_H, RET_DK, RET_DV = 4, 128, 256
ROPE_BASE = 10000.0
ML_H, ML_D, ML_CONV, ML_QK_BLOCK = 4, 256, 4, 4
XA_H, XA_D = 4, 256
LANE = 128
VMEM_LIMIT = 56 * 1024 * 1024

_IN_LAYOUT = (
    ('rwkv_shift', RWKV_SHIFT_W), ('rwkv_z', BW), ('gla_q', 512), ('gla_k', 512), ('gla_v', BW),
    ('gla_gd', GLA_LORA), ('gla_z', BW), ('ret_q', 512), ('ret_k', 512), ('ret_v', BW), ('ret_z', BW),
    ('ml_x', BW), ('ml_z', BW), ('xa_q', BW), ('xa_z', BW), ('gates', N_BRANCH * D_MODEL),
)
_PACK_ORDER = ('gates', 'rwkv_z', 'gla_z', 'ret_z', 'ml_z', 'xa_z', 'ret_q', 'ret_k', 'ret_v',
               'gla_q', 'gla_k', 'gla_v', 'ml_x', 'xa_q', 'rwkv_shift', 'gla_gd')
PROJ_TN = 512


def _src_cols(name):
    start = 0
    for nm, size in _IN_LAYOUT:
        if nm == name:
            return start, start + size
        start += size
    raise KeyError(name)


def _pack_offsets():
    off, cur = {}, 0
    for nm in _PACK_ORDER:
        a, b = _src_cols(nm)
        off[nm] = cur
        cur += -(-(b - a) // LANE) * LANE
    total = -(-cur // PROJ_TN) * PROJ_TN
    return off, total


_OFF, NP = _pack_offsets()


def _pack_w_in(w_in):
    parts, cur = [], 0
    for nm in _PACK_ORDER:
        a, b = _src_cols(nm)
        parts.append(w_in[:, :, a:b])
        width = -(-(b - a) // LANE) * LANE
        if width != b - a:
            parts.append(jnp.zeros(w_in.shape[:2] + (width - (b - a),), w_in.dtype))
        cur += width
    if NP != cur:
        parts.append(jnp.zeros(w_in.shape[:2] + (NP - cur,), w_in.dtype))
    return jnp.concatenate(parts, axis=-1).astype(BF16)


def _cparams(*sem):
    return pltpu.CompilerParams(dimension_semantics=sem, vmem_limit_bytes=VMEM_LIMIT)


def _dot(a, b):
    return jnp.dot(a, b, preferred_element_type=F32)


def _dot_nt(a, b):
    return lax.dot_general(a, b, (((1,), (1,)), ((), ())), preferred_element_type=F32)


def _dot_tn(a, b):
    return lax.dot_general(a, b, (((0,), (0,)), ((), ())), preferred_element_type=F32)


def _bf(x):
    return x.astype(BF16)


def _split3(x):
    hi = x.astype(BF16)
    r1 = x - hi.astype(F32)
    mid = r1.astype(BF16)
    lo = (r1 - mid.astype(F32)).astype(BF16)
    return hi, mid, lo


def _cumsum_rows(tri, x):
    hi, mid, lo = _split3(x)
    return _dot(tri, hi) + _dot(tri, mid) + _dot(tri, lo)


def _cumsum_cols(x, triu):
    hi, mid, lo = _split3(x)
    return _dot(hi, triu) + _dot(mid, triu) + _dot(lo, triu)


def _iota2(shape, dim):
    return lax.broadcasted_iota(jnp.int32, shape, dim)


def _log_sigmoid(x):
    return jnp.minimum(x, 0.0) - jnp.log1p(jnp.exp(-jnp.abs(x)))


def _sigmoid(x):
    return 1.0 / (1.0 + jnp.exp(-x))


def _silu(x):
    return x * _sigmoid(x)


def _normproj_kernel(x_ref, g_ref, w_ref, o_ref, h_ref):
    @pl.when(pl.program_id(1) == 0)
    def _():
        x = x_ref[...]
        ms = jnp.mean(x * x, axis=-1, keepdims=True)
        h_ref[...] = ((x * lax.rsqrt(ms + EPS)) * g_ref[...]).astype(BF16)

    o_ref[...] = _dot(h_ref[...], w_ref[...])


def _normproj(x, g, w, tm, tn):
    M, K = x.shape
    N = w.shape[1]
    return pl.pallas_call(
        _normproj_kernel,
        grid=(M // tm, N // tn),
        in_specs=[pl.BlockSpec((tm, K), lambda i, j: (i, 0)),
                  pl.BlockSpec((1, K), lambda i, j: (0, 0)),
                  pl.BlockSpec((K, tn), lambda i, j: (0, j))],
        out_specs=pl.BlockSpec((tm, tn), lambda i, j: (i, j)),
        out_shape=jax.ShapeDtypeStruct((M, N), F32),
        scratch_shapes=[pltpu.VMEM((tm, K), BF16)],
        compiler_params=_cparams("parallel", "arbitrary"),
        name="normproj",
    )(x, g, w)


def _mm_kernel(x_ref, w_ref, o_ref):
    o_ref[...] = _dot(_bf(x_ref[...]), w_ref[...])


def _matmul(x, w, tm):
    M, K = x.shape
    N = w.shape[1]
    return pl.pallas_call(
        _mm_kernel,
        grid=(M // tm,),
        in_specs=[pl.BlockSpec((tm, K), lambda i: (i, 0)),
                  pl.BlockSpec((K, N), lambda i: (0, 0))],
        out_specs=pl.BlockSpec((tm, N), lambda i: (i, 0)),
        out_shape=jax.ShapeDtypeStruct((M, N), F32),
        compiler_params=_cparams("parallel"),
        name="matmul",
    )(x, w)


def _rwkv_kernel(r_ref, lw_ref, k_ref, v_ref, kk_ref, b_ref, s0_ref, y_ref, s_ref, s_scr, *, L, HG):
    t = pl.program_id(2)

    @pl.when(t == 0)
    def _():
        s_scr[...] = s0_ref[0]

    nchunk = r_ref.shape[1] // L
    row, col = _iota2((L, L), 0), _iota2((L, L), 1)
    tri = (row >= col).astype(BF16)
    strict, incl = row > col, row >= col
    eye = (row == col).astype(F32)

    def chunk(c, carry):
        rows = pl.ds(pl.multiple_of(c * L, L), L)
        r_, lw, k_, v_ = r_ref[0, rows, :], lw_ref[0, rows, :], k_ref[0, rows, :], v_ref[0, rows, :]
        kk, b_ = kk_ref[0, rows, :], b_ref[0, rows, :]
        g = _cumsum_rows(tri, lw)
        gl = g[L - 1:L, :]
        kkp = kk * jnp.exp(g - lw)
        rg = r_ * jnp.exp(g)
        eng = jnp.exp(-g)
        kn, bn = k_ * eng, b_ * eng
        ee = jnp.exp(gl - g)
        kend, bend = k_ * ee, b_ * ee
        egl = jnp.exp(gl)
        ys = []
        for h in range(HG):
            sl = slice(h * RWKV_D, (h + 1) * RWKV_D)
            S = s_scr[h]
            Sb = _bf(S)
            kkp_h, rg_h = _bf(kkp[:, sl]), _bf(rg[:, sl])
            bk = _bf(jnp.concatenate([bn[:, sl], kn[:, sl]], axis=0))
            mn = _dot_nt(kkp_h, bk)
            aq = _dot_nt(rg_h, bk)
            M = jnp.where(strict, mn[:, :L], 0.0)
            N = jnp.where(strict, mn[:, L:], 0.0)
            aqb = jnp.where(incl, aq[:, :L], 0.0)
            aqk = jnp.where(incl, aq[:, L:], 0.0)
            w0 = _dot_nt(kkp_h, Sb)
            rs = _dot_nt(rg_h, Sb)
            vb = _bf(v_[:, sl])
            X = -M
            tinv = eye + X
            P = X
            n = 1
            while 2 * n < L:
                Pb = _bf(P)
                P = _dot(Pb, Pb)
                tinv = tinv + _dot(_bf(tinv), _bf(P))
                n *= 2
            U = _dot(_bf(tinv), _bf(w0 + _dot(_bf(N), vb)))
            Ub = _bf(U)
            ys.append(rs + _dot(_bf(aqk), vb) - _dot(_bf(aqb), Ub))
            s_scr[h] = (S * egl[:, sl] + _dot_tn(vb, _bf(kend[:, sl])) - _dot_tn(Ub, _bf(bend[:, sl])))
        y_ref[0, rows, :] = jnp.concatenate(ys, axis=1) if HG > 1 else ys[0]
        return carry

    lax.fori_loop(0, nchunk, chunk, 0)

    @pl.when(t == pl.num_programs(2) - 1)
    def _():
        s_ref[0] = s_scr[...]


def _rwkv_scan(r, lw, k, v, kk, b, s0):
    B, T, _ = r.shape
    L = min(T, CHUNK)
    tt = min(T, 512)
    HG = 4
    W = HG * RWKV_D
    seq = pl.BlockSpec((1, tt, W), lambda b_, h, t: (b_, t, h))
    st = pl.BlockSpec((1, HG, RWKV_D, RWKV_D), lambda b_, h, t: (b_, h, 0, 0))
    return pl.pallas_call(
        functools.partial(_rwkv_kernel, L=L, HG=HG),
        grid=(B, RWKV_H // HG, T // tt),
        in_specs=[seq] * 6 + [st],
        out_specs=[seq, st],
        out_shape=[jax.ShapeDtypeStruct((B, T, BW), F32), jax.ShapeDtypeStruct(s0.shape, F32)],
        scratch_shapes=[pltpu.VMEM((HG, RWKV_D, RWKV_D), F32)],
        compiler_params=_cparams("parallel", "parallel", "arbitrary"),
        name="rwkv_scan",
    )(r, lw, k, v, kk, b, s0)


def _gla_kernel(q_ref, k_ref, v_ref, gd_ref, g2_ref, gb_ref, ng_ref, s0_ref, o_ref, s_ref, s_scr, g_scr, *, L):
    t = pl.program_id(1)

    @pl.when(t == 0)
    def _():
        s_scr[...] = s0_ref[0]

    gk = _dot(_bf(gd_ref[...]), g2_ref[...]) + gb_ref[...]
    g_scr[...] = _log_sigmoid(gk) * (1.0 / GLA_GATE_NORM)

    nchunk = q_ref.shape[0] // L
    row, col = _iota2((L, L), 0), _iota2((L, L), 1)
    tri = (row >= col).astype(BF16)
    causal = row >= col
    scale = GLA_DK ** -0.5

    def chunk(c, carry):
        rows = pl.ds(pl.multiple_of(c * L, L), L)
        for h in range(GLA_H):
            ks = slice(h * GLA_DK, (h + 1) * GLA_DK)
            vs = slice(h * GLA_DV, (h + 1) * GLA_DV)
            q = q_ref[rows, ks] * scale
            k = k_ref[rows, ks]
            vb = _bf(v_ref[rows, vs])
            b = _cumsum_rows(tri, g_scr[rows, ks])
            b_end = b[L - 1:L, :]
            qi = _bf(q * jnp.exp(b))
            ki = _bf(k * jnp.exp(-b))
            A = jnp.where(causal, _dot_nt(qi, ki), 0.0)
            St = s_scr[h]
            o = _dot(_bf(A), vb) + _dot_nt(qi, _bf(St))
            k_end = _bf(k * jnp.exp(b_end - b))
            s_scr[h] = St * jnp.exp(b_end) + _dot_tn(vb, k_end)
            ms = jnp.mean(o * o, axis=-1, keepdims=True)
            o_ref[rows, vs] = (o * lax.rsqrt(ms + EPS)) * ng_ref[:, vs]
        return carry

    lax.fori_loop(0, nchunk, chunk, 0)

    @pl.when(t == pl.num_programs(1) - 1)
    def _():
        s_ref[0] = s_scr[...]


def _gla(P, B, T, g2p, gb, ng, s0t):
    L = min(T, CHUNK)
    tt = min(T, 512)
    nT = T // tt
    rowblk = lambda w, off: pl.BlockSpec((tt, w), lambda b_, t: (b_ * nT + t, off // w))
    const = lambda shape: pl.BlockSpec(shape, lambda b_, t: (0,) * len(shape))
    st = pl.BlockSpec((1, GLA_H, GLA_DV, GLA_DK), lambda b_, t: (b_, 0, 0, 0))
    return pl.pallas_call(
        functools.partial(_gla_kernel, L=L),
        grid=(B, nT),
        in_specs=[rowblk(512, _OFF['gla_q']), rowblk(512, _OFF['gla_k']), rowblk(BW, _OFF['gla_v']),
                  rowblk(LANE, _OFF['gla_gd']), const((LANE, 512)), const((1, 512)), const((1, BW)), st],
        out_specs=[pl.BlockSpec((tt, BW), lambda b_, t: (b_ * nT + t, 0)), st],
        out_shape=[jax.ShapeDtypeStruct((B * T, BW), F32), jax.ShapeDtypeStruct(s0t.shape, F32)],
        scratch_shapes=[pltpu.VMEM((GLA_H, GLA_DV, GLA_DK), F32), pltpu.VMEM((tt, 512), F32)],
        compiler_params=_cparams("parallel", "arbitrary"),
        name="gla_scan",
    )(P, P, P, P, g2p, gb, ng, s0t)


def _ret_kernel(q_ref, k_ref, v_ref, cos_ref, sin_ref, s0_ref, o_ref, s_ref, s_scr, *, L):
    t = pl.program_id(1)

    @pl.when(t == 0)
    def _():
        s_scr[...] = s0_ref[0]

    nchunk = q_ref.shape[0] // L
    rel = (_iota2((L, L), 0) - _iota2((L, L), 1)).astype(F32)
    tcol = _iota2((L, 1), 0).astype(F32)
    scale = RET_DK ** -0.5
    lgs = [math.log(1.0 - 2.0 ** (-5.0 - h)) for h in range(RET_H)]
    decay = [jnp.where(rel >= 0, jnp.exp(lg * jnp.maximum(rel, 0.0)), 0.0) for lg in lgs]
    rowd = [jnp.exp(lg * (tcol + 1.0)) for lg in lgs]
    cold = [jnp.exp(lg * (L - 1.0 - tcol)) for lg in lgs]

    def rope(x, cos2, sin2):
        return x * cos2 + pltpu.roll(x, RET_DK // 2, 1) * sin2

    def chunk(c, carry):
        rows = pl.ds(pl.multiple_of(c * L, L), L)
        cos2, sin2 = cos_ref[rows, :], sin_ref[rows, :]
        for h in range(RET_H):
            ks = slice(h * RET_DK, (h + 1) * RET_DK)
            vs = slice(h * RET_DV, (h + 1) * RET_DV)
            q = rope(q_ref[rows, ks], cos2, sin2) * scale
            k = rope(k_ref[rows, ks], cos2, sin2)
            vb = _bf(v_ref[rows, vs])
            qb = _bf(q)
            A = _dot_nt(qb, _bf(k)) * decay[h]
            S = s_scr[h]
            o = _dot(_bf(A), vb) + rowd[h] * _dot(qb, _bf(S))
            s_scr[h] = math.exp(lgs[h] * L) * S + _dot_tn(_bf(k * cold[h]), vb)
            ms = jnp.mean(o * o, axis=-1, keepdims=True)
            o_ref[rows, vs] = o * lax.rsqrt(ms + EPS)
        return carry

    lax.fori_loop(0, nchunk, chunk, 0)

    @pl.when(t == pl.num_programs(1) - 1)
    def _():
        s_ref[0] = s_scr[...]


def _ret(P, B, T, cos2, sin2, s0):
    L = min(T, CHUNK)
    tt = min(T, 512)
    nT = T // tt
    rowblk = lambda w, off: pl.BlockSpec((tt, w), lambda b_, t: (b_ * nT + t, off // w))
    tab = pl.BlockSpec((tt, RET_DK), lambda b_, t: (t, 0))
    st = pl.BlockSpec((1, RET_H, RET_DK, RET_DV), lambda b_, t: (b_, 0, 0, 0))
    return pl.pallas_call(
        functools.partial(_ret_kernel, L=L),
        grid=(B, nT),
        in_specs=[rowblk(512, _OFF['ret_q']), rowblk(512, _OFF['ret_k']), rowblk(BW, _OFF['ret_v']), tab, tab, st],
        out_specs=[pl.BlockSpec((tt, BW), lambda b_, t: (b_ * nT + t, 0)), st],
        out_shape=[jax.ShapeDtypeStruct((B * T, BW), F32), jax.ShapeDtypeStruct(s0.shape, F32)],
        scratch_shapes=[pltpu.VMEM((RET_H, RET_DK, RET_DV), F32)],
        compiler_params=_cparams("parallel", "arbitrary"),
        name="ret_scan",
    )(P, P, P, cos2, sin2, s0)


def _mlstm_pre_kernel(xc_ref, xm_ref, wq_ref, wk_ref, wv_ref, wif_ref, bif_ref, q_ref, k_ref, v_ref, g_ref):
    xc, xm = _bf(xc_ref[...]), _bf(xm_ref[...])
    q = _dot(xc, wq_ref[...])
    k = _dot(xc, wk_ref[...])
    v = _dot(xm, wv_ref[...])
    g_ref[...] = (_dot(_bf(q), wif_ref[0]) + _dot(_bf(k), wif_ref[1]) + _dot(_bf(v), wif_ref[2])) + bif_ref[...]
    q_ref[...] = q
    k_ref[...] = k * (ML_D ** -0.5)
    v_ref[...] = v


def _mlstm_pre(xc, xm, wq, wk, wv, wif, bif, tm):
    M = xc.shape[0]
    row = pl.BlockSpec((tm, BW), lambda i: (i, 0))
    wspec = pl.BlockSpec((BW, BW), lambda i: (0, 0))
    return pl.pallas_call(
        _mlstm_pre_kernel,
        grid=(M // tm,),
        in_specs=[row, row, wspec, wspec, wspec,
                  pl.BlockSpec((3, BW, LANE), lambda i: (0, 0, 0)), pl.BlockSpec((1, LANE), lambda i: (0, 0))],
        out_specs=[row, row, row, pl.BlockSpec((tm, LANE), lambda i: (i, 0))],
        out_shape=[jax.ShapeDtypeStruct((M, BW), F32)] * 3 + [jax.ShapeDtypeStruct((M, LANE), F32)],
        compiler_params=_cparams("parallel"),
        name="mlstm_pre",
    )(xc, xm, wq, wk, wv, wif, bif)


def _mlstm_kernel(q_ref, k_ref, v_ref, gc_ref, gr_ref, xc_ref, ng_ref, sk_ref, c0_ref, n0_ref, m0_ref,
                  o_ref, c_ref, n_ref, m_ref, c_scr, n_scr, m_scr, *, L):
    t = pl.program_id(1)

    @pl.when(t == 0)
    def _():
        c_scr[...] = c0_ref[0]
        n_scr[...] = n0_ref[0]
        m_scr[...] = m0_ref[0]

    nchunk = q_ref.shape[0] // L
    row, col = _iota2((L, L), 0), _iota2((L, L), 1)
    tri = (row >= col).astype(BF16)
    triu = (row <= col).astype(BF16)
    causal = row >= col

    def chunk(c, carry):
        rows = pl.ds(pl.multiple_of(c * L, L), L)
        gcb = gc_ref[0, c]
        grb = gr_ref[0, c]
        bc = _cumsum_rows(tri, gcb)
        br = _cumsum_cols(grb, triu)
        for h in range(ML_H):
            hs = slice(h * ML_D, (h + 1) * ML_D)
            q = q_ref[rows, hs]
            qb = _bf(q)
            k = k_ref[rows, hs]
            vb = _bf(v_ref[rows, hs])
            b_c = bc[:, ML_H + h:ML_H + h + 1]
            b_r = br[ML_H + h:ML_H + h + 1, :]
            ig_r = grb[h:h + 1, :]
            ig_c = gcb[:, h:h + 1]
            m_prev = m_scr[h][:, 0:1]
            dlog = jnp.where(causal, b_c - b_r + ig_r, -jnp.inf)
            from_state = b_c + m_prev
            m_t = jnp.maximum(from_state, jnp.max(dlog, axis=-1, keepdims=True))
            w_intra = jnp.exp(dlog - m_t)
            w_state = jnp.exp(from_state - m_t)
            s = _dot_nt(qb, _bf(k)) * w_intra
            C = c_scr[h]
            nvec = n_scr[h]
            num = _dot(_bf(s), vb) + w_state * _dot(qb, _bf(C))
            qn = jnp.sum(q * nvec, axis=-1, keepdims=True)
            den = jnp.sum(s, axis=-1, keepdims=True) + w_state * qn
            hh = num / jnp.maximum(jnp.abs(den), jnp.exp(-m_t))
            m_new = m_t[L - 1:L, :]
            b_end = b_c[L - 1:L, :]
            cd = jnp.exp(b_end + m_prev - m_new)
            w_k = jnp.exp(b_end - b_c + ig_c - m_new)
            kw = k * w_k
            c_scr[h] = cd * C + _dot_tn(_bf(kw), vb)
            n_scr[h] = cd * nvec + jnp.sum(kw, axis=0, keepdims=True)
            m_scr[h] = jnp.broadcast_to(m_new, (1, LANE))
            mu = jnp.mean(hh, axis=-1, keepdims=True)
            cen = hh - mu
            y = cen * lax.rsqrt(jnp.mean(cen * cen, axis=-1, keepdims=True) + EPS)
            o_ref[rows, hs] = y * ng_ref[:, hs] + sk_ref[:, hs] * xc_ref[rows, hs]
        return carry

    lax.fori_loop(0, nchunk, chunk, 0)

    @pl.when(t == pl.num_programs(1) - 1)
    def _():
        c_ref[0] = c_scr[...]
        n_ref[0] = n_scr[...]
        m_ref[0] = m_scr[...]


def _mlstm(q, k, v, gc, gr, xc, ng, sk, c0, n0, m0, B, T):
    L = min(T, CHUNK)
    tt = min(T, 512)
    nT = T // tt
    nc = tt // L
    row = pl.BlockSpec((tt, BW), lambda b_, t: (b_ * nT + t, 0))
    const = pl.BlockSpec((1, BW), lambda b_, t: (0, 0))
    stc = pl.BlockSpec((1, ML_H, ML_D, ML_D), lambda b_, t: (b_, 0, 0, 0))
    stn = pl.BlockSpec((1, ML_H, 1, ML_D), lambda b_, t: (b_, 0, 0, 0))
    stm = pl.BlockSpec((1, ML_H, 1, LANE), lambda b_, t: (b_, 0, 0, 0))
    return pl.pallas_call(
        functools.partial(_mlstm_kernel, L=L),
        grid=(B, nT),
        in_specs=[row, row, row,
                  pl.BlockSpec((1, nc, L, 8), lambda b_, t: (b_, t, 0, 0)),
                  pl.BlockSpec((1, nc, 8, L), lambda b_, t: (b_, t, 0, 0)),
                  row, const, const, stc, stn, stm],
        out_specs=[row, stc, stn, stm],
        out_shape=[jax.ShapeDtypeStruct((B * T, BW), F32), jax.ShapeDtypeStruct(c0.shape, F32),
                   jax.ShapeDtypeStruct(n0.shape, F32), jax.ShapeDtypeStruct(m0.shape, F32)],
        scratch_shapes=[pltpu.VMEM((ML_H, ML_D, ML_D), F32), pltpu.VMEM((ML_H, 1, ML_D), F32),
                        pltpu.VMEM((ML_H, 1, LANE), F32)],
        compiler_params=_cparams("parallel", "arbitrary"),
        name="mlstm_scan",
    )(q, k, v, gc, gr, xc, ng, sk, c0, n0, m0)


def _xattn_kernel(q_ref, mk_ref, mv_ref, o_ref):
    for h in range(XA_H):
        hs = slice(h * XA_D, (h + 1) * XA_D)
        s = _dot_nt(_bf(q_ref[:, hs]), _bf(mk_ref[0, :, hs])) * (XA_D ** -0.5)
        s = s - jnp.max(s, axis=-1, keepdims=True)
        e = jnp.exp(s)
        prob = e / jnp.sum(e, axis=-1, keepdims=True)
        o_ref[:, hs] = _dot(_bf(prob), _bf(mv_ref[0, :, hs]))


def _xattn(P, B, T, mk, mv):
    tt = min(T, 512)
    nT = T // tt
    n_mem = mk.shape[1]
    mem = pl.BlockSpec((1, n_mem, BW), lambda b_, t: (b_, 0, 0))
    return pl.pallas_call(
        _xattn_kernel,
        grid=(B, nT),
        in_specs=[pl.BlockSpec((tt, BW), lambda b_, t: (b_ * nT + t, _OFF['xa_q'] // BW)), mem, mem],
        out_specs=pl.BlockSpec((tt, BW), lambda b_, t: (b_ * nT + t, 0)),
        out_shape=jax.ShapeDtypeStruct((B * T, BW), F32),
        compiler_params=_cparams("parallel", "parallel"),
        name="xattn",
    )(P, mk, mv)


def _merge_kernel(o0, o1, o2, o3, o4, g_ref, w_ref, out_ref, acc):
    j = pl.program_id(1)

    @pl.when(j == 0)
    def _():
        acc[...] = jnp.zeros_like(acc)

    for i, o_ref in enumerate((o0, o1, o2, o3, o4)):
        @pl.when(j == i)
        def _(o_ref=o_ref):
            acc[...] += _sigmoid(g_ref[...]) * _dot(o_ref[...], w_ref[0])

    @pl.when(j == N_BRANCH - 1)
    def _():
        out_ref[...] = acc[...].astype(BF16)


def _merge(ogs, P, wb, tm):
    M = ogs[0].shape[0]
    og = pl.BlockSpec((tm, BW), lambda i, j: (i, 0))
    return pl.pallas_call(
        _merge_kernel,
        grid=(M // tm, N_BRANCH),
        in_specs=[og] * N_BRANCH + [pl.BlockSpec((tm, D_MODEL), lambda i, j: (i, _OFF['gates'] // D_MODEL + j)),
                                    pl.BlockSpec((1, BW, D_MODEL), lambda i, j: (j, 0, 0))],
        out_specs=pl.BlockSpec((tm, D_MODEL), lambda i, j: (i, 0)),
        out_shape=jax.ShapeDtypeStruct((M, D_MODEL), BF16),
        scratch_shapes=[pltpu.VMEM((tm, D_MODEL), F32)],
        compiler_params=_cparams("parallel", "arbitrary"),
        name="merge",
    )(*ogs, P, wb)


def _resid_kernel(x_ref, m_ref, w_ref, o_ref):
    o_ref[...] = x_ref[...] + _dot(m_ref[...], w_ref[...])


def _resid_out(x, merged, w, tm, tn):
    M = x.shape[0]
    return pl.pallas_call(
        _resid_kernel,
        grid=(M // tm, D_MODEL // tn),
        in_specs=[pl.BlockSpec((tm, tn), lambda i, j: (i, j)),
                  pl.BlockSpec((tm, D_MODEL), lambda i, j: (i, 0)),
                  pl.BlockSpec((D_MODEL, tn), lambda i, j: (0, j))],
        out_specs=pl.BlockSpec((tm, tn), lambda i, j: (i, j)),
        out_shape=jax.ShapeDtypeStruct((M, D_MODEL), F32),
        compiler_params=_cparams("parallel", "parallel"),
        name="resid_out",
    )(x, merged, w)


def _rmsnorm_kernel(x_ref, g_ref, o_ref):
    x = x_ref[...]
    ms = jnp.mean(x * x, axis=-1, keepdims=True)
    o_ref[...] = (x * lax.rsqrt(ms + EPS)) * g_ref[...]


def _rmsnorm(x, g, tm):
    M, K = x.shape
    return pl.pallas_call(
        _rmsnorm_kernel,
        grid=(M // tm,),
        in_specs=[pl.BlockSpec((tm, K), lambda i: (i, 0)), pl.BlockSpec((1, K), lambda i: (0, 0))],
        out_specs=pl.BlockSpec((tm, K), lambda i: (i, 0)),
        out_shape=jax.ShapeDtypeStruct((M, K), F32),
        compiler_params=_cparams("parallel"),
        name="rmsnorm",
    )(x, g)


def _heads(t, n):
    return t.reshape(t.shape[:-1] + (n, t.shape[-1] // n))


def _head_ln(o, g):
    mu = jnp.mean(o, axis=-1, keepdims=True)
    c = o - mu
    return c * lax.rsqrt(jnp.mean(c * c, axis=-1, keepdims=True) + EPS) * g.reshape(o.shape[-2:])


def _blockdiag(w):
    n, c, d = w.shape
    eye = jnp.eye(n, dtype=w.dtype)
    return (eye[:, None, :, None] * w[:, :, None, :]).reshape(n * c, n * d)


def _rope_tables(pos):
    half = RET_DK // 2
    inv = ROPE_BASE ** (-jnp.linspace(0.0, 1.0, half, dtype=F32))
    ang = pos[:, None] * inv[None, :]
    cos, sin = jnp.cos(ang), jnp.sin(ang)
    return jnp.concatenate([cos, cos], axis=-1), jnp.concatenate([-sin, sin], axis=-1)


def _layer(x, B, T, tabs, st, mem_k, mem_v, p):
    M = B * T
    tm = min(M, 1024)
    P = _normproj(x, p['norm_g'], p['w_pack'], tm, PROJ_TN)

    def z(name):
        return P[:, _OFF[name]:_OFF[name] + BW]

    raw = P[:, _OFF['rwkv_shift']:_OFF['rwkv_shift'] + RWKV_SHIFT_W].reshape(B, T, RWKV_SHIFT_W)
    prev = jnp.concatenate([st['rwkv_shift'], raw[:, :-1]], axis=1)
    mixed = raw + (prev - raw) * p['rwkv_mu']
    r, k, v, wd, ad = jnp.split(mixed, [BW, 2 * BW, 3 * BW, 3 * BW + RWKV_LORA], axis=-1)
    tms = min(M, 2048)
    lw = -RWKV_DECAY_SCALE * _sigmoid(p['rwkv_w0'] + _matmul(jnp.tanh(wd).reshape(M, RWKV_LORA), p['rwkv_w2'], tms).reshape(B, T, BW))
    a = _sigmoid(p['rwkv_a0'] + _matmul(ad.reshape(M, RWKV_LORA), p['rwkv_a2'], tms).reshape(B, T, BW))
    kkh = _heads(k * p['rwkv_k_k'], RWKV_H)
    kk = (kkh / jnp.maximum(jnp.sqrt(jnp.sum(kkh * kkh, axis=-1, keepdims=True)), 1e-12)).reshape(B, T, BW)
    k2 = k * (1.0 + (a - 1.0) * p['rwkv_k_a'])
    y, s_rwkv = _rwkv_scan(r, lw, k2, v, kk, kk * a, st['rwkv'])
    rh, kh, vh = (_heads(t_, RWKV_H) for t_ in (r, k2, v))
    bonus = jnp.sum(rh * kh * p['rwkv_r_k'].reshape(RWKV_H, RWKV_D), axis=-1, keepdims=True) * vh
    o_a = (_head_ln(_heads(y, RWKV_H), p['rwkv_ln_g']) + bonus).reshape(M, BW)
    shift_new = raw[:, -1:]

    o_b, s_gla_t = _gla(P, B, T, p['gla_g2p'], p['gla_gb'], p['gla_norm_g'], jnp.swapaxes(st['gla'], -1, -2))
    s_gla = jnp.swapaxes(s_gla_t, -1, -2)

    o_c, s_ret = _ret(P, B, T, tabs[0], tabs[1], st['ret'])

    xm = P[:, _OFF['ml_x']:_OFF['ml_x'] + BW].reshape(B, T, BW)
    xpad = jnp.concatenate([st['ml_conv'], xm], axis=1)
    conv = sum(xpad[:, j:j + T] * p['ml_conv_w'][j] for j in range(ML_CONV)) + p['ml_conv_b']
    xc = _silu(conv).reshape(M, BW)
    q_m, k_m, v_m, gates = _mlstm_pre(xc, xm.reshape(M, BW), p['ml_wq'], p['ml_wk'], p['ml_wv'],
                                      p['ml_wif'], p['ml_bif'], min(M, 512))
    L = min(T, CHUNK)
    gcol = jnp.concatenate([gates[:, :ML_H], _log_sigmoid(gates[:, ML_H:2 * ML_H])], axis=-1).reshape(B, T // L, L, 2 * ML_H)
    grow = jnp.swapaxes(gcol, -1, -2)
    o_d, c_new, n_new, m_new = _mlstm(
        q_m, k_m, v_m, gcol, grow, xc, p['ml_norm_g'], p['ml_skip'], st['ml_c'], st['ml_n'][:, :, None, :],
        jnp.broadcast_to(st['ml_m'][:, :, None, None], st['ml_m'].shape + (1, LANE)), B, T)
    conv_new = xpad[:, -(ML_CONV - 1):]

    o_x = _xattn(P, B, T, mem_k.reshape(B, -1, BW), mem_v.reshape(B, -1, BW))

    znames = ('rwkv_z', 'gla_z', 'ret_z', 'ml_z', 'xa_z')
    ogs = [_bf(o * _silu(z(nm))) for o, nm in zip((o_a, o_b, o_c, o_d, o_x), znames)]
    merged = _merge(ogs, P, p['w_branch'], min(M, 512))
    x_new = _resid_out(x, merged, p['w_out'], min(M, 1024), 512)
    new = {'rwkv': s_rwkv, 'rwkv_shift': shift_new, 'gla': s_gla, 'ret': s_ret, 'ml_c': c_new,
           'ml_n': n_new[:, :, 0, :], 'ml_m': m_new[:, :, 0, 0], 'ml_conv': conv_new}
    return x_new, new


_STATE_KEYS = ('rwkv', 'rwkv_shift', 'gla', 'ret', 'ml_c', 'ml_n', 'ml_m', 'ml_conv')


def kernel(x_prompt, x_sample, mem_prompt, cache_mem_k, cache_mem_v, state_rwkv, state_rwkv_shift, state_gla, state_ret, state_mlstm_c, state_mlstm_n, state_mlstm_m, state_mlstm_conv, norm_g, mem_norm_g, w_in, w_mem_kv, rwkv_mu, rwkv_w0, rwkv_w2, rwkv_a0, rwkv_a2, rwkv_k_k, rwkv_k_a, rwkv_r_k, rwkv_ln_g, gla_g2, gla_gb, gla_norm_g, ml_conv_w, ml_conv_b, ml_wq, ml_wk, ml_wv, ml_w_if, ml_b_if, ml_skip, ml_norm_g, w_branch, w_out, final_norm_g):
    Bp, Tp, D = x_prompt.shape
    Bs, Ts, _ = x_sample.shape
    depth = w_in.shape[0]
    n_mem = mem_prompt.shape[1]

    params = {
        'norm_g': norm_g[:, None, :], 'mem_norm_g': mem_norm_g[:, None, :],
        'w_pack': _pack_w_in(w_in), 'w_mem_kv': _bf(w_mem_kv),
        'rwkv_mu': rwkv_mu, 'rwkv_w0': rwkv_w0, 'rwkv_w2': _bf(rwkv_w2), 'rwkv_a0': rwkv_a0, 'rwkv_a2': _bf(rwkv_a2),
        'rwkv_k_k': rwkv_k_k, 'rwkv_k_a': rwkv_k_a, 'rwkv_r_k': rwkv_r_k, 'rwkv_ln_g': rwkv_ln_g,
        'gla_g2p': _bf(jnp.pad(gla_g2, ((0, 0), (0, LANE - GLA_LORA), (0, 0)))), 'gla_gb': gla_gb[:, None, :],
        'gla_norm_g': gla_norm_g[:, None, :],
        'ml_conv_w': ml_conv_w, 'ml_conv_b': ml_conv_b,
        'ml_wq': _bf(jax.vmap(_blockdiag)(ml_wq)), 'ml_wk': _bf(jax.vmap(_blockdiag)(ml_wk)),
        'ml_wv': _bf(jax.vmap(_blockdiag)(ml_wv)),
        'ml_wif': _bf(jnp.pad(ml_w_if.reshape(depth, 3, BW, 2 * ML_H), ((0, 0), (0, 0), (0, 0), (0, LANE - 2 * ML_H)))),
        'ml_bif': jnp.pad(ml_b_if, ((0, 0), (0, LANE - 2 * ML_H)))[:, None, :],
        'ml_skip': ml_skip[:, None, :], 'ml_norm_g': ml_norm_g[:, None, :],
        'w_branch': _bf(w_branch), 'w_out': _bf(w_out),
    }
    cache = {'rwkv': state_rwkv, 'rwkv_shift': state_rwkv_shift, 'gla': state_gla, 'ret': state_ret,
             'ml_c': state_mlstm_c, 'ml_n': state_mlstm_n, 'ml_m': state_mlstm_m, 'ml_conv': state_mlstm_conv,
             'mem_k': cache_mem_k, 'mem_v': cache_mem_v}
    tabs_p = _rope_tables(jnp.arange(Tp, dtype=F32))
    tabs_s = _rope_tables(PAST_LEN + jnp.arange(Ts, dtype=F32))
    zero_p = {
        'rwkv': jnp.zeros((Bp, RWKV_H, RWKV_D, RWKV_D), F32), 'rwkv_shift': jnp.zeros((Bp, 1, RWKV_SHIFT_W), F32),
        'gla': jnp.zeros((Bp, GLA_H, GLA_DK, GLA_DV), F32), 'ret': jnp.zeros((Bp, RET_H, RET_DK, RET_DV), F32),
        'ml_c': jnp.zeros((Bp, ML_H, ML_D, ML_D), F32), 'ml_n': jnp.zeros((Bp, ML_H, ML_D), F32),
        'ml_m': jnp.zeros((Bp, ML_H), F32), 'ml_conv': jnp.zeros((Bp, ML_CONV - 1, BW), F32),
    }
    mem2d = mem_prompt.reshape(Bp * n_mem, D)

    def body(carry, xs):
        yp, ys = carry
        p, c = xs
        kv = _normproj(mem2d, p['mem_norm_g'], p['w_mem_kv'], min(Bp * n_mem, 512), 512)
        mk = kv[:, :BW].reshape(Bp, n_mem, XA_H, XA_D)
        mv = kv[:, BW:].reshape(Bp, n_mem, XA_H, XA_D)
        yp, stp = _layer(yp, Bp, Tp, tabs_p, zero_p, mk, mv, p)
        ys, sts = _layer(ys, Bs, Ts, tabs_s, {nm: c[nm] for nm in _STATE_KEYS}, c['mem_k'], c['mem_v'], p)
        return (yp, ys), (stp, mk, mv, sts)

    (yp, ys), (stp, mk, mv, sts) = lax.scan(
        body, (x_prompt.reshape(Bp * Tp, D), x_sample.reshape(Bs * Ts, D)), (params, cache))
    fg = final_norm_g[None, :]
    y_prompt = _rmsnorm(yp, fg, min(Bp * Tp, 1024)).reshape(Bp, Tp, D)
    y_sample = _rmsnorm(ys, fg, min(Bs * Ts, 1024)).reshape(Bs, Ts, D)
    return (y_prompt, y_sample,
            stp['rwkv'], stp['rwkv_shift'], stp['gla'], stp['ret'], stp['ml_c'], stp['ml_n'], stp['ml_m'],
            stp['ml_conv'], mk, mv,
            sts['rwkv'], sts['rwkv_shift'], sts['gla'], sts['ret'], sts['ml_c'], sts['ml_n'], sts['ml_m'],
            sts['ml_conv'])
```

```python
import functools
import math

import jax
import jax.numpy as jnp
from jax import lax
from jax.experimental import pallas as pl
from jax.experimental.pallas import tpu as pltpu

F32 = jnp.float32
BF16 = jnp.bfloat16

D_MODEL = 2048
BW = 1024
EPS = 1e-6
CHUNK = 64
N_BRANCH = 5
PAST_LEN = 2048
RWKV_H, RWKV_D, RWKV_LORA = 16, 64, 64
RWKV_SHIFT_W = 3 * BW + 2 * RWKV_LORA
RWKV_DECAY_SCALE = 0.606531
GLA_H, GLA_DK, GLA_DV, GLA_LORA = 4, 128, 256, 16
GLA_GATE_NORM = 16.0
RET_H, RET_DK, RET_DV = 4, 128, 256
ROPE_BASE = 10000.0
ML_H, ML_D, ML_CONV, ML_QK_BLOCK = 4, 256, 4, 4
XA_H, XA_D = 4, 256
LANE = 128
VMEM_LIMIT = 56 * 1024 * 1024

_IN_LAYOUT = (
    ('rwkv_shift', RWKV_SHIFT_W), ('rwkv_z', BW), ('gla_q', 512), ('gla_k', 512), ('gla_v', BW),
    ('gla_gd', GLA_LORA), ('gla_z', BW), ('ret_q', 512), ('ret_k', 512), ('ret_v', BW), ('ret_z', BW),
    ('ml_x', BW), ('ml_z', BW), ('xa_q', BW), ('xa_z', BW), ('gates', N_BRANCH * D_MODEL),
)
_PACK_ORDER = ('gates', 'rwkv_z', 'gla_z', 'ret_z', 'ml_z', 'xa_z', 'ret_q', 'ret_k', 'ret_v',
               'gla_q', 'gla_k', 'gla_v', 'ml_x', 'xa_q', 'rwkv_shift', 'gla_gd')
PROJ_TN = 512


def _src_cols(name):
    start = 0
    for nm, size in _IN_LAYOUT:
        if nm == name:
            return start, start + size
        start += size
    raise KeyError(name)


def _pack_offsets():
    off, cur = {}, 0
    for nm in _PACK_ORDER:
        a, b = _src_cols(nm)
        off[nm] = cur
        cur += -(-(b - a) // LANE) * LANE
    total = -(-cur // PROJ_TN) * PROJ_TN
    return off, total


_OFF, NP = _pack_offsets()


def _pack_w_in(w_in):
    parts, cur = [], 0
    for nm in _PACK_ORDER:
        a, b = _src_cols(nm)
        parts.append(w_in[:, :, a:b])
        width = -(-(b - a) // LANE) * LANE
        if width != b - a:
            parts.append(jnp.zeros(w_in.shape[:2] + (width - (b - a),), w_in.dtype))
        cur += width
    if NP != cur:
        parts.append(jnp.zeros(w_in.shape[:2] + (NP - cur,), w_in.dtype))
    return jnp.concatenate(parts, axis=-1).astype(BF16)


def _cparams(*sem):
    return pltpu.CompilerParams(dimension_semantics=sem, vmem_limit_bytes=VMEM_LIMIT)


def _dot(a, b):
    return jnp.dot(a, b, preferred_element_type=F32)


def _dot_nt(a, b):
    return lax.dot_general(a, b, (((1,), (1,)), ((), ())), preferred_element_type=F32)


def _dot_tn(a, b):
    return lax.dot_general(a, b, (((0,), (0,)), ((), ())), preferred_element_type=F32)


def _bf(x):
    return x.astype(BF16)


def _split3(x):
    hi = x.astype(BF16)
    r1 = x - hi.astype(F32)
    mid = r1.astype(BF16)
    lo = (r1 - mid.astype(F32)).astype(BF16)
    return hi, mid, lo


def _cumsum_rows(tri, x):
    hi, mid, lo = _split3(x)
    return _dot(tri, hi) + _dot(tri, mid) + _dot(tri, lo)


def _cumsum_cols(x, triu):
    hi, mid, lo = _split3(x)
    return _dot(hi, triu) + _dot(mid, triu) + _dot(lo, triu)


def _iota2(shape, dim):
    return lax.broadcasted_iota(jnp.int32, shape, dim)


def _log_sigmoid(x):
    return jnp.minimum(x, 0.0) - jnp.log1p(jnp.exp(-jnp.abs(x)))


def _sigmoid(x):
    return 1.0 / (1.0 + jnp.exp(-x))


def _silu(x):
    return x * _sigmoid(x)


def _normproj_kernel(x_ref, g_ref, w_ref, o_ref, h_ref):
    @pl.when(pl.program_id(1) == 0)
    def _():
        x = x_ref[...]
        ms = jnp.mean(x * x, axis=-1, keepdims=True)
        h_ref[...] = ((x * lax.rsqrt(ms + EPS)) * g_ref[...]).astype(BF16)

    o_ref[...] = _dot(h_ref[...], w_ref[...])


def _normproj(x, g, w, tm, tn):
    M, K = x.shape
    N = w.shape[1]
    return pl.pallas_call(
        _normproj_kernel,
        grid=(M // tm, N // tn),
        in_specs=[pl.BlockSpec((tm, K), lambda i, j: (i, 0)),
                  pl.BlockSpec((1, K), lambda i, j: (0, 0)),
                  pl.BlockSpec((K, tn), lambda i, j: (0, j))],
        out_specs=pl.BlockSpec((tm, tn), lambda i, j: (i, j)),
        out_shape=jax.ShapeDtypeStruct((M, N), F32),
        scratch_shapes=[pltpu.VMEM((tm, K), BF16)],
        compiler_params=_cparams("parallel", "arbitrary"),
        name="normproj",
    )(x, g, w)


def _rwkv_kernel(xr_ref, xk_ref, xv_ref, xwa_ref, z_ref, sp_ref, mu_ref, w2_ref, a2_ref, w0_ref, a0_ref,
                 kkw_ref, ka_ref, rk_ref, lng_ref, s0_ref, og_ref, s_ref,
                 s_scr, phi_scr, psi_scr, qt_scr, egl_scr, carry_scr,
                 r_scr, lw_scr, k_scr, v_scr, kk_scr, b_scr, y_scr, *, L):
    t = pl.program_id(1)
    HG, D = RWKV_H, RWKV_D
    hs = range(HG)
    sls = [slice(h * D, (h + 1) * D) for h in hs]

    @pl.when(t == 0)
    def _():
        s_scr[...] = s0_ref[0]
        carry_scr[...] = sp_ref[0]

    def cat(parts):
        return jnp.concatenate(parts, axis=1)

    tt = xr_ref.shape[0]
    first = _iota2((tt, 1), 0) == 0

    def mix(x_ref, lo, hi):
        x = x_ref[...]
        prev = jnp.where(first, carry_scr[:, lo:hi], pltpu.roll(x, 1, 0))
        carry_scr[:, lo:hi] = x[tt - 1:tt, :]
        return x + (prev - x) * mu_ref[:, lo:hi]

    r_ = mix(xr_ref, 0, BW)
    k_ = mix(xk_ref, BW, 2 * BW)
    v_ = mix(xv_ref, 2 * BW, 3 * BW)
    wa = mix(xwa_ref, 3 * BW, RWKV_SHIFT_W)
    lw_scr[...] = -RWKV_DECAY_SCALE * _sigmoid(w0_ref[...] + _dot(_bf(jnp.tanh(wa)), w2_ref[...]))
    a_ = _sigmoid(a0_ref[...] + _dot(_bf(wa), a2_ref[...]))
    ones_blk = (_iota2((LANE, LANE), 0) // D == _iota2((LANE, LANE), 1) // D).astype(BF16)

    def headsum(x):
        hi = _bf(x)
        lo = _bf(x - hi.astype(F32))
        return cat([_dot(hi[:, j:j + LANE], ones_blk) + _dot(lo[:, j:j + LANE], ones_blk)
                    for j in range(0, BW, LANE)])

    kkf = k_ * kkw_ref[...]
    kk = kkf / jnp.maximum(jnp.sqrt(headsum(kkf * kkf)), 1e-12)
    r_scr[...] = r_
    k_scr[...] = k_ * (1.0 + (a_ - 1.0) * ka_ref[...])
    v_scr[...] = v_
    kk_scr[...] = kk
    b_scr[...] = kk * a_

    nchunk = tt // L
    row, col = _iota2((L, L), 0), _iota2((L, L), 1)
    tri = (row >= col).astype(BF16)
    strict, incl = row > col, row >= col
    eye = (row == col).astype(F32)

    def phase1(c, carry):
        rows = pl.ds(pl.multiple_of(c * L, L), L)
        r_c, lw, k_c, v_c = r_scr[rows, :], lw_scr[rows, :], k_scr[rows, :], v_scr[rows, :]
        kk_c, b_c = kk_scr[rows, :], b_scr[rows, :]
        g = _cumsum_rows(tri, lw)
        gl = g[L - 1:L, :]
        kkp = kk_c * jnp.exp(g - lw)
        rg = r_c * jnp.exp(g)
        eng = jnp.exp(-g)
        kn, bn = k_c * eng, b_c * eng
        ee = jnp.exp(gl - g)
        kend, bend = k_c * ee, b_c * ee
        egl_scr[c] = jnp.exp(gl)
        kkp_h = [kkp[:, s_] for s_ in sls]
        rg_h = [rg[:, s_] for s_ in sls]
        bk = [_bf(jnp.concatenate([bn[:, s_], kn[:, s_]], axis=0)) for s_ in sls]
        mn = [_dot_nt(_bf(kkp_h[h]), bk[h]) for h in hs]
        aq = [_dot_nt(_bf(rg_h[h]), bk[h]) for h in hs]
        N = [_bf(jnp.where(strict, m[:, L:], 0.0)) for m in mn]
        vb = [_bf(v_c[:, s_]) for s_ in sls]
        nv = [_dot(N[h], vb[h]) for h in hs]
        P = [jnp.where(strict, -m[:, :L], 0.0) for m in mn]
        tinv = [eye + p for p in P]
        n = 1
        while 2 * n < L:
            Pb = [_bf(p) for p in P]
            P = [_dot(pb, pb) for pb in Pb]
            tinv = [t_ + _dot(_bf(t_), _bf(p)) for t_, p in zip(tinv, P)]
            n *= 2
        gh = [_dot(_bf(tinv[h]), _bf(jnp.concatenate([kkp_h[h], nv[h]], axis=1))) for h in hs]
        gm = [_bf(x[:, :D]) for x in gh]
        hm = [_bf(x[:, D:]) for x in gh]
        bend_b = [_bf(bend[:, s_]) for s_ in sls]
        kend_b = [_bf(kend[:, s_]) for s_ in sls]
        aqb = [_bf(jnp.where(incl, a[:, :L], 0.0)) for a in aq]
        aqk = [_bf(jnp.where(incl, a[:, L:], 0.0)) for a in aq]
        phi = [_dot_tn(gm[h], bend_b[h]) for h in hs]
        psi = [_dot_tn(vb[h], kend_b[h]) - _dot_tn(hm[h], bend_b[h]) for h in hs]
        qts = [rg_h[h] - _dot(aqb[h], gm[h]) for h in hs]
        y0s = [_dot(aqk[h], vb[h]) - _dot(aqb[h], hm[h]) for h in hs]
        for h in hs:
            phi_scr[c, h] = _bf(phi[h])
            psi_scr[c, h] = psi[h]
        qt_scr[rows, :] = _bf(cat(qts))
        y_scr[rows, :] = cat(y0s)
        return carry

    lax.fori_loop(0, nchunk, phase1, 0)

    def phase2(c, carry):
        rows = pl.ds(pl.multiple_of(c * L, L), L)
        qt = qt_scr[rows, :]
        egl = egl_scr[c]
        S = [s_scr[h] for h in hs]
        Sb = [_bf(s_) for s_ in S]
        sphi = [_dot(Sb[h], phi_scr[c, h]) for h in hs]
        ys = [_dot_nt(qt[:, sls[h]], Sb[h]) for h in hs]
        for h in hs:
            s_scr[h] = S[h] * egl[:, sls[h]] - sphi[h] + psi_scr[c, h]
        y_scr[rows, :] += cat(ys)
        return carry

    lax.fori_loop(0, nchunk, phase2, 0)

    y = y_scr[...]
    rk = r_scr[...] * k_scr[...] * rk_ref[...]
    cen = y - headsum(y) * (1.0 / D)
    ln = cen * lax.rsqrt(headsum(cen * cen) * (1.0 / D) + EPS)
    out = ln * lng_ref[...] + headsum(rk) * v_scr[...]
    og_ref[...] = _bf(out * _silu(z_ref[...]))

    @pl.when(t == pl.num_programs(1) - 1)
    def _():
        s_ref[0] = s_scr[...]


def _rwkv(P, B, T, p, s0, shift_prev):
    L = min(T, CHUNK)
    tt = min(T, 256)
    nT = T // tt
    nc = tt // L
    off = _OFF['rwkv_shift']
    rowblk = lambda w, o: pl.BlockSpec((tt, w), lambda b_, t: (b_ * nT + t, o // w))
    const = lambda shape: pl.BlockSpec(shape, lambda b_, t: (0,) * len(shape))
    st = pl.BlockSpec((1, RWKV_H, RWKV_D, RWKV_D), lambda b_, t: (b_, 0, 0, 0))
    vec = const((1, BW))
    return pl.pallas_call(
        functools.partial(_rwkv_kernel, L=L),
        grid=(B, nT),
        in_specs=[rowblk(BW, off), rowblk(BW, off + BW), rowblk(BW, off + 2 * BW), rowblk(LANE, off + 3 * BW),
                  rowblk(BW, _OFF['rwkv_z']),
                  pl.BlockSpec((1, 1, RWKV_SHIFT_W), lambda b_, t: (b_, 0, 0)), const((1, RWKV_SHIFT_W)),
                  const((LANE, BW)), const((LANE, BW)), vec, vec, vec, vec, vec, vec, st],
        out_specs=[pl.BlockSpec((tt, BW), lambda b_, t: (b_ * nT + t, 0)), st],
        out_shape=[jax.ShapeDtypeStruct((B * T, BW), BF16), jax.ShapeDtypeStruct(s0.shape, F32)],
        scratch_shapes=[pltpu.VMEM((RWKV_H, RWKV_D, RWKV_D), F32),
                        pltpu.VMEM((nc, RWKV_H, RWKV_D, RWKV_D), BF16),
                        pltpu.VMEM((nc, RWKV_H, RWKV_D, RWKV_D), F32),
                        pltpu.VMEM((tt, BW), BF16),
                        pltpu.VMEM((nc, 1, BW), F32),
                        pltpu.VMEM((1, RWKV_SHIFT_W), F32)] + [pltpu.VMEM((tt, BW), F32)] * 7,
        compiler_params=_cparams("parallel", "arbitrary"),
        name="rwkv_scan",
    )(P, P, P, P, P, shift_prev, p['rwkv_mu'], p['rwkv_w2p'], p['rwkv_a2p'], p['rwkv_w0'], p['rwkv_a0'],
      p['rwkv_k_k'], p['rwkv_k_a'], p['rwkv_r_k'], p['rwkv_ln_g'], s0)


def _gla_kernel(q_ref, k_ref, v_ref, gd_ref, z_ref, g2_ref, gb_ref, ng_ref, s0_ref, o_ref, s_ref, s_scr, g_scr, *, L):
    t = pl.program_id(1)

    @pl.when(t == 0)
    def _():
        s_scr[...] = s0_ref[0]

    gk = _dot(_bf(gd_ref[...]), g2_ref[...]) + gb_ref[...]
    g_scr[...] = _log_sigmoid(gk) * (1.0 / GLA_GATE_NORM)

    nchunk = q_ref.shape[0] // L
    row, col = _iota2((L, L), 0), _iota2((L, L), 1)
    tri = (row >= col).astype(BF16)
    causal = row >= col
    scale = GLA_DK ** -0.5

    def chunk(c, carry):
        rows = pl.ds(pl.multiple_of(c * L, L), L)
        hs = range(GLA_H)
        vsl = [slice(h * GLA_DV, (h + 1) * GLA_DV) for h in hs]
        q = q_ref[rows, :] * scale
        k = k_ref[rows, :]
        b = _cumsum_rows(tri, g_scr[rows, :])
        b_end = b[L - 1:L, :]
        qi_all = _bf(q * jnp.exp(b))
        ki_all = _bf(k * jnp.exp(-b))
        kend_all = _bf(k * jnp.exp(b_end - b))
        eb_end = jnp.exp(b_end)
        ksl = [slice(h * GLA_DK, (h + 1) * GLA_DK) for h in hs]
        qi = [qi_all[:, s] for s in ksl]
        vb = [_bf(v_ref[rows, s]) for s in vsl]
        St = [s_scr[h] for h in hs]
        A = [_bf(jnp.where(causal, _dot_nt(qi[h], ki_all[:, ksl[h]]), 0.0)) for h in hs]
        o_state = [_dot_nt(qi[h], _bf(St[h])) for h in hs]
        kv = [_dot_tn(vb[h], kend_all[:, ksl[h]]) for h in hs]
        o = [_dot(A[h], vb[h]) + o_state[h] for h in hs]
        for h in hs:
            s_scr[h] = St[h] * eb_end[:, ksl[h]] + kv[h]
            ms = jnp.mean(o[h] * o[h], axis=-1, keepdims=True)
            o_ref[rows, vsl[h]] = _bf((o[h] * lax.rsqrt(ms + EPS)) * ng_ref[:, vsl[h]] * _silu(z_ref[rows, vsl[h]]))
        return carry

    lax.fori_loop(0, nchunk, chunk, 0)

    @pl.when(t == pl.num_programs(1) - 1)
    def _():
        s_ref[0] = s_scr[...]


def _gla(P, B, T, g2p, gb, ng, s0t):
    L = min(T, CHUNK)
    tt = min(T, 512)
    nT = T // tt
    rowblk = lambda w, off: pl.BlockSpec((tt, w), lambda b_, t: (b_ * nT + t, off // w))
    const = lambda shape: pl.BlockSpec(shape, lambda b_, t: (0,) * len(shape))
    st = pl.BlockSpec((1, GLA_H, GLA_DV, GLA_DK), lambda b_, t: (b_, 0, 0, 0))
    return pl.pallas_call(
        functools.partial(_gla_kernel, L=L),
        grid=(B, nT),
        in_specs=[rowblk(512, _OFF['gla_q']), rowblk(512, _OFF['gla_k']), rowblk(BW, _OFF['gla_v']),
                  rowblk(LANE, _OFF['gla_gd']), rowblk(BW, _OFF['gla_z']), const((LANE, 512)), const((1, 512)),
                  const((1, BW)), st],
        out_specs=[pl.BlockSpec((tt, BW), lambda b_, t: (b_ * nT + t, 0)), st],
        out_shape=[jax.ShapeDtypeStruct((B * T, BW), BF16), jax.ShapeDtypeStruct(s0t.shape, F32)],
        scratch_shapes=[pltpu.VMEM((GLA_H, GLA_DV, GLA_DK), F32), pltpu.VMEM((tt, 512), F32)],
        compiler_params=_cparams("parallel", "arbitrary"),
        name="gla_scan",
    )(P, P, P, P, P, g2p, gb, ng, s0t)


def _ret_kernel(q_ref, k_ref, v_ref, z_ref, cos_ref, sin_ref, s0_ref, o_ref, s_ref, s_scr, *, L):
    t = pl.program_id(1)

    @pl.when(t == 0)
    def _():
        s_scr[...] = s0_ref[0]

    nchunk = q_ref.shape[0] // L
    rel = (_iota2((L, L), 0) - _iota2((L, L), 1)).astype(F32)
    tcol = _iota2((L, 1), 0).astype(F32)
    scale = RET_DK ** -0.5
    lgs = [math.log(1.0 - 2.0 ** (-5.0 - h)) for h in range(RET_H)]
    decay = [jnp.where(rel >= 0, jnp.exp(lg * jnp.maximum(rel, 0.0)), 0.0) for lg in lgs]
    rowd = [jnp.exp(lg * (tcol + 1.0)) for lg in lgs]
    cold = [jnp.exp(lg * (L - 1.0 - tcol)) for lg in lgs]

    def rope(x, cos2, sin2):
        return x * cos2 + pltpu.roll(x, RET_DK // 2, 1) * sin2

    def chunk(c, carry):
        rows = pl.ds(pl.multiple_of(c * L, L), L)
        cos2, sin2 = cos_ref[rows, :], sin_ref[rows, :]
        hs = range(RET_H)
        ksl = [slice(h * RET_DK, (h + 1) * RET_DK) for h in hs]
        vsl = [slice(h * RET_DV, (h + 1) * RET_DV) for h in hs]
        k = [rope(k_ref[rows, s], cos2, sin2) for s in ksl]
        qb = [_bf(rope(q_ref[rows, s], cos2, sin2) * scale) for s in ksl]
        kb = [_bf(x) for x in k]
        kcb = [_bf(k[h] * cold[h]) for h in hs]
        vb = [_bf(v_ref[rows, s]) for s in vsl]
        S = [s_scr[h] for h in hs]
        A = [_bf(_dot_nt(qb[h], kb[h]) * decay[h]) for h in hs]
        o_state = [_dot(qb[h], _bf(S[h])) for h in hs]
        kv = [_dot_tn(kcb[h], vb[h]) for h in hs]
        o = [_dot(A[h], vb[h]) + rowd[h] * o_state[h] for h in hs]
        for h in hs:
            s_scr[h] = math.exp(lgs[h] * L) * S[h] + kv[h]
            ms = jnp.mean(o[h] * o[h], axis=-1, keepdims=True)
            o_ref[rows, vsl[h]] = _bf(o[h] * lax.rsqrt(ms + EPS) * _silu(z_ref[rows, vsl[h]]))
        return carry

    lax.fori_loop(0, nchunk, chunk, 0)

    @pl.when(t == pl.num_programs(1) - 1)
    def _():
        s_ref[0] = s_scr[...]


def _ret(P, B, T, cos2, sin2, s0):
    L = min(T, CHUNK)
    tt = min(T, 512)
    nT = T // tt
    rowblk = lambda w, off: pl.BlockSpec((tt, w), lambda b_, t: (b_ * nT + t, off // w))
    tab = pl.BlockSpec((tt, RET_DK), lambda b_, t: (t, 0))
    st = pl.BlockSpec((1, RET_H, RET_DK, RET_DV), lambda b_, t: (b_, 0, 0, 0))
    return pl.pallas_call(
        functools.partial(_ret_kernel, L=L),
        grid=(B, nT),
        in_specs=[rowblk(512, _OFF['ret_q']), rowblk(512, _OFF['ret_k']), rowblk(BW, _OFF['ret_v']),
                  rowblk(BW, _OFF['ret_z']), tab, tab, st],
        out_specs=[pl.BlockSpec((tt, BW), lambda b_, t: (b_ * nT + t, 0)), st],
        out_shape=[jax.ShapeDtypeStruct((B * T, BW), BF16), jax.ShapeDtypeStruct(s0.shape, F32)],
        scratch_shapes=[pltpu.VMEM((RET_H, RET_DK, RET_DV), F32)],
        compiler_params=_cparams("parallel", "arbitrary"),
        name="ret_scan",
    )(P, P, P, P, cos2, sin2, s0)


def _mlstm_pre_kernel(xm_ref, cp_ref, cw_ref, cb_ref, wq_ref, wk_ref, wv_ref, wif_ref, bif_ref,
                      xc_ref, q_ref, k_ref, v_ref, g_ref, carry_scr):
    t = pl.program_id(1)
    tm = xm_ref.shape[0]
    nprev = ML_CONV - 1

    @pl.when(t == 0)
    def _():
        carry_scr[...] = jnp.zeros_like(carry_scr)
        carry_scr[8 - nprev:8, :] = cp_ref[0]

    xm = xm_ref[...]
    c8 = carry_scr[...]
    rid = _iota2((8, 1), 0)
    conv = xm * cw_ref[nprev:nprev + 1, :] + cb_ref[...]
    for j in range(1, ML_CONV):
        rolled = pltpu.roll(xm, j, 0)
        head = jnp.where(rid < j, pltpu.roll(c8, j, 0), rolled[0:8, :])
        prev = jnp.concatenate([head, rolled[8:, :]], axis=0) if tm > 8 else head
        conv = conv + prev * cw_ref[nprev - j:nprev - j + 1, :]
    carry_scr[...] = xm[tm - 8:tm, :]
    xc = _silu(conv)
    xc_ref[...] = xc
    xcb, xmb = _bf(xc), _bf(xm)
    q = _dot(xcb, wq_ref[...])
    k = _dot(xcb, wk_ref[...])
    v = _dot(xmb, wv_ref[...])
    g_ref[...] = (_dot(_bf(q), wif_ref[0]) + _dot(_bf(k), wif_ref[1]) + _dot(_bf(v), wif_ref[2])) + bif_ref[...]
    q_ref[...] = q
    k_ref[...] = k * (ML_D ** -0.5)
    v_ref[...] = v


def _mlstm_pre(P, B, T, conv_prev, p):
    tm = min(T, 512)
    nT = T // tm
    row = pl.BlockSpec((tm, BW), lambda b_, t: (b_ * nT + t, 0))
    const = lambda shape: pl.BlockSpec(shape, lambda b_, t: (0,) * len(shape))
    M = B * T
    return pl.pallas_call(
        _mlstm_pre_kernel,
        grid=(B, nT),
        in_specs=[pl.BlockSpec((tm, BW), lambda b_, t: (b_ * nT + t, _OFF['ml_x'] // BW)),
                  pl.BlockSpec((1, ML_CONV - 1, BW), lambda b_, t: (b_, 0, 0)),
                  const((ML_CONV, BW)), const((1, BW)), const((BW, BW)), const((BW, BW)), const((BW, BW)),
                  const((3, BW, LANE)), const((1, LANE))],
        out_specs=[row, row, row, row, pl.BlockSpec((tm, LANE), lambda b_, t: (b_ * nT + t, 0))],
        out_shape=[jax.ShapeDtypeStruct((M, BW), F32)] * 4 + [jax.ShapeDtypeStruct((M, LANE), F32)],
        scratch_shapes=[pltpu.VMEM((8, BW), F32)],
        compiler_params=_cparams("parallel", "arbitrary"),
        name="mlstm_pre",
    )(P, conv_prev, p['ml_conv_w'], p['ml_conv_b'], p['ml_wq'], p['ml_wk'], p['ml_wv'], p['ml_wif'], p['ml_bif'])


def _mlstm_kernel(q_ref, k_ref, v_ref, gc_ref, gr_ref, xc_ref, z_ref, ng_ref, sk_ref, c0_ref, n0_ref, m0_ref,
                  o_ref, c_ref, n_ref, m_ref, c_scr, n_scr, m_scr, *, L):
    t = pl.program_id(1)

    @pl.when(t == 0)
    def _():
        c_scr[...] = c0_ref[0]
        n_scr[...] = n0_ref[0]
        m_scr[...] = m0_ref[0]

    nchunk = q_ref.shape[0] // L
    row, col = _iota2((L, L), 0), _iota2((L, L), 1)
    tri = (row >= col).astype(BF16)
    triu = (row <= col).astype(BF16)
    causal = row >= col

    def chunk(c, carry):
        rows = pl.ds(pl.multiple_of(c * L, L), L)
        gcb = gc_ref[0, c]
        grb = gr_ref[0, c]
        bc = _cumsum_rows(tri, gcb)
        br = _cumsum_cols(grb, triu)
        hs = range(ML_H)
        sl = [slice(h * ML_D, (h + 1) * ML_D) for h in hs]
        q = [q_ref[rows, s] for s in sl]
        k = [k_ref[rows, s] for s in sl]
        qb = [_bf(x) for x in q]
        vb = [_bf(v_ref[rows, s]) for s in sl]
        C = [c_scr[h] for h in hs]
        qk = [_dot_nt(qb[h], _bf(k[h])) for h in hs]
        qc = [_dot(qb[h], _bf(C[h])) for h in hs]
        b_c = [bc[:, ML_H + h:ML_H + h + 1] for h in hs]
        m_prev = [m_scr[h][:, 0:1] for h in hs]
        dlog = [jnp.where(causal, b_c[h] - br[ML_H + h:ML_H + h + 1, :] + grb[h:h + 1, :], -jnp.inf) for h in hs]
        from_state = [b_c[h] + m_prev[h] for h in hs]
        m_t = [jnp.maximum(from_state[h], jnp.max(dlog[h], axis=-1, keepdims=True)) for h in hs]
        s = [qk[h] * jnp.exp(dlog[h] - m_t[h]) for h in hs]
        m_new = [m_t[h][L - 1:L, :] for h in hs]
        b_end = [b_c[h][L - 1:L, :] for h in hs]
        kw = [k[h] * jnp.exp(b_end[h] - b_c[h] + gcb[:, h:h + 1] - m_new[h]) for h in hs]
        sv = [_dot(_bf(s[h]), vb[h]) for h in hs]
        kv = [_dot_tn(_bf(kw[h]), vb[h]) for h in hs]
        for h in hs:
            nvec = n_scr[h]
            w_state = jnp.exp(from_state[h] - m_t[h])
            num = sv[h] + w_state * qc[h]
            qn = jnp.sum(q[h] * nvec, axis=-1, keepdims=True)
            den = jnp.sum(s[h], axis=-1, keepdims=True) + w_state * qn
            hh = num / jnp.maximum(jnp.abs(den), jnp.exp(-m_t[h]))
            cd = jnp.exp(b_end[h] + m_prev[h] - m_new[h])
            c_scr[h] = cd * C[h] + kv[h]
            n_scr[h] = cd * nvec + jnp.sum(kw[h], axis=0, keepdims=True)
            m_scr[h] = jnp.broadcast_to(m_new[h], (1, LANE))
            mu = jnp.mean(hh, axis=-1, keepdims=True)
            cen = hh - mu
            y = cen * lax.rsqrt(jnp.mean(cen * cen, axis=-1, keepdims=True) + EPS)
            o_ref[rows, sl[h]] = _bf((y * ng_ref[:, sl[h]] + sk_ref[:, sl[h]] * xc_ref[rows, sl[h]])
                                     * _silu(z_ref[rows, sl[h]]))
        return carry

    lax.fori_loop(0, nchunk, chunk, 0)

    @pl.when(t == pl.num_programs(1) - 1)
    def _():
        c_ref[0] = c_scr[...]
        n_ref[0] = n_scr[...]
        m_ref[0] = m_scr[...]


def _mlstm(q, k, v, gc, gr, xc, P, ng, sk, c0, n0, m0, B, T):
    L = min(T, CHUNK)
    tt = min(T, 512)
    nT = T // tt
    nc = tt // L
    row = pl.BlockSpec((tt, BW), lambda b_, t: (b_ * nT + t, 0))
    const = pl.BlockSpec((1, BW), lambda b_, t: (0, 0))
    stc = pl.BlockSpec((1, ML_H, ML_D, ML_D), lambda b_, t: (b_, 0, 0, 0))
    stn = pl.BlockSpec((1, ML_H, 1, ML_D), lambda b_, t: (b_, 0, 0, 0))
    stm = pl.BlockSpec((1, ML_H, 1, LANE), lambda b_, t: (b_, 0, 0, 0))
    return pl.pallas_call(
        functools.partial(_mlstm_kernel, L=L),
        grid=(B, nT),
        in_specs=[row, row, row,
                  pl.BlockSpec((1, nc, L, 8), lambda b_, t: (b_, t, 0, 0)),
                  pl.BlockSpec((1, nc, 8, L), lambda b_, t: (b_, t, 0, 0)),
                  row, pl.BlockSpec((tt, BW), lambda b_, t: (b_ * nT + t, _OFF['ml_z'] // BW)), const, const, stc, stn, stm],
        out_specs=[row, stc, stn, stm],
        out_shape=[jax.ShapeDtypeStruct((B * T, BW), BF16), jax.ShapeDtypeStruct(c0.shape, F32),
                   jax.ShapeDtypeStruct(n0.shape, F32), jax.ShapeDtypeStruct(m0.shape, F32)],
        scratch_shapes=[pltpu.VMEM((ML_H, ML_D, ML_D), F32), pltpu.VMEM((ML_H, 1, ML_D), F32),
                        pltpu.VMEM((ML_H, 1, LANE), F32)],
        compiler_params=_cparams("parallel", "arbitrary"),
        name="mlstm_scan",
    )(q, k, v, gc, gr, xc, P, ng, sk, c0, n0, m0)


def _xattn_kernel(q_ref, z_ref, mk_ref, mv_ref, o_ref):
    hs = range(XA_H)
    sl = [slice(h * XA_D, (h + 1) * XA_D) for h in hs]
    s = [_dot_nt(_bf(q_ref[:, x]), _bf(mk_ref[0, :, x])) * (XA_D ** -0.5) for x in sl]
    e = [jnp.exp(x - jnp.max(x, axis=-1, keepdims=True)) for x in s]
    prob = [_bf(x / jnp.sum(x, axis=-1, keepdims=True)) for x in e]
    o = [_dot(prob[h], _bf(mv_ref[0, :, sl[h]])) for h in hs]
    for h in hs:
        o_ref[:, sl[h]] = _bf(o[h] * _silu(z_ref[:, sl[h]]))


def _xattn(P, B, T, mk, mv):
    tt = min(T, 512)
    nT = T // tt
    n_mem = mk.shape[1]
    mem = pl.BlockSpec((1, n_mem, BW), lambda b_, t: (b_, 0, 0))
    return pl.pallas_call(
        _xattn_kernel,
        grid=(B, nT),
        in_specs=[pl.BlockSpec((tt, BW), lambda b_, t: (b_ * nT + t, _OFF['xa_q'] // BW)),
                  pl.BlockSpec((tt, BW), lambda b_, t: (b_ * nT + t, _OFF['xa_z'] // BW)), mem, mem],
        out_specs=pl.BlockSpec((tt, BW), lambda b_, t: (b_ * nT + t, 0)),
        out_shape=jax.ShapeDtypeStruct((B * T, BW), BF16),
        compiler_params=_cparams("parallel", "parallel"),
        name="xattn",
    )(P, P, mk, mv)


def _merge_kernel(o0, o1, o2, o3, o4, g_ref, w_ref, out_ref, acc):
    j = pl.program_id(1)

    @pl.when(j == 0)
    def _():
        acc[...] = jnp.zeros_like(acc)

    for i, o_ref in enumerate((o0, o1, o2, o3, o4)):
        @pl.when(j == i)
        def _(o_ref=o_ref):
            acc[...] += _sigmoid(g_ref[...]) * _dot(o_ref[...], w_ref[0])

    @pl.when(j == N_BRANCH - 1)
    def _():
        out_ref[...] = acc[...].astype(BF16)


def _merge(ogs, P, wb, tm):
    M = ogs[0].shape[0]
    og = pl.BlockSpec((tm, BW), lambda i, j: (i, 0))
    return pl.pallas_call(
        _merge_kernel,
        grid=(M // tm, N_BRANCH),
        in_specs=[og] * N_BRANCH + [pl.BlockSpec((tm, D_MODEL), lambda i, j: (i, _OFF['gates'] // D_MODEL + j)),
                                    pl.BlockSpec((1, BW, D_MODEL), lambda i, j: (j, 0, 0))],
        out_specs=pl.BlockSpec((tm, D_MODEL), lambda i, j: (i, 0)),
        out_shape=jax.ShapeDtypeStruct((M, D_MODEL), BF16),
        scratch_shapes=[pltpu.VMEM((tm, D_MODEL), F32)],
        compiler_params=_cparams("parallel", "arbitrary"),
        name="merge",
    )(*ogs, P, wb)


def _resid_kernel(x_ref, m_ref, w_ref, o_ref):
    o_ref[...] = x_ref[...] + _dot(m_ref[...], w_ref[...])


def _resid_out(x, merged, w, tm, tn):
    M = x.shape[0]
    return pl.pallas_call(
        _resid_kernel,
        grid=(M // tm, D_MODEL // tn),
        in_specs=[pl.BlockSpec((tm, tn), lambda i, j: (i, j)),
                  pl.BlockSpec((tm, D_MODEL), lambda i, j: (i, 0)),
                  pl.BlockSpec((D_MODEL, tn), lambda i, j: (0, j))],
        out_specs=pl.BlockSpec((tm, tn), lambda i, j: (i, j)),
        out_shape=jax.ShapeDtypeStruct((M, D_MODEL), F32),
        compiler_params=_cparams("parallel", "parallel"),
        name="resid_out",
    )(x, merged, w)


def _rmsnorm_kernel(x_ref, g_ref, o_ref):
    x = x_ref[...]
    ms = jnp.mean(x * x, axis=-1, keepdims=True)
    o_ref[...] = (x * lax.rsqrt(ms + EPS)) * g_ref[...]


def _rmsnorm(x, g, tm):
    M, K = x.shape
    return pl.pallas_call(
        _rmsnorm_kernel,
        grid=(M // tm,),
        in_specs=[pl.BlockSpec((tm, K), lambda i: (i, 0)), pl.BlockSpec((1, K), lambda i: (0, 0))],
        out_specs=pl.BlockSpec((tm, K), lambda i: (i, 0)),
        out_shape=jax.ShapeDtypeStruct((M, K), F32),
        compiler_params=_cparams("parallel"),
        name="rmsnorm",
    )(x, g)


def _blockdiag(w):
    n, c, d = w.shape
    eye = jnp.eye(n, dtype=w.dtype)
    return (eye[:, None, :, None] * w[:, :, None, :]).reshape(n * c, n * d)


def _rope_tables(pos):
    half = RET_DK // 2
    inv = ROPE_BASE ** (-jnp.linspace(0.0, 1.0, half, dtype=F32))
    ang = pos[:, None] * inv[None, :]
    cos, sin = jnp.cos(ang), jnp.sin(ang)
    return jnp.concatenate([cos, cos], axis=-1), jnp.concatenate([-sin, sin], axis=-1)


def _layer(x, B, T, tabs, st, mem_k, mem_v, p):
    M = B * T
    tm = min(M, 1024)
    P = _normproj(x, p['norm_g'], p['w_pack'], tm, PROJ_TN)
    P3 = P.reshape(B, T, NP)

    og_a, s_rwkv = _rwkv(P, B, T, p, st['rwkv'], st['rwkv_shift'])
    shift_new = P3[:, T - 1:, _OFF['rwkv_shift']:_OFF['rwkv_shift'] + RWKV_SHIFT_W]

    og_b, s_gla_t = _gla(P, B, T, p['gla_g2p'], p['gla_gb'], p['gla_norm_g'], jnp.swapaxes(st['gla'], -1, -2))
    s_gla = jnp.swapaxes(s_gla_t, -1, -2)

    og_c, s_ret = _ret(P, B, T, tabs[0], tabs[1], st['ret'])

    xc, q_m, k_m, v_m, gates = _mlstm_pre(P, B, T, st['ml_conv'], p)
    L = min(T, CHUNK)
    gcol = jnp.concatenate([gates[:, :ML_H], _log_sigmoid(gates[:, ML_H:2 * ML_H])], axis=-1).reshape(B, T // L, L, 2 * ML_H)
    grow = jnp.swapaxes(gcol, -1, -2)
    og_d, c_new, n_new, m_new = _mlstm(
        q_m, k_m, v_m, gcol, grow, xc, P, p['ml_norm_g'], p['ml_skip'], st['ml_c'], st['ml_n'][:, :, None, :],
        jnp.broadcast_to(st['ml_m'][:, :, None, None], st['ml_m'].shape + (1, LANE)), B, T)
    xm_tail = P3[:, max(T - (ML_CONV - 1), 0):, _OFF['ml_x']:_OFF['ml_x'] + BW]
    conv_new = jnp.concatenate([st['ml_conv'], xm_tail], axis=1)[:, -(ML_CONV - 1):]

    og_x = _xattn(P, B, T, mem_k.reshape(B, -1, BW), mem_v.reshape(B, -1, BW))

    merged = _merge([og_a, og_b, og_c, og_d, og_x], P, p['w_branch'], min(M, 512))
    x_new = _resid_out(x, merged, p['w_out'], min(M, 1024), 512)
    new = {'rwkv': s_rwkv, 'rwkv_shift': shift_new, 'gla': s_gla, 'ret': s_ret, 'ml_c': c_new,
           'ml_n': n_new[:, :, 0, :], 'ml_m': m_new[:, :, 0, 0], 'ml_conv': conv_new}
    return x_new, new


_STATE_KEYS = ('rwkv', 'rwkv_shift', 'gla', 'ret', 'ml_c', 'ml_n', 'ml_m', 'ml_conv')


def kernel(x_prompt, x_sample, mem_prompt, cache_mem_k, cache_mem_v, state_rwkv, state_rwkv_shift, state_gla, state_ret, state_mlstm_c, state_mlstm_n, state_mlstm_m, state_mlstm_conv, norm_g, mem_norm_g, w_in, w_mem_kv, rwkv_mu, rwkv_w0, rwkv_w2, rwkv_a0, rwkv_a2, rwkv_k_k, rwkv_k_a, rwkv_r_k, rwkv_ln_g, gla_g2, gla_gb, gla_norm_g, ml_conv_w, ml_conv_b, ml_wq, ml_wk, ml_wv, ml_w_if, ml_b_if, ml_skip, ml_norm_g, w_branch, w_out, final_norm_g):
    Bp, Tp, D = x_prompt.shape
    Bs, Ts, _ = x_sample.shape
    depth = w_in.shape[0]
    n_mem = mem_prompt.shape[1]

    params = {
        'norm_g': norm_g[:, None, :], 'mem_norm_g': mem_norm_g[:, None, :],
        'w_pack': _pack_w_in(w_in), 'w_mem_kv': _bf(w_mem_kv),
        'rwkv_mu': rwkv_mu[:, None, :], 'rwkv_w0': rwkv_w0[:, None, :], 'rwkv_a0': rwkv_a0[:, None, :],
        'rwkv_w2p': _bf(jnp.pad(rwkv_w2, ((0, 0), (0, LANE - RWKV_LORA), (0, 0)))),
        'rwkv_a2p': _bf(jnp.pad(rwkv_a2, ((0, 0), (LANE - RWKV_LORA, 0), (0, 0)))),
        'rwkv_k_k': rwkv_k_k[:, None, :], 'rwkv_k_a': rwkv_k_a[:, None, :], 'rwkv_r_k': rwkv_r_k[:, None, :],
        'rwkv_ln_g': rwkv_ln_g[:, None, :],
        'gla_g2p': _bf(jnp.pad(gla_g2, ((0, 0), (0, LANE - GLA_LORA), (0, 0)))), 'gla_gb': gla_gb[:, None, :],
        'gla_norm_g': gla_norm_g[:, None, :],
        'ml_conv_w': ml_conv_w, 'ml_conv_b': ml_conv_b[:, None, :],
        'ml_wq': _bf(jax.vmap(_blockdiag)(ml_wq)), 'ml_wk': _bf(jax.vmap(_blockdiag)(ml_wk)),
        'ml_wv': _bf(jax.vmap(_blockdiag)(ml_wv)),
        'ml_wif': _bf(jnp.pad(ml_w_if.reshape(depth, 3, BW, 2 * ML_H), ((0, 0), (0, 0), (0, 0), (0, LANE - 2 * ML_H)))),
        'ml_bif': jnp.pad(ml_b_if, ((0, 0), (0, LANE - 2 * ML_H)))[:, None, :],
        'ml_skip': ml_skip[:, None, :], 'ml_norm_g': ml_norm_g[:, None, :],
        'w_branch': _bf(w_branch), 'w_out': _bf(w_out),
    }
    cache = {'rwkv': state_rwkv, 'rwkv_shift': state_rwkv_shift, 'gla': state_gla, 'ret': state_ret,
             'ml_c': state_mlstm_c, 'ml_n': state_mlstm_n, 'ml_m': state_mlstm_m, 'ml_conv': state_mlstm_conv,
             'mem_k': cache_mem_k, 'mem_v': cache_mem_v}
    tabs_p = _rope_tables(jnp.arange(Tp, dtype=F32))
    tabs_s = _rope_tables(PAST_LEN + jnp.arange(Ts, dtype=F32))
    zero_p = {
        'rwkv': jnp.zeros((Bp, RWKV_H, RWKV_D, RWKV_D), F32), 'rwkv_shift': jnp.zeros((Bp, 1, RWKV_SHIFT_W), F32),
        'gla': jnp.zeros((Bp, GLA_H, GLA_DK, GLA_DV), F32), 'ret': jnp.zeros((Bp, RET_H, RET_DK, RET_DV), F32),
        'ml_c': jnp.zeros((Bp, ML_H, ML_D, ML_D), F32), 'ml_n': jnp.zeros((Bp, ML_H, ML_D), F32),
        'ml_m': jnp.zeros((Bp, ML_H), F32), 'ml_conv': jnp.zeros((Bp, ML_CONV - 1, BW), F32),
    }
    mem2d = mem_prompt.reshape(Bp * n_mem, D)

    def body(carry, xs):
        yp, ys = carry
        p, c = xs
        kv = _normproj(mem2d, p['mem_norm_g'], p['w_mem_kv'], min(Bp * n_mem, 512), 512)
        mk = kv[:, :BW].reshape(Bp, n_mem, XA_H, XA_D)
        mv = kv[:, BW:].reshape(Bp, n_mem, XA_H, XA_D)
        yp, stp = _layer(yp, Bp, Tp, tabs_p, zero_p, mk, mv, p)
        ys, sts = _layer(ys, Bs, Ts, tabs_s, {nm: c[nm] for nm in _STATE_KEYS}, c['mem_k'], c['mem_v'], p)
        return (yp, ys), (stp, mk, mv, sts)

    (yp, ys), (stp, mk, mv, sts) = lax.scan(
        body, (x_prompt.reshape(Bp * Tp, D), x_sample.reshape(Bs * Ts, D)), (params, cache))
    fg = final_norm_g[None, :]
    y_prompt = _rmsnorm(yp, fg, min(Bp * Tp, 1024)).reshape(Bp, Tp, D)
    y_sample = _rmsnorm(ys, fg, min(Bs * Ts, 1024)).reshape(Bs, Ts, D)
    return (y_prompt, y_sample,
            stp['rwkv'], stp['rwkv_shift'], stp['gla'], stp['ret'], stp['ml_c'], stp['ml_n'], stp['ml_m'],
            stp['ml_conv'], mk, mv,
            sts['rwkv'], sts['rwkv_shift'], sts['gla'], sts['ret'], sts['ml_c'], sts['ml_n'], sts['ml_m'],
            sts['ml_conv'])
```

```python
import functools
import math

import jax
import jax.numpy as jnp
from jax import lax
from jax.experimental import pallas as pl
from jax.experimental.pallas import tpu as pltpu

F32 = jnp.float32
BF16 = jnp.bfloat16

D_MODEL = 2048
BW = 1024
EPS = 1e-6
CHUNK = 64
N_BRANCH = 5
PAST_LEN = 2048
RWKV_H, RWKV_D, RWKV_LORA = 16, 64, 64
RWKV_SHIFT_W = 3 * BW + 2 * RWKV_LORA
RWKV_DECAY_SCALE = 0.606531
GLA_H, GLA_DK, GLA_DV, GLA_LORA = 4, 128, 256, 16
GLA_GATE_NORM = 16.0
RET_H, RET_DK, RET_DV = 4, 128, 256
ROPE_BASE = 10000.0
ML_H, ML_D, ML_CONV, ML_QK_BLOCK = 4, 256, 4, 4
XA_H, XA_D = 4, 256
LANE = 128
VMEM_LIMIT = 56 * 1024 * 1024

_IN_LAYOUT = (
    ('rwkv_shift', RWKV_SHIFT_W), ('rwkv_z', BW), ('gla_q', 512), ('gla_k', 512), ('gla_v', BW),
    ('gla_gd', GLA_LORA), ('gla_z', BW), ('ret_q', 512), ('ret_k', 512), ('ret_v', BW), ('ret_z', BW),
    ('ml_x', BW), ('ml_z', BW), ('xa_q', BW), ('xa_z', BW), ('gates', N_BRANCH * D_MODEL),
)
_PACK_ORDER = ('gates', 'rwkv_z', 'gla_z', 'ret_z', 'ml_z', 'xa_z', 'ret_q', 'ret_k', 'ret_v',
               'gla_q', 'gla_k', 'gla_v', 'ml_x', 'xa_q', 'rwkv_shift', 'gla_gd')
PROJ_TN = 512


def _src_cols(name):
    start = 0
    for nm, size in _IN_LAYOUT:
        if nm == name:
            return start, start + size
        start += size
    raise KeyError(name)


def _pack_offsets():
    off, cur = {}, 0
    for nm in _PACK_ORDER:
        a, b = _src_cols(nm)
        off[nm] = cur
        cur += -(-(b - a) // LANE) * LANE
    total = -(-cur // PROJ_TN) * PROJ_TN
    return off, total


_OFF, NP = _pack_offsets()


def _pack_w_in(w_in):
    parts, cur = [], 0
    for nm in _PACK_ORDER:
        a, b = _src_cols(nm)
        parts.append(w_in[:, :, a:b])
        width = -(-(b - a) // LANE) * LANE
        if width != b - a:
            parts.append(jnp.zeros(w_in.shape[:2] + (width - (b - a),), w_in.dtype))
        cur += width
    if NP != cur:
        parts.append(jnp.zeros(w_in.shape[:2] + (NP - cur,), w_in.dtype))
    return jnp.concatenate(parts, axis=-1).astype(BF16)


def _cparams(*sem):
    return pltpu.CompilerParams(dimension_semantics=sem, vmem_limit_bytes=VMEM_LIMIT)


def _dot(a, b):
    return jnp.dot(a, b, preferred_element_type=F32)


def _dot_nt(a, b):
    return lax.dot_general(a, b, (((1,), (1,)), ((), ())), preferred_element_type=F32)


def _dot_tn(a, b):
    return lax.dot_general(a, b, (((0,), (0,)), ((), ())), preferred_element_type=F32)


def _bf(x):
    return x.astype(BF16)


def _split3(x):
    hi = x.astype(BF16)
    r1 = x - hi.astype(F32)
    mid = r1.astype(BF16)
    lo = (r1 - mid.astype(F32)).astype(BF16)
    return hi, mid, lo


def _cumsum_rows(tri, x):
    hi, mid, lo = _split3(x)
    return _dot(tri, hi) + _dot(tri, mid) + _dot(tri, lo)


def _cumsum_cols(x, triu):
    hi, mid, lo = _split3(x)
    return _dot(hi, triu) + _dot(mid, triu) + _dot(lo, triu)


def _iota2(shape, dim):
    return lax.broadcasted_iota(jnp.int32, shape, dim)


def _log_sigmoid(x):
    return jnp.minimum(x, 0.0) - jnp.log1p(jnp.exp(-jnp.abs(x)))


def _sigmoid(x):
    return 1.0 / (1.0 + jnp.exp(-x))


def _silu(x):
    return x * _sigmoid(x)


def _normproj_kernel(x_ref, g_ref, w_ref, o_ref, h_ref):
    @pl.when(pl.program_id(1) == 0)
    def _():
        x = x_ref[...]
        ms = jnp.mean(x * x, axis=-1, keepdims=True)
        h_ref[...] = ((x * lax.rsqrt(ms + EPS)) * g_ref[...]).astype(BF16)

    o_ref[...] = _dot(h_ref[...], w_ref[...])


def _normproj(x, g, w, tm, tn):
    M, K = x.shape
    N = w.shape[1]
    return pl.pallas_call(
        _normproj_kernel,
        grid=(M // tm, N // tn),
        in_specs=[pl.BlockSpec((tm, K), lambda i, j: (i, 0)),
                  pl.BlockSpec((1, K), lambda i, j: (0, 0)),
                  pl.BlockSpec((K, tn), lambda i, j: (0, j))],
        out_specs=pl.BlockSpec((tm, tn), lambda i, j: (i, j)),
        out_shape=jax.ShapeDtypeStruct((M, N), F32),
        scratch_shapes=[pltpu.VMEM((tm, K), BF16)],
        compiler_params=_cparams("parallel", "arbitrary"),
        name="normproj",
    )(x, g, w)


def _rwkv_kernel(xr_ref, xk_ref, xv_ref, xwa_ref, z_ref, sp_ref, mu_ref, w2_ref, a2_ref, w0_ref, a0_ref,
                 kkw_ref, ka_ref, rk_ref, lng_ref, s0_ref, og_ref, s_ref,
                 s_scr, phi_scr, psi_scr, qt_scr, egl_scr, carry_scr,
                 r_scr, lw_scr, k_scr, v_scr, kk_scr, b_scr, y_scr, *, L):
    t = pl.program_id(1)
    HG, D = RWKV_H, RWKV_D
    hs = range(HG)
    sls = [slice(h * D, (h + 1) * D) for h in hs]

    @pl.when(t == 0)
    def _():
        s_scr[...] = s0_ref[0]
        carry_scr[...] = sp_ref[0]

    def cat(parts):
        return jnp.concatenate(parts, axis=1)

    tt = xr_ref.shape[0]
    first = _iota2((tt, 1), 0) == 0

    def mix(x_ref, lo, hi):
        x = x_ref[...]
        prev = jnp.where(first, carry_scr[:, lo:hi], pltpu.roll(x, 1, 0))
        carry_scr[:, lo:hi] = x[tt - 1:tt, :]
        return x + (prev - x) * mu_ref[:, lo:hi]

    r_ = mix(xr_ref, 0, BW)
    k_ = mix(xk_ref, BW, 2 * BW)
    v_ = mix(xv_ref, 2 * BW, 3 * BW)
    wa = mix(xwa_ref, 3 * BW, RWKV_SHIFT_W)
    lw_scr[...] = -RWKV_DECAY_SCALE * _sigmoid(w0_ref[...] + _dot(_bf(jnp.tanh(wa)), w2_ref[...]))
    a_ = _sigmoid(a0_ref[...] + _dot(_bf(wa), a2_ref[...]))
    ones_blk = (_iota2((LANE, LANE), 0) // D == _iota2((LANE, LANE), 1) // D).astype(BF16)

    def headsum(x):
        hi = _bf(x)
        lo = _bf(x - hi.astype(F32))
        return cat([_dot(hi[:, j:j + LANE], ones_blk) + _dot(lo[:, j:j + LANE], ones_blk)
                    for j in range(0, BW, LANE)])

    kkf = k_ * kkw_ref[...]
    kk = kkf / jnp.maximum(jnp.sqrt(headsum(kkf * kkf)), 1e-12)
    r_scr[...] = r_
    k_scr[...] = k_ * (1.0 + (a_ - 1.0) * ka_ref[...])
    v_scr[...] = v_
    kk_scr[...] = kk
    b_scr[...] = kk * a_

    nchunk = tt // L
    row, col = _iota2((L, L), 0), _iota2((L, L), 1)
    tri = (row >= col).astype(BF16)
    strict = row > col
    incl2 = _iota2((L, 2 * L), 0) >= _iota2((L, 2 * L), 1) % L
    eye = (row == col).astype(F32)

    def phase1(c, carry):
        rows = pl.ds(pl.multiple_of(c * L, L), L)
        r_c, lw, k_c, v_c = r_scr[rows, :], lw_scr[rows, :], k_scr[rows, :], v_scr[rows, :]
        kk_c, b_c = kk_scr[rows, :], b_scr[rows, :]
        g = _cumsum_rows(tri, lw)
        gl = g[L - 1:L, :]
        kkp = kk_c * jnp.exp(g - lw)
        rg = r_c * jnp.exp(g)
        eng = jnp.exp(-g)
        kn, bn = k_c * eng, b_c * eng
        ee = jnp.exp(gl - g)
        kend, bend = k_c * ee, b_c * ee
        egl_scr[c] = jnp.exp(gl)
        kkp_h = [kkp[:, s_] for s_ in sls]
        rg_h = [rg[:, s_] for s_ in sls]
        bk = [_bf(jnp.concatenate([bn[:, s_], kn[:, s_]], axis=0)) for s_ in sls]
        mn = [_dot_nt(_bf(kkp_h[h]), bk[h]) for h in hs]
        aq = [_dot_nt(_bf(rg_h[h]), bk[h]) for h in hs]
        N = [_bf(jnp.where(strict, m[:, L:], 0.0)) for m in mn]
        vb = [_bf(v_c[:, s_]) for s_ in sls]
        nv = [_dot(N[h], vb[h]) for h in hs]
        P = [jnp.where(strict, -m[:, :L], 0.0) for m in mn]
        tinv = [eye + p for p in P]
        n = 1
        while 2 * n < L:
            Pb = [_bf(p) for p in P]
            P = [_dot(pb, pb) for pb in Pb]
            tinv = [t_ + _dot(_bf(t_), _bf(p)) for t_, p in zip(tinv, P)]
            n *= 2
        gh = [_dot(_bf(tinv[h]), _bf(jnp.concatenate([kkp_h[h], nv[h]], axis=1))) for h in hs]
        gm = [_bf(x[:, :D]) for x in gh]
        hm = [_bf(x[:, D:]) for x in gh]
        bend_b = [_bf(bend[:, s_]) for s_ in sls]
        aqm = [_bf(jnp.where(incl2, a, 0.0)) for a in aq]
        hv = [jnp.concatenate([-hm[h], vb[h]], axis=0) for h in hs]
        bke = [jnp.concatenate([bend_b[h], _bf(kend[:, sls[h]])], axis=0) for h in hs]
        phi = [_dot_tn(gm[h], bend_b[h]) for h in hs]
        psi = [_dot_tn(hv[h], bke[h]) for h in hs]
        qts = [rg_h[h] - _dot(aqm[h][:, :L], gm[h]) for h in hs]
        y0s = [_dot(aqm[h], hv[h]) for h in hs]
        for h in hs:
            phi_scr[c, h] = _bf(phi[h])
            psi_scr[c, h] = psi[h]
        qt_scr[rows, :] = _bf(cat(qts))
        y_scr[rows, :] = cat(y0s)
        return carry

    lax.fori_loop(0, nchunk, phase1, 0)

    def phase2(c, carry):
        rows = pl.ds(pl.multiple_of(c * L, L), L)
        qt = qt_scr[rows, :]
        egl = egl_scr[c]
        S = [s_scr[h] for h in hs]
        Sb = [_bf(s_) for s_ in S]
        sphi = [_dot(Sb[h], phi_scr[c, h]) for h in hs]
        ys = [_dot_nt(qt[:, sls[h]], Sb[h]) for h in hs]
        for h in hs:
            s_scr[h] = S[h] * egl[:, sls[h]] - sphi[h] + psi_scr[c, h]
        y_scr[rows, :] += cat(ys)
        return carry

    lax.fori_loop(0, nchunk, phase2, 0)

    y = y_scr[...]
    rk = r_scr[...] * k_scr[...] * rk_ref[...]
    cen = y - headsum(y) * (1.0 / D)
    ln = cen * lax.rsqrt(headsum(cen * cen) * (1.0 / D) + EPS)
    out = ln * lng_ref[...] + headsum(rk) * v_scr[...]
    og_ref[...] = _bf(out * _silu(z_ref[...]))

    @pl.when(t == pl.num_programs(1) - 1)
    def _():
        s_ref[0] = s_scr[...]


def _rwkv(P, B, T, p, s0, shift_prev):
    L = min(T, CHUNK)
    tt = min(T, 256)
    nT = T // tt
    nc = tt // L
    off = _OFF['rwkv_shift']
    rowblk = lambda w, o: pl.BlockSpec((tt, w), lambda b_, t: (b_ * nT + t, o // w))
    const = lambda shape: pl.BlockSpec(shape, lambda b_, t: (0,) * len(shape))
    st = pl.BlockSpec((1, RWKV_H, RWKV_D, RWKV_D), lambda b_, t: (b_, 0, 0, 0))
    vec = const((1, BW))
    return pl.pallas_call(
        functools.partial(_rwkv_kernel, L=L),
        grid=(B, nT),
        in_specs=[rowblk(BW, off), rowblk(BW, off + BW), rowblk(BW, off + 2 * BW), rowblk(LANE, off + 3 * BW),
                  rowblk(BW, _OFF['rwkv_z']),
                  pl.BlockSpec((1, 1, RWKV_SHIFT_W), lambda b_, t: (b_, 0, 0)), const((1, RWKV_SHIFT_W)),
                  const((LANE, BW)), const((LANE, BW)), vec, vec, vec, vec, vec, vec, st],
        out_specs=[pl.BlockSpec((tt, BW), lambda b_, t: (b_ * nT + t, 0)), st],
        out_shape=[jax.ShapeDtypeStruct((B * T, BW), BF16), jax.ShapeDtypeStruct(s0.shape, F32)],
        scratch_shapes=[pltpu.VMEM((RWKV_H, RWKV_D, RWKV_D), F32),
                        pltpu.VMEM((nc, RWKV_H, RWKV_D, RWKV_D), BF16),
                        pltpu.VMEM((nc, RWKV_H, RWKV_D, RWKV_D), F32),
                        pltpu.VMEM((tt, BW), BF16),
                        pltpu.VMEM((nc, 1, BW), F32),
                        pltpu.VMEM((1, RWKV_SHIFT_W), F32)] + [pltpu.VMEM((tt, BW), F32)] * 7,
        compiler_params=_cparams("parallel", "arbitrary"),
        name="rwkv_scan",
    )(P, P, P, P, P, shift_prev, p['rwkv_mu'], p['rwkv_w2p'], p['rwkv_a2p'], p['rwkv_w0'], p['rwkv_a0'],
      p['rwkv_k_k'], p['rwkv_k_a'], p['rwkv_r_k'], p['rwkv_ln_g'], s0)


def _gla_kernel(q_ref, k_ref, v_ref, gd_ref, z_ref, g2_ref, gb_ref, ng_ref, s0_ref, o_ref, s_ref, s_scr, g_scr, *, L):
    t = pl.program_id(1)

    @pl.when(t == 0)
    def _():
        s_scr[...] = s0_ref[0]

    gk = _dot(_bf(gd_ref[...]), g2_ref[...]) + gb_ref[...]
    g_scr[...] = _log_sigmoid(gk) * (1.0 / GLA_GATE_NORM)

    nchunk = q_ref.shape[0] // L
    row, col = _iota2((L, L), 0), _iota2((L, L), 1)
    tri = (row >= col).astype(BF16)
    causal = row >= col
    scale = GLA_DK ** -0.5

    U = 2 if nchunk % 2 == 0 else 1
    hs, us = range(GLA_H), range(U)
    ksl = [slice(h * GLA_DK, (h + 1) * GLA_DK) for h in hs]
    vsl = [slice(h * GLA_DV, (h + 1) * GLA_DV) for h in hs]
    uh = [(u, h) for u in us for h in hs]

    def chunks(i, carry):
        rws = [pl.ds(pl.multiple_of((i * U + u) * L, L), L) for u in us]
        q = [q_ref[r, :] * scale for r in rws]
        k = [k_ref[r, :] for r in rws]
        b = [_cumsum_rows(tri, g_scr[r, :]) for r in rws]
        b_end = [x[L - 1:L, :] for x in b]
        qi = [_bf(q[u] * jnp.exp(b[u])) for u in us]
        ki = [_bf(k[u] * jnp.exp(-b[u])) for u in us]
        kend = [_bf(k[u] * jnp.exp(b_end[u] - b[u])) for u in us]
        eb_end = [jnp.exp(x) for x in b_end]
        vb = {(u, h): _bf(v_ref[rws[u], vsl[h]]) for u, h in uh}
        A = {(u, h): _bf(jnp.where(causal, _dot_nt(qi[u][:, ksl[h]], ki[u][:, ksl[h]]), 0.0)) for u, h in uh}
        kv = {(u, h): _dot_tn(vb[u, h], kend[u][:, ksl[h]]) for u, h in uh}
        av = {(u, h): _dot(A[u, h], vb[u, h]) for u, h in uh}
        St = [s_scr[h] for h in hs]
        o = {}
        for u in us:
            for h in hs:
                o[u, h] = av[u, h] + _dot_nt(qi[u][:, ksl[h]], _bf(St[h]))
            St = [St[h] * eb_end[u][:, ksl[h]] + kv[u, h] for h in hs]
        ms = {x: jnp.mean(o[x] * o[x], axis=-1, keepdims=True) for x in uh}
        for u, h in uh:
            o_ref[rws[u], vsl[h]] = _bf((o[u, h] * lax.rsqrt(ms[u, h] + EPS)) * ng_ref[:, vsl[h]]
                                        * _silu(z_ref[rws[u], vsl[h]]))
        for h in hs:
            s_scr[h] = St[h]
        return carry

    lax.fori_loop(0, nchunk // U, chunks, 0)

    @pl.when(t == pl.num_programs(1) - 1)
    def _():
        s_ref[0] = s_scr[...]


def _gla(P, B, T, g2p, gb, ng, s0t):
    L = min(T, CHUNK)
    tt = min(T, 512)
    nT = T // tt
    rowblk = lambda w, off: pl.BlockSpec((tt, w), lambda b_, t: (b_ * nT + t, off // w))
    const = lambda shape: pl.BlockSpec(shape, lambda b_, t: (0,) * len(shape))
    st = pl.BlockSpec((1, GLA_H, GLA_DV, GLA_DK), lambda b_, t: (b_, 0, 0, 0))
    return pl.pallas_call(
        functools.partial(_gla_kernel, L=L),
        grid=(B, nT),
        in_specs=[rowblk(512, _OFF['gla_q']), rowblk(512, _OFF['gla_k']), rowblk(BW, _OFF['gla_v']),
                  rowblk(LANE, _OFF['gla_gd']), rowblk(BW, _OFF['gla_z']), const((LANE, 512)), const((1, 512)),
                  const((1, BW)), st],
        out_specs=[pl.BlockSpec((tt, BW), lambda b_, t: (b_ * nT + t, 0)), st],
        out_shape=[jax.ShapeDtypeStruct((B * T, BW), BF16), jax.ShapeDtypeStruct(s0t.shape, F32)],
        scratch_shapes=[pltpu.VMEM((GLA_H, GLA_DV, GLA_DK), F32), pltpu.VMEM((tt, 512), F32)],
        compiler_params=_cparams("parallel", "arbitrary"),
        name="gla_scan",
    )(P, P, P, P, P, g2p, gb, ng, s0t)


def _ret_kernel(q_ref, k_ref, v_ref, z_ref, cos_ref, sin_ref, s0_ref, o_ref, s_ref, s_scr, *, L):
    t = pl.program_id(1)

    @pl.when(t == 0)
    def _():
        s_scr[...] = s0_ref[0]

    nchunk = q_ref.shape[0] // L
    rel = (_iota2((L, L), 0) - _iota2((L, L), 1)).astype(F32)
    tcol = _iota2((L, 1), 0).astype(F32)
    scale = RET_DK ** -0.5
    lgs = [math.log(1.0 - 2.0 ** (-5.0 - h)) for h in range(RET_H)]
    decay = [jnp.where(rel >= 0, jnp.exp(lg * jnp.maximum(rel, 0.0)), 0.0) for lg in lgs]
    rowd = [jnp.exp(lg * (tcol + 1.0)) for lg in lgs]
    cold = [jnp.exp(lg * (L - 1.0 - tcol)) for lg in lgs]

    def rope(x, cos2, sin2):
        return x * cos2 + pltpu.roll(x, RET_DK // 2, 1) * sin2

    U = 2 if nchunk % 2 == 0 else 1
    hs, us = range(RET_H), range(U)
    ksl = [slice(h * RET_DK, (h + 1) * RET_DK) for h in hs]
    vsl = [slice(h * RET_DV, (h + 1) * RET_DV) for h in hs]
    uh = [(u, h) for u in us for h in hs]

    def chunks(i, carry):
        rws = [pl.ds(pl.multiple_of((i * U + u) * L, L), L) for u in us]
        cs = [(cos_ref[r, :], sin_ref[r, :]) for r in rws]
        k = {(u, h): rope(k_ref[rws[u], ksl[h]], *cs[u]) for u, h in uh}
        qb = {(u, h): _bf(rope(q_ref[rws[u], ksl[h]], *cs[u]) * scale) for u, h in uh}
        kb = {x: _bf(k[x]) for x in uh}
        kcb = {(u, h): _bf(k[u, h] * cold[h]) for u, h in uh}
        vb = {(u, h): _bf(v_ref[rws[u], vsl[h]]) for u, h in uh}
        A = {(u, h): _bf(_dot_nt(qb[u, h], kb[u, h]) * decay[h]) for u, h in uh}
        kv = {x: _dot_tn(kcb[x], vb[x]) for x in uh}
        av = {x: _dot(A[x], vb[x]) for x in uh}
        S = [s_scr[h] for h in hs]
        o = {}
        for u in us:
            for h in hs:
                o[u, h] = av[u, h] + rowd[h] * _dot(qb[u, h], _bf(S[h]))
            S = [math.exp(lgs[h] * L) * S[h] + kv[u, h] for h in hs]
        ms = {x: jnp.mean(o[x] * o[x], axis=-1, keepdims=True) for x in uh}
        for u, h in uh:
            o_ref[rws[u], vsl[h]] = _bf(o[u, h] * lax.rsqrt(ms[u, h] + EPS) * _silu(z_ref[rws[u], vsl[h]]))
        for h in hs:
            s_scr[h] = S[h]
        return carry

    lax.fori_loop(0, nchunk // U, chunks, 0)

    @pl.when(t == pl.num_programs(1) - 1)
    def _():
        s_ref[0] = s_scr[...]


def _ret(P, B, T, cos2, sin2, s0):
    L = min(T, CHUNK)
    tt = min(T, 512)
    nT = T // tt
    rowblk = lambda w, off: pl.BlockSpec((tt, w), lambda b_, t: (b_ * nT + t, off // w))
    tab = pl.BlockSpec((tt, RET_DK), lambda b_, t: (t, 0))
    st = pl.BlockSpec((1, RET_H, RET_DK, RET_DV), lambda b_, t: (b_, 0, 0, 0))
    return pl.pallas_call(
        functools.partial(_ret_kernel, L=L),
        grid=(B, nT),
        in_specs=[rowblk(512, _OFF['ret_q']), rowblk(512, _OFF['ret_k']), rowblk(BW, _OFF['ret_v']),
                  rowblk(BW, _OFF['ret_z']), tab, tab, st],
        out_specs=[pl.BlockSpec((tt, BW), lambda b_, t: (b_ * nT + t, 0)), st],
        out_shape=[jax.ShapeDtypeStruct((B * T, BW), BF16), jax.ShapeDtypeStruct(s0.shape, F32)],
        scratch_shapes=[pltpu.VMEM((RET_H, RET_DK, RET_DV), F32)],
        compiler_params=_cparams("parallel", "arbitrary"),
        name="ret_scan",
    )(P, P, P, P, cos2, sin2, s0)


def _mlstm_pre_kernel(xm_ref, cp_ref, cw_ref, cb_ref, wq_ref, wk_ref, wv_ref, wif_ref, bif_ref,
                      xc_ref, q_ref, k_ref, v_ref, g_ref, carry_scr):
    t = pl.program_id(1)
    tm = xm_ref.shape[0]
    nprev = ML_CONV - 1

    @pl.when(t == 0)
    def _():
        carry_scr[...] = jnp.zeros_like(carry_scr)
        carry_scr[8 - nprev:8, :] = cp_ref[0]

    xm = xm_ref[...]
    c8 = carry_scr[...]
    rid = _iota2((8, 1), 0)
    conv = xm * cw_ref[nprev:nprev + 1, :] + cb_ref[...]
    for j in range(1, ML_CONV):
        rolled = pltpu.roll(xm, j, 0)
        head = jnp.where(rid < j, pltpu.roll(c8, j, 0), rolled[0:8, :])
        prev = jnp.concatenate([head, rolled[8:, :]], axis=0) if tm > 8 else head
        conv = conv + prev * cw_ref[nprev - j:nprev - j + 1, :]
    carry_scr[...] = xm[tm - 8:tm, :]
    xc = _silu(conv)
    xc_ref[...] = xc
    xcb, xmb = _bf(xc), _bf(xm)
    q = _dot(xcb, wq_ref[...])
    k = _dot(xcb, wk_ref[...])
    v = _dot(xmb, wv_ref[...])
    g_ref[...] = (_dot(_bf(q), wif_ref[0]) + _dot(_bf(k), wif_ref[1]) + _dot(_bf(v), wif_ref[2])) + bif_ref[...]
    q_ref[...] = q
    k_ref[...] = k * (ML_D ** -0.5)
    v_ref[...] = v


def _mlstm_pre(P, B, T, conv_prev, p):
    tm = min(T, 512)
    nT = T // tm
    row = pl.BlockSpec((tm, BW), lambda b_, t: (b_ * nT + t, 0))
    const = lambda shape: pl.BlockSpec(shape, lambda b_, t: (0,) * len(shape))
    M = B * T
    return pl.pallas_call(
        _mlstm_pre_kernel,
        grid=(B, nT),
        in_specs=[pl.BlockSpec((tm, BW), lambda b_, t: (b_ * nT + t, _OFF['ml_x'] // BW)),
                  pl.BlockSpec((1, ML_CONV - 1, BW), lambda b_, t: (b_, 0, 0)),
                  const((ML_CONV, BW)), const((1, BW)), const((BW, BW)), const((BW, BW)), const((BW, BW)),
                  const((3, BW, LANE)), const((1, LANE))],
        out_specs=[row, row, row, row, pl.BlockSpec((tm, LANE), lambda b_, t: (b_ * nT + t, 0))],
        out_shape=[jax.ShapeDtypeStruct((M, BW), F32)] * 4 + [jax.ShapeDtypeStruct((M, LANE), F32)],
        scratch_shapes=[pltpu.VMEM((8, BW), F32)],
        compiler_params=_cparams("parallel", "arbitrary"),
        name="mlstm_pre",
    )(P, conv_prev, p['ml_conv_w'], p['ml_conv_b'], p['ml_wq'], p['ml_wk'], p['ml_wv'], p['ml_wif'], p['ml_bif'])


def _mlstm_kernel(q_ref, k_ref, v_ref, gc_ref, gr_ref, xc_ref, z_ref, ng_ref, sk_ref, c0_ref, n0_ref, m0_ref,
                  o_ref, c_ref, n_ref, m_ref, c_scr, n_scr, m_scr, *, L):
    t = pl.program_id(1)

    @pl.when(t == 0)
    def _():
        c_scr[...] = c0_ref[0]
        n_scr[...] = n0_ref[0]
        m_scr[...] = m0_ref[0]

    nchunk = q_ref.shape[0] // L
    row, col = _iota2((L, L), 0), _iota2((L, L), 1)
    tri = (row >= col).astype(BF16)
    triu = (row <= col).astype(BF16)
    causal = row >= col

    U = 2 if nchunk % 2 == 0 else 1
    hs, us = range(ML_H), range(U)
    sl = [slice(h * ML_D, (h + 1) * ML_D) for h in hs]
    uh = [(u, h) for u in us for h in hs]

    def chunks(i, carry):
        cidx = [i * U + u for u in us]
        rws = [pl.ds(pl.multiple_of(c * L, L), L) for c in cidx]
        gcb = [gc_ref[0, c] for c in cidx]
        grb = [gr_ref[0, c] for c in cidx]
        bc = [_cumsum_rows(tri, x) for x in gcb]
        br = [_cumsum_cols(x, triu) for x in grb]
        q = {(u, h): q_ref[rws[u], sl[h]] for u, h in uh}
        k = {(u, h): k_ref[rws[u], sl[h]] for u, h in uh}
        qb = {x: _bf(q[x]) for x in uh}
        vb = {(u, h): _bf(v_ref[rws[u], sl[h]]) for u, h in uh}
        qk = {x: _dot_nt(qb[x], _bf(k[x])) for x in uh}
        b_c = {(u, h): bc[u][:, ML_H + h:ML_H + h + 1] for u, h in uh}
        dlog = {(u, h): jnp.where(causal, b_c[u, h] - br[u][ML_H + h:ML_H + h + 1, :] + grb[u][h:h + 1, :], -jnp.inf)
                for u, h in uh}
        m_loc = {x: jnp.max(dlog[x], axis=-1, keepdims=True) for x in uh}
        s_loc = {x: qk[x] * jnp.exp(dlog[x] - m_loc[x]) for x in uh}
        rs = {x: jnp.sum(s_loc[x], axis=-1, keepdims=True) for x in uh}
        b_end = {x: b_c[x][L - 1:L, :] for x in uh}
        m_le = {x: m_loc[x][L - 1:L, :] for x in uh}
        kw = {(u, h): k[u, h] * jnp.exp(b_end[u, h] - b_c[u, h] + gcb[u][:, h:h + 1] - m_le[u, h]) for u, h in uh}
        sv = {x: _dot(_bf(s_loc[x]), vb[x]) for x in uh}
        kv = {x: _dot_tn(_bf(kw[x]), vb[x]) for x in uh}
        ksum = {x: jnp.sum(kw[x], axis=0, keepdims=True) for x in uh}
        C = [c_scr[h] for h in hs]
        nvec = [n_scr[h] for h in hs]
        m_prev = [m_scr[h][:, 0:1] for h in hs]
        hh = {}
        for u in us:
            qc = [_dot(qb[u, h], _bf(C[h])) for h in hs]
            qn = [jnp.sum(q[u, h] * nvec[h], axis=-1, keepdims=True) for h in hs]
            from_state = [b_c[u, h] + m_prev[h] for h in hs]
            m_t = [jnp.maximum(from_state[h], m_loc[u, h]) for h in hs]
            a_in = [jnp.exp(m_loc[u, h] - m_t[h]) for h in hs]
            w_state = [jnp.exp(from_state[h] - m_t[h]) for h in hs]
            den = [a_in[h] * rs[u, h] + w_state[h] * qn[h] for h in hs]
            for h in hs:
                hh[u, h] = ((a_in[h] * sv[u, h] + w_state[h] * qc[h])
                            / jnp.maximum(jnp.abs(den[h]), jnp.exp(-m_t[h])))
            m_new = [m_t[h][L - 1:L, :] for h in hs]
            cd = [jnp.exp(b_end[u, h] + m_prev[h] - m_new[h]) for h in hs]
            sc = [jnp.exp(m_le[u, h] - m_new[h]) for h in hs]
            C = [cd[h] * C[h] + sc[h] * kv[u, h] for h in hs]
            nvec = [cd[h] * nvec[h] + sc[h] * ksum[u, h] for h in hs]
            m_prev = m_new
        mean = {x: jnp.mean(hh[x], axis=-1, keepdims=True) for x in uh}
        cen = {x: hh[x] - mean[x] for x in uh}
        var = {x: jnp.mean(cen[x] * cen[x], axis=-1, keepdims=True) for x in uh}
        for u, h in uh:
            y = cen[u, h] * lax.rsqrt(var[u, h] + EPS)
            o_ref[rws[u], sl[h]] = _bf((y * ng_ref[:, sl[h]] + sk_ref[:, sl[h]] * xc_ref[rws[u], sl[h]])
                                       * _silu(z_ref[rws[u], sl[h]]))
        for h in hs:
            c_scr[h] = C[h]
            n_scr[h] = nvec[h]
            m_scr[h] = jnp.broadcast_to(m_prev[h], (1, LANE))
        return carry

    lax.fori_loop(0, nchunk // U, chunks, 0)

    @pl.when(t == pl.num_programs(1) - 1)
    def _():
        c_ref[0] = c_scr[...]
        n_ref[0] = n_scr[...]
        m_ref[0] = m_scr[...]


def _mlstm(q, k, v, gc, gr, xc, P, ng, sk, c0, n0, m0, B, T):
    L = min(T, CHUNK)
    tt = min(T, 512)
    nT = T // tt
    nc = tt // L
    row = pl.BlockSpec((tt, BW), lambda b_, t: (b_ * nT + t, 0))
    const = pl.BlockSpec((1, BW), lambda b_, t: (0, 0))
    stc = pl.BlockSpec((1, ML_H, ML_D, ML_D), lambda b_, t: (b_, 0, 0, 0))
    stn = pl.BlockSpec((1, ML_H, 1, ML_D), lambda b_, t: (b_, 0, 0, 0))
    stm = pl.BlockSpec((1, ML_H, 1, LANE), lambda b_, t: (b_, 0, 0, 0))
    return pl.pallas_call(
        functools.partial(_mlstm_kernel, L=L),
        grid=(B, nT),
        in_specs=[row, row, row,
                  pl.BlockSpec((1, nc, L, 8), lambda b_, t: (b_, t, 0, 0)),
                  pl.BlockSpec((1, nc, 8, L), lambda b_, t: (b_, t, 0, 0)),
                  row, pl.BlockSpec((tt, BW), lambda b_, t: (b_ * nT + t, _OFF['ml_z'] // BW)), const, const, stc, stn, stm],
        out_specs=[row, stc, stn, stm],
        out_shape=[jax.ShapeDtypeStruct((B * T, BW), BF16), jax.ShapeDtypeStruct(c0.shape, F32),
                   jax.ShapeDtypeStruct(n0.shape, F32), jax.ShapeDtypeStruct(m0.shape, F32)],
        scratch_shapes=[pltpu.VMEM((ML_H, ML_D, ML_D), F32), pltpu.VMEM((ML_H, 1, ML_D), F32),
                        pltpu.VMEM((ML_H, 1, LANE), F32)],
        compiler_params=_cparams("parallel", "arbitrary"),
        name="mlstm_scan",
    )(q, k, v, gc, gr, xc, P, ng, sk, c0, n0, m0)


def _xattn_kernel(q_ref, z_ref, mk_ref, mv_ref, o_ref):
    hs = range(XA_H)
    sl = [slice(h * XA_D, (h + 1) * XA_D) for h in hs]
    s = [_dot_nt(_bf(q_ref[:, x]), _bf(mk_ref[0, :, x])) * (XA_D ** -0.5) for x in sl]
    e = [jnp.exp(x - jnp.max(x, axis=-1, keepdims=True)) for x in s]
    prob = [_bf(x / jnp.sum(x, axis=-1, keepdims=True)) for x in e]
    o = [_dot(prob[h], _bf(mv_ref[0, :, sl[h]])) for h in hs]
    for h in hs:
        o_ref[:, sl[h]] = _bf(o[h] * _silu(z_ref[:, sl[h]]))


def _xattn(P, B, T, mk, mv):
    tt = min(T, 512)
    nT = T // tt
    n_mem = mk.shape[1]
    mem = pl.BlockSpec((1, n_mem, BW), lambda b_, t: (b_, 0, 0))
    return pl.pallas_call(
        _xattn_kernel,
        grid=(B, nT),
        in_specs=[pl.BlockSpec((tt, BW), lambda b_, t: (b_ * nT + t, _OFF['xa_q'] // BW)),
                  pl.BlockSpec((tt, BW), lambda b_, t: (b_ * nT + t, _OFF['xa_z'] // BW)), mem, mem],
        out_specs=pl.BlockSpec((tt, BW), lambda b_, t: (b_ * nT + t, 0)),
        out_shape=jax.ShapeDtypeStruct((B * T, BW), BF16),
        compiler_params=_cparams("parallel", "parallel"),
        name="xattn",
    )(P, P, mk, mv)


def _merge_kernel(o0, o1, o2, o3, o4, g_ref, w_ref, out_ref):
    ys = [_dot(o_ref[...], w_ref[i]) for i, o_ref in enumerate((o0, o1, o2, o3, o4))]
    acc = None
    for i, y in enumerate(ys):
        term = _sigmoid(g_ref[:, i * D_MODEL:(i + 1) * D_MODEL]) * y
        acc = term if acc is None else acc + term
    out_ref[...] = _bf(acc)


def _merge(ogs, P, wb, tm):
    M = ogs[0].shape[0]
    gw = N_BRANCH * D_MODEL
    og = pl.BlockSpec((tm, BW), lambda i: (i, 0))
    return pl.pallas_call(
        _merge_kernel,
        grid=(M // tm,),
        in_specs=[og] * N_BRANCH + [pl.BlockSpec((tm, gw), lambda i: (i, _OFF['gates'] // gw)),
                                    pl.BlockSpec((N_BRANCH, BW, D_MODEL), lambda i: (0, 0, 0),
                                                 pipeline_mode=pl.Buffered(1))],
        out_specs=pl.BlockSpec((tm, D_MODEL), lambda i: (i, 0)),
        out_shape=jax.ShapeDtypeStruct((M, D_MODEL), BF16),
        compiler_params=_cparams("parallel"),
        name="merge",
    )(*ogs, P, wb)


def _resid_kernel(x_ref, m_ref, w_ref, o_ref):
    o_ref[...] = x_ref[...] + _dot(m_ref[...], w_ref[...])


def _resid_out(x, merged, w, tm, tn):
    M = x.shape[0]
    return pl.pallas_call(
        _resid_kernel,
        grid=(M // tm, D_MODEL // tn),
        in_specs=[pl.BlockSpec((tm, tn), lambda i, j: (i, j)),
                  pl.BlockSpec((tm, D_MODEL), lambda i, j: (i, 0)),
                  pl.BlockSpec((D_MODEL, tn), lambda i, j: (0, j))],
        out_specs=pl.BlockSpec((tm, tn), lambda i, j: (i, j)),
        out_shape=jax.ShapeDtypeStruct((M, D_MODEL), F32),
        compiler_params=_cparams("parallel", "parallel"),
        name="resid_out",
    )(x, merged, w)


def _rmsnorm_kernel(x_ref, g_ref, o_ref):
    x = x_ref[...]
    ms = jnp.mean(x * x, axis=-1, keepdims=True)
    o_ref[...] = (x * lax.rsqrt(ms + EPS)) * g_ref[...]


def _rmsnorm(x, g, tm):
    M, K = x.shape
    return pl.pallas_call(
        _rmsnorm_kernel,
        grid=(M // tm,),
        in_specs=[pl.BlockSpec((tm, K), lambda i: (i, 0)), pl.BlockSpec((1, K), lambda i: (0, 0))],
        out_specs=pl.BlockSpec((tm, K), lambda i: (i, 0)),
        out_shape=jax.ShapeDtypeStruct((M, K), F32),
        compiler_params=_cparams("parallel"),
        name="rmsnorm",
    )(x, g)


def _blockdiag(w):
    depth, n, c, d = w.shape
    rows = w.reshape(depth, n * c, d)
    tile = (jnp.arange(d)[:, None] == jnp.arange(n * d)[None, :] % d).astype(w.dtype)
    dense = jnp.einsum('lrd,dj->lrj', rows, tile)
    mask = (jnp.arange(n * c)[:, None] // c) == (jnp.arange(n * d)[None, :] // d)
    return jnp.where(mask, dense, 0.0).astype(BF16)


def _rope_tables(pos):
    half = RET_DK // 2
    inv = ROPE_BASE ** (-jnp.linspace(0.0, 1.0, half, dtype=F32))
    ang = pos[:, None] * inv[None, :]
    cos, sin = jnp.cos(ang), jnp.sin(ang)
    return jnp.concatenate([cos, cos], axis=-1), jnp.concatenate([-sin, sin], axis=-1)


def _layer(x, B, T, tabs, st, mem_k, mem_v, p):
    M = B * T
    tm = min(M, 1024)
    P = _normproj(x, p['norm_g'], p['w_pack'], tm, PROJ_TN)
    P3 = P.reshape(B, T, NP)

    og_a, s_rwkv = _rwkv(P, B, T, p, st['rwkv'], st['rwkv_shift'])
    shift_new = P3[:, T - 1:, _OFF['rwkv_shift']:_OFF['rwkv_shift'] + RWKV_SHIFT_W]

    og_b, s_gla_t = _gla(P, B, T, p['gla_g2p'], p['gla_gb'], p['gla_norm_g'], jnp.swapaxes(st['gla'], -1, -2))
    s_gla = jnp.swapaxes(s_gla_t, -1, -2)

    og_c, s_ret = _ret(P, B, T, tabs[0], tabs[1], st['ret'])

    xc, q_m, k_m, v_m, gates = _mlstm_pre(P, B, T, st['ml_conv'], p)
    L = min(T, CHUNK)
    gcol = jnp.concatenate([gates[:, :ML_H], _log_sigmoid(gates[:, ML_H:2 * ML_H])], axis=-1).reshape(B, T // L, L, 2 * ML_H)
    grow = jnp.swapaxes(gcol, -1, -2)
    og_d, c_new, n_new, m_new = _mlstm(
        q_m, k_m, v_m, gcol, grow, xc, P, p['ml_norm_g'], p['ml_skip'], st['ml_c'], st['ml_n'][:, :, None, :],
        jnp.broadcast_to(st['ml_m'][:, :, None, None], st['ml_m'].shape + (1, LANE)), B, T)
    xm_tail = P3[:, max(T - (ML_CONV - 1), 0):, _OFF['ml_x']:_OFF['ml_x'] + BW]
    conv_new = jnp.concatenate([st['ml_conv'], xm_tail], axis=1)[:, -(ML_CONV - 1):]

    og_x = _xattn(P, B, T, mem_k.reshape(B, -1, BW), mem_v.reshape(B, -1, BW))

    merged = _merge([og_a, og_b, og_c, og_d, og_x], P, p['w_branch'], min(M, 256))
    x_new = _resid_out(x, merged, p['w_out'], min(M, 1024), 512)
    new = {'rwkv': s_rwkv, 'rwkv_shift': shift_new, 'gla': s_gla, 'ret': s_ret, 'ml_c': c_new,
           'ml_n': n_new[:, :, 0, :], 'ml_m': m_new[:, :, 0, 0], 'ml_conv': conv_new}
    return x_new, new


_STATE_KEYS = ('rwkv', 'rwkv_shift', 'gla', 'ret', 'ml_c', 'ml_n', 'ml_m', 'ml_conv')


def kernel(x_prompt, x_sample, mem_prompt, cache_mem_k, cache_mem_v, state_rwkv, state_rwkv_shift, state_gla, state_ret, state_mlstm_c, state_mlstm_n, state_mlstm_m, state_mlstm_conv, norm_g, mem_norm_g, w_in, w_mem_kv, rwkv_mu, rwkv_w0, rwkv_w2, rwkv_a0, rwkv_a2, rwkv_k_k, rwkv_k_a, rwkv_r_k, rwkv_ln_g, gla_g2, gla_gb, gla_norm_g, ml_conv_w, ml_conv_b, ml_wq, ml_wk, ml_wv, ml_w_if, ml_b_if, ml_skip, ml_norm_g, w_branch, w_out, final_norm_g):
    Bp, Tp, D = x_prompt.shape
    Bs, Ts, _ = x_sample.shape
    depth = w_in.shape[0]
    n_mem = mem_prompt.shape[1]

    params = {
        'norm_g': norm_g[:, None, :], 'mem_norm_g': mem_norm_g[:, None, :],
        'w_pack': _pack_w_in(w_in), 'w_mem_kv': _bf(w_mem_kv),
        'rwkv_mu': rwkv_mu[:, None, :], 'rwkv_w0': rwkv_w0[:, None, :], 'rwkv_a0': rwkv_a0[:, None, :],
        'rwkv_w2p': _bf(jnp.pad(rwkv_w2, ((0, 0), (0, LANE - RWKV_LORA), (0, 0)))),
        'rwkv_a2p': _bf(jnp.pad(rwkv_a2, ((0, 0), (LANE - RWKV_LORA, 0), (0, 0)))),
        'rwkv_k_k': rwkv_k_k[:, None, :], 'rwkv_k_a': rwkv_k_a[:, None, :], 'rwkv_r_k': rwkv_r_k[:, None, :],
        'rwkv_ln_g': rwkv_ln_g[:, None, :],
        'gla_g2p': _bf(jnp.pad(gla_g2, ((0, 0), (0, LANE - GLA_LORA), (0, 0)))), 'gla_gb': gla_gb[:, None, :],
        'gla_norm_g': gla_norm_g[:, None, :],
        'ml_conv_w': ml_conv_w, 'ml_conv_b': ml_conv_b[:, None, :],
        'ml_wq': _blockdiag(ml_wq), 'ml_wk': _blockdiag(ml_wk), 'ml_wv': _blockdiag(ml_wv),
        'ml_wif': _bf(jnp.pad(ml_w_if.reshape(depth, 3, BW, 2 * ML_H), ((0, 0), (0, 0), (0, 0), (0, LANE - 2 * ML_H)))),
        'ml_bif': jnp.pad(ml_b_if, ((0, 0), (0, LANE - 2 * ML_H)))[:, None, :],
        'ml_skip': ml_skip[:, None, :], 'ml_norm_g': ml_norm_g[:, None, :],
        'w_branch': _bf(w_branch), 'w_out': _bf(w_out),
    }
    cache = {'rwkv': state_rwkv, 'rwkv_shift': state_rwkv_shift, 'gla': state_gla, 'ret': state_ret,
             'ml_c': state_mlstm_c, 'ml_n': state_mlstm_n, 'ml_m': state_mlstm_m, 'ml_conv': state_mlstm_conv,
             'mem_k': cache_mem_k, 'mem_v': cache_mem_v}
    tabs_p = _rope_tables(jnp.arange(Tp, dtype=F32))
    tabs_s = _rope_tables(PAST_LEN + jnp.arange(Ts, dtype=F32))
    zero_p = {
        'rwkv': jnp.zeros((Bp, RWKV_H, RWKV_D, RWKV_D), F32), 'rwkv_shift': jnp.zeros((Bp, 1, RWKV_SHIFT_W), F32),
        'gla': jnp.zeros((Bp, GLA_H, GLA_DK, GLA_DV), F32), 'ret': jnp.zeros((Bp, RET_H, RET_DK, RET_DV), F32),
        'ml_c': jnp.zeros((Bp, ML_H, ML_D, ML_D), F32), 'ml_n': jnp.zeros((Bp, ML_H, ML_D), F32),
        'ml_m': jnp.zeros((Bp, ML_H), F32), 'ml_conv': jnp.zeros((Bp, ML_CONV - 1, BW), F32),
    }
    mem2d = mem_prompt.reshape(Bp * n_mem, D)

    def body(carry, xs):
        yp, ys = carry
        p, c = xs
        kv = _normproj(mem2d, p['mem_norm_g'], p['w_mem_kv'], min(Bp * n_mem, 512), 512)
        mk = kv[:, :BW].reshape(Bp, n_mem, XA_H, XA_D)
        mv = kv[:, BW:].reshape(Bp, n_mem, XA_H, XA_D)
        yp, stp = _layer(yp, Bp, Tp, tabs_p, zero_p, mk, mv, p)
        ys, sts = _layer(ys, Bs, Ts, tabs_s, {nm: c[nm] for nm in _STATE_KEYS}, c['mem_k'], c['mem_v'], p)
        return (yp, ys), (stp, mk, mv, sts)

    (yp, ys), (stp, mk, mv, sts) = lax.scan(
        body, (x_prompt.reshape(Bp * Tp, D), x_sample.reshape(Bs * Ts, D)), (params, cache))
    fg = final_norm_g[None, :]
    y_prompt = _rmsnorm(yp, fg, min(Bp * Tp, 1024)).reshape(Bp, Tp, D)
    y_sample = _rmsnorm(ys, fg, min(Bs * Ts, 1024)).reshape(Bs, Ts, D)
    return (y_prompt, y_sample,
            stp['rwkv'], stp['rwkv_shift'], stp['gla'], stp['ret'], stp['ml_c'], stp['ml_n'], stp['ml_m'],
            stp['ml_conv'], mk, mv,
            sts['rwkv'], sts['rwkv_shift'], sts['gla'], sts['ret'], sts['ml_c'], sts['ml_n'], sts['ml_m'],
            sts['ml_conv'])
```

```python
import functools
import math

import jax
import jax.numpy as jnp
from jax import lax
from jax.experimental import pallas as pl
from jax.experimental.pallas import tpu as pltpu

F32 = jnp.float32
BF16 = jnp.bfloat16

D_MODEL = 2048
BW = 1024
EPS = 1e-6
CHUNK = 64
N_BRANCH = 5
PAST_LEN = 2048
RWKV_H, RWKV_D, RWKV_LORA = 16, 64, 64
RWKV_SHIFT_W = 3 * BW + 2 * RWKV_LORA
RWKV_DECAY_SCALE = 0.606531
GLA_H, GLA_DK, GLA_DV, GLA_LORA = 4, 128, 256, 16
GLA_GATE_NORM = 16.0
RET_H, RET_DK, RET_DV = 4, 128, 256
ROPE_BASE = 10000.0
ML_H, ML_D, ML_CONV, ML_QK_BLOCK = 4, 256, 4, 4
XA_H, XA_D = 4, 256
LANE = 128
VMEM_LIMIT = 56 * 1024 * 1024

_IN_LAYOUT = (
    ('rwkv_shift', RWKV_SHIFT_W), ('rwkv_z', BW), ('gla_q', 512), ('gla_k', 512), ('gla_v', BW),
    ('gla_gd', GLA_LORA), ('gla_z', BW), ('ret_q', 512), ('ret_k', 512), ('ret_v', BW), ('ret_z', BW),
    ('ml_x', BW), ('ml_z', BW), ('xa_q', BW), ('xa_z', BW), ('gates', N_BRANCH * D_MODEL),
)
_PACK_ORDER = ('gates', 'rwkv_z', 'gla_z', 'ret_z', 'ml_z', 'xa_z', 'ret_q', 'ret_k', 'ret_v',
               'gla_q', 'gla_k', 'gla_v', 'ml_x', 'xa_q', 'rwkv_shift', 'gla_gd')
PROJ_TN = 512


def _src_cols(name):
    start = 0
    for nm, size in _IN_LAYOUT:
        if nm == name:
            return start, start + size
        start += size
    raise KeyError(name)


def _pack_offsets():
    off, cur = {}, 0
    for nm in _PACK_ORDER:
        a, b = _src_cols(nm)
        off[nm] = cur
        cur += -(-(b - a) // LANE) * LANE
    total = -(-cur // PROJ_TN) * PROJ_TN
    return off, total


_OFF, NP = _pack_offsets()


def _pack_w_in(w_in):
    parts, cur = [], 0
    for nm in _PACK_ORDER:
        a, b = _src_cols(nm)
        parts.append(w_in[..., a:b].astype(BF16))
        width = -(-(b - a) // LANE) * LANE
        if width != b - a:
            parts.append(jnp.zeros(w_in.shape[:-1] + (width - (b - a),), BF16))
        cur += width
    if NP != cur:
        parts.append(jnp.zeros(w_in.shape[:-1] + (NP - cur,), BF16))
    return jnp.concatenate(parts, axis=-1)


def _cparams(*sem):
    return pltpu.CompilerParams(dimension_semantics=sem, vmem_limit_bytes=VMEM_LIMIT)


def _dot(a, b):
    return jnp.dot(a, b, preferred_element_type=F32)


def _dot_nt(a, b):
    return lax.dot_general(a, b, (((1,), (1,)), ((), ())), preferred_element_type=F32)


def _dot_tn(a, b):
    return lax.dot_general(a, b, (((0,), (0,)), ((), ())), preferred_element_type=F32)


def _bf(x):
    return x.astype(BF16)


def _split3(x):
    hi = x.astype(BF16)
    r1 = x - hi.astype(F32)
    mid = r1.astype(BF16)
    lo = (r1 - mid.astype(F32)).astype(BF16)
    return hi, mid, lo


def _cumsum_rows(tri, x):
    hi, mid, lo = _split3(x)
    return _dot(tri, hi) + _dot(tri, mid) + _dot(tri, lo)


def _cumsum_cols(x, triu):
    hi, mid, lo = _split3(x)
    return _dot(hi, triu) + _dot(mid, triu) + _dot(lo, triu)


def _iota2(shape, dim):
    return lax.broadcasted_iota(jnp.int32, shape, dim)


def _log_sigmoid(x):
    return jnp.minimum(x, 0.0) - jnp.log1p(jnp.exp(-jnp.abs(x)))


def _sigmoid(x):
    return 1.0 / (1.0 + jnp.exp(-x))


def _silu(x):
    return x * _sigmoid(x)


def _normproj_kernel(x_ref, g_ref, w_ref, o_ref, h_ref):
    @pl.when(pl.program_id(1) == 0)
    def _():
        x = x_ref[...]
        ms = jnp.mean(x * x, axis=-1, keepdims=True)
        h_ref[...] = ((x * lax.rsqrt(ms + EPS)) * g_ref[...]).astype(BF16)

    o_ref[...] = _dot(h_ref[...], w_ref[...])


def _normproj(x, g, w, tm, tn):
    M, K = x.shape
    N = w.shape[1]
    return pl.pallas_call(
        _normproj_kernel,
        grid=(M // tm, N // tn),
        in_specs=[pl.BlockSpec((tm, K), lambda i, j: (i, 0)),
                  pl.BlockSpec((1, K), lambda i, j: (0, 0)),
                  pl.BlockSpec((K, tn), lambda i, j: (0, j))],
        out_specs=pl.BlockSpec((tm, tn), lambda i, j: (i, j)),
        out_shape=jax.ShapeDtypeStruct((M, N), F32),
        scratch_shapes=[pltpu.VMEM((tm, K), BF16)],
        compiler_params=_cparams("parallel", "arbitrary"),
        name="normproj",
    )(x, g, w)


def _rwkv_kernel(xr_ref, xk_ref, xv_ref, xwa_ref, z_ref, sp_ref, mu_ref, w2_ref, a2_ref, w0_ref, a0_ref,
                 kkw_ref, ka_ref, rk_ref, lng_ref, s0_ref, og_ref, s_ref,
                 s_scr, phi_scr, psi_scr, qt_scr, egl_scr, carry_scr,
                 r_scr, lw_scr, k_scr, v_scr, kk_scr, b_scr, y_scr, *, L):
    t = pl.program_id(1)
    HG, D = RWKV_H, RWKV_D
    hs = range(HG)
    sls = [slice(h * D, (h + 1) * D) for h in hs]

    @pl.when(t == 0)
    def _():
        s_scr[...] = s0_ref[0]
        carry_scr[...] = sp_ref[0]

    def cat(parts):
        return jnp.concatenate(parts, axis=1)

    tt = xr_ref.shape[0]
    first = _iota2((tt, 1), 0) == 0

    def mix(x_ref, lo, hi):
        x = x_ref[...]
        prev = jnp.where(first, carry_scr[:, lo:hi], pltpu.roll(x, 1, 0))
        carry_scr[:, lo:hi] = x[tt - 1:tt, :]
        return x + (prev - x) * mu_ref[:, lo:hi]

    r_ = mix(xr_ref, 0, BW)
    k_ = mix(xk_ref, BW, 2 * BW)
    v_ = mix(xv_ref, 2 * BW, 3 * BW)
    wa = mix(xwa_ref, 3 * BW, RWKV_SHIFT_W)
    lw_scr[...] = -RWKV_DECAY_SCALE * _sigmoid(w0_ref[...] + _dot(_bf(jnp.tanh(wa)), w2_ref[...]))
    a_ = _sigmoid(a0_ref[...] + _dot(_bf(wa), a2_ref[...]))
    ones_blk = (_iota2((LANE, LANE), 0) // D == _iota2((LANE, LANE), 1) // D).astype(BF16)

    def headsum(x):
        hi = _bf(x)
        lo = _bf(x - hi.astype(F32))
        return cat([_dot(hi[:, j:j + LANE], ones_blk) + _dot(lo[:, j:j + LANE], ones_blk)
                    for j in range(0, BW, LANE)])

    kkf = k_ * kkw_ref[...]
    kk = kkf / jnp.maximum(jnp.sqrt(headsum(kkf * kkf)), 1e-12)
    r_scr[...] = r_
    k_scr[...] = k_ * (1.0 + (a_ - 1.0) * ka_ref[...])
    v_scr[...] = v_
    kk_scr[...] = kk
    b_scr[...] = kk * a_

    nchunk = tt // L
    row, col = _iota2((L, L), 0), _iota2((L, L), 1)
    tri = (row >= col).astype(BF16)
    strict = row > col
    incl2 = _iota2((L, 2 * L), 0) >= _iota2((L, 2 * L), 1) % L
    eye = (row == col).astype(F32)

    U1 = 4 if nchunk % 4 == 0 else 1
    it = range(U1 * HG)
    isl = [sls[i % HG] for i in it]

    def phase1(ci, carry):
        cidx = [ci * U1 + u for u in range(U1)]
        rws = [pl.ds(pl.multiple_of(c * L, L), L) for c in cidx]
        kkp, rg, kn, bn, kend, bend, v_c = [], [], [], [], [], [], []
        for u, c in enumerate(cidx):
            lw, k_c, b_c = lw_scr[rws[u], :], k_scr[rws[u], :], b_scr[rws[u], :]
            g = _cumsum_rows(tri, lw)
            gl = g[L - 1:L, :]
            kkp.append(kk_scr[rws[u], :] * jnp.exp(g - lw))
            rg.append(r_scr[rws[u], :] * jnp.exp(g))
            eng = jnp.exp(-g)
            kn.append(k_c * eng)
            bn.append(b_c * eng)
            ee = jnp.exp(gl - g)
            kend.append(k_c * ee)
            bend.append(b_c * ee)
            v_c.append(v_scr[rws[u], :])
            egl_scr[c] = jnp.exp(gl)
        kkp_h = [kkp[i // HG][:, isl[i]] for i in it]
        rg_h = [rg[i // HG][:, isl[i]] for i in it]
        bk = [_bf(jnp.concatenate([bn[i // HG][:, isl[i]], kn[i // HG][:, isl[i]]], axis=0)) for i in it]
        kr = [_bf(jnp.concatenate([kkp_h[i], rg_h[i]], axis=0)) for i in it]
        mnaq = [_dot_nt(kr[i], bk[i]) for i in it]
        mn = [x[:L, :] for x in mnaq]
        aq = [x[L:, :] for x in mnaq]
        N = [_bf(jnp.where(strict, m[:, L:], 0.0)) for m in mn]
        vb = [_bf(v_c[i // HG][:, isl[i]]) for i in it]
        nv = [_dot(N[i], vb[i]) for i in it]
        X = [jnp.where(strict, -m[:, :L], 0.0) for m in mn]
        Xb = [_bf(x) for x in X]
        P = [_dot(xb, xb) for xb in Xb]
        tinv = [eye + x for x in X]
        n = 2
        while n < L:
            Pb = [_bf(p) for p in P]
            if 2 * n >= L:
                tinv = [tinv[i] + _dot(_bf(tinv[i]), Pb[i]) for i in it]
            else:
                st = [_dot(_bf(jnp.concatenate([P[i], tinv[i]], axis=0)), Pb[i]) for i in it]
                P = [x[:L, :] for x in st]
                tinv = [tinv[i] + st[i][L:, :] for i in it]
            n *= 2
        gh = [_dot(_bf(tinv[i]), _bf(jnp.concatenate([kkp_h[i], nv[i]], axis=1))) for i in it]
        gm = [_bf(x[:, :D]) for x in gh]
        hm = [_bf(x[:, D:]) for x in gh]
        bend_b = [_bf(bend[i // HG][:, isl[i]]) for i in it]
        aqm = [_bf(jnp.where(incl2, a, 0.0)) for a in aq]
        hv = [jnp.concatenate([-hm[i], vb[i]], axis=0) for i in it]
        bke = [jnp.concatenate([bend_b[i], _bf(kend[i // HG][:, isl[i]])], axis=0) for i in it]
        phi = [_dot_tn(gm[i], bend_b[i]) for i in it]
        psi = [_dot_tn(hv[i], bke[i]) for i in it]
        qts = [rg_h[i] - _dot(aqm[i][:, :L], gm[i]) for i in it]
        y0s = [_dot(aqm[i], hv[i]) for i in it]
        for i in it:
            phi_scr[cidx[i // HG], i % HG] = _bf(phi[i])
            psi_scr[cidx[i // HG], i % HG] = psi[i]
        for u in range(U1):
            qt_scr[rws[u], :] = _bf(cat(qts[u * HG:(u + 1) * HG]))
            y_scr[rws[u], :] = cat(y0s[u * HG:(u + 1) * HG])
        return carry

    lax.fori_loop(0, nchunk // U1, phase1, 0)

    def phase2(c, carry):
        rows = pl.ds(pl.multiple_of(c * L, L), L)
        qt = qt_scr[rows, :]
        egl = egl_scr[c]
        S = [s_scr[h] for h in hs]
        Sb = [_bf(s_) for s_ in S]
        sphi = [_dot(Sb[h], phi_scr[c, h]) for h in hs]
        ys = [_dot_nt(qt[:, sls[h]], Sb[h]) for h in hs]
        for h in hs:
            s_scr[h] = S[h] * egl[:, sls[h]] - sphi[h] + psi_scr[c, h]
        y_scr[rows, :] += cat(ys)
        return carry

    lax.fori_loop(0, nchunk, phase2, 0)

    y = y_scr[...]
    rk = r_scr[...] * k_scr[...] * rk_ref[...]
    cen = y - headsum(y) * (1.0 / D)
    ln = cen * lax.rsqrt(headsum(cen * cen) * (1.0 / D) + EPS)
    out = ln * lng_ref[...] + headsum(rk) * v_scr[...]
    og_ref[...] = _bf(out * _silu(z_ref[...]))

    @pl.when(t == pl.num_programs(1) - 1)
    def _():
        s_ref[0] = s_scr[...]


def _rwkv(P, B, T, p, s0, shift_prev):
    L = min(T, CHUNK)
    tt = min(T, 256)
    nT = T // tt
    nc = tt // L
    off = _OFF['rwkv_shift']
    rowblk = lambda w, o: pl.BlockSpec((tt, w), lambda b_, t: (b_ * nT + t, o // w))
    const = lambda shape: pl.BlockSpec(shape, lambda b_, t: (0,) * len(shape))
    st = pl.BlockSpec((1, RWKV_H, RWKV_D, RWKV_D), lambda b_, t: (b_, 0, 0, 0))
    vec = const((1, BW))
    return pl.pallas_call(
        functools.partial(_rwkv_kernel, L=L),
        grid=(B, nT),
        in_specs=[rowblk(BW, off), rowblk(BW, off + BW), rowblk(BW, off + 2 * BW), rowblk(LANE, off + 3 * BW),
                  rowblk(BW, _OFF['rwkv_z']),
                  pl.BlockSpec((1, 1, RWKV_SHIFT_W), lambda b_, t: (b_, 0, 0)), const((1, RWKV_SHIFT_W)),
                  const((LANE, BW)), const((LANE, BW)), vec, vec, vec, vec, vec, vec, st],
        out_specs=[pl.BlockSpec((tt, BW), lambda b_, t: (b_ * nT + t, 0)), st],
        out_shape=[jax.ShapeDtypeStruct((B * T, BW), BF16), jax.ShapeDtypeStruct(s0.shape, F32)],
        scratch_shapes=[pltpu.VMEM((RWKV_H, RWKV_D, RWKV_D), F32),
                        pltpu.VMEM((nc, RWKV_H, RWKV_D, RWKV_D), BF16),
                        pltpu.VMEM((nc, RWKV_H, RWKV_D, RWKV_D), F32),
                        pltpu.VMEM((tt, BW), BF16),
                        pltpu.VMEM((nc, 1, BW), F32),
                        pltpu.VMEM((1, RWKV_SHIFT_W), F32)] + [pltpu.VMEM((tt, BW), F32)] * 7,
        compiler_params=_cparams("parallel", "arbitrary"),
        name="rwkv_scan",
    )(P, P, P, P, P, shift_prev, p['rwkv_mu'], p['rwkv_w2p'], p['rwkv_a2p'], p['rwkv_w0'], p['rwkv_a0'],
      p['rwkv_k_k'], p['rwkv_k_a'], p['rwkv_r_k'], p['rwkv_ln_g'], s0)


def _gla_kernel(q_ref, k_ref, v_ref, gd_ref, z_ref, g2_ref, gb_ref, ng_ref, s0_ref, o_ref, s_ref, s_scr, g_scr, *, L):
    t = pl.program_id(1)

    @pl.when(t == 0)
    def _():
        s_scr[...] = s0_ref[0]

    gk = _dot(_bf(gd_ref[...]), g2_ref[...]) + gb_ref[...]
    g_scr[...] = _log_sigmoid(gk) * (1.0 / GLA_GATE_NORM)

    nchunk = q_ref.shape[0] // L
    row, col = _iota2((L, L), 0), _iota2((L, L), 1)
    tri = (row >= col).astype(BF16)
    causal = row >= col
    scale = GLA_DK ** -0.5

    U = 2 if nchunk % 2 == 0 else 1
    hs, us = range(GLA_H), range(U)
    ksl = [slice(h * GLA_DK, (h + 1) * GLA_DK) for h in hs]
    vsl = [slice(h * GLA_DV, (h + 1) * GLA_DV) for h in hs]
    uh = [(u, h) for u in us for h in hs]

    def chunks(i, carry):
        rws = [pl.ds(pl.multiple_of((i * U + u) * L, L), L) for u in us]
        q = [q_ref[r, :] * scale for r in rws]
        k = [k_ref[r, :] for r in rws]
        b = [_cumsum_rows(tri, g_scr[r, :]) for r in rws]
        b_end = [x[L - 1:L, :] for x in b]
        qi = [_bf(q[u] * jnp.exp(b[u])) for u in us]
        ki = [_bf(k[u] * jnp.exp(-b[u])) for u in us]
        kend = [_bf(k[u] * jnp.exp(b_end[u] - b[u])) for u in us]
        eb_end = [jnp.exp(x) for x in b_end]
        vb = {(u, h): _bf(v_ref[rws[u], vsl[h]]) for u, h in uh}
        A = {(u, h): _bf(jnp.where(causal, _dot_nt(qi[u][:, ksl[h]], ki[u][:, ksl[h]]), 0.0)) for u, h in uh}
        kv = {(u, h): _dot_tn(vb[u, h], kend[u][:, ksl[h]]) for u, h in uh}
        av = {(u, h): _dot(A[u, h], vb[u, h]) for u, h in uh}
        St = [s_scr[h] for h in hs]
        o = {}
        for u in us:
            for h in hs:
                o[u, h] = av[u, h] + _dot_nt(qi[u][:, ksl[h]], _bf(St[h]))
            St = [St[h] * eb_end[u][:, ksl[h]] + kv[u, h] for h in hs]
        ms = {x: jnp.mean(o[x] * o[x], axis=-1, keepdims=True) for x in uh}
        for u, h in uh:
            o_ref[rws[u], vsl[h]] = _bf((o[u, h] * lax.rsqrt(ms[u, h] + EPS)) * ng_ref[:, vsl[h]]
                                        * _silu(z_ref[rws[u], vsl[h]]))
        for h in hs:
            s_scr[h] = St[h]
        return carry

    lax.fori_loop(0, nchunk // U, chunks, 0)

    @pl.when(t == pl.num_programs(1) - 1)
    def _():
        s_ref[0] = s_scr[...]


def _gla(P, B, T, g2p, gb, ng, s0t):
    L = min(T, CHUNK)
    tt = min(T, 512)
    nT = T // tt
    rowblk = lambda w, off: pl.BlockSpec((tt, w), lambda b_, t: (b_ * nT + t, off // w))
    const = lambda shape: pl.BlockSpec(shape, lambda b_, t: (0,) * len(shape))
    st = pl.BlockSpec((1, GLA_H, GLA_DV, GLA_DK), lambda b_, t: (b_, 0, 0, 0))
    return pl.pallas_call(
        functools.partial(_gla_kernel, L=L),
        grid=(B, nT),
        in_specs=[rowblk(512, _OFF['gla_q']), rowblk(512, _OFF['gla_k']), rowblk(BW, _OFF['gla_v']),
                  rowblk(LANE, _OFF['gla_gd']), rowblk(BW, _OFF['gla_z']), const((LANE, 512)), const((1, 512)),
                  const((1, BW)), st],
        out_specs=[pl.BlockSpec((tt, BW), lambda b_, t: (b_ * nT + t, 0)), st],
        out_shape=[jax.ShapeDtypeStruct((B * T, BW), BF16), jax.ShapeDtypeStruct(s0t.shape, F32)],
        scratch_shapes=[pltpu.VMEM((GLA_H, GLA_DV, GLA_DK), F32), pltpu.VMEM((tt, 512), F32)],
        compiler_params=_cparams("parallel", "arbitrary"),
        name="gla_scan",
    )(P, P, P, P, P, g2p, gb, ng, s0t)


def _ret_kernel(q_ref, k_ref, v_ref, z_ref, cos_ref, sin_ref, s0_ref, o_ref, s_ref, s_scr, *, L):
    t = pl.program_id(1)

    @pl.when(t == 0)
    def _():
        s_scr[...] = s0_ref[0]

    nchunk = q_ref.shape[0] // L
    rel = (_iota2((L, L), 0) - _iota2((L, L), 1)).astype(F32)
    tcol = _iota2((L, 1), 0).astype(F32)
    scale = RET_DK ** -0.5
    lgs = [math.log(1.0 - 2.0 ** (-5.0 - h)) for h in range(RET_H)]
    decay = [jnp.where(rel >= 0, jnp.exp(lg * jnp.maximum(rel, 0.0)), 0.0) for lg in lgs]
    rowd = [jnp.exp(lg * (tcol + 1.0)) for lg in lgs]
    cold = [jnp.exp(lg * (L - 1.0 - tcol)) for lg in lgs]

    def rope(x, cos2, sin2):
        return x * cos2 + pltpu.roll(x, RET_DK // 2, 1) * sin2

    U = 2 if nchunk % 2 == 0 else 1
    hs, us = range(RET_H), range(U)
    ksl = [slice(h * RET_DK, (h + 1) * RET_DK) for h in hs]
    vsl = [slice(h * RET_DV, (h + 1) * RET_DV) for h in hs]
    uh = [(u, h) for u in us for h in hs]

    def chunks(i, carry):
        rws = [pl.ds(pl.multiple_of((i * U + u) * L, L), L) for u in us]
        cs = [(cos_ref[r, :], sin_ref[r, :]) for r in rws]
        k = {(u, h): rope(k_ref[rws[u], ksl[h]], *cs[u]) for u, h in uh}
        qb = {(u, h): _bf(rope(q_ref[rws[u], ksl[h]], *cs[u]) * scale) for u, h in uh}
        kb = {x: _bf(k[x]) for x in uh}
        kcb = {(u, h): _bf(k[u, h] * cold[h]) for u, h in uh}
        vb = {(u, h): _bf(v_ref[rws[u], vsl[h]]) for u, h in uh}
        A = {(u, h): _bf(_dot_nt(qb[u, h], kb[u, h]) * decay[h]) for u, h in uh}
        kv = {x: _dot_tn(kcb[x], vb[x]) for x in uh}
        av = {x: _dot(A[x], vb[x]) for x in uh}
        S = [s_scr[h] for h in hs]
        o = {}
        for u in us:
            for h in hs:
                o[u, h] = av[u, h] + rowd[h] * _dot(qb[u, h], _bf(S[h]))
            S = [math.exp(lgs[h] * L) * S[h] + kv[u, h] for h in hs]
        ms = {x: jnp.mean(o[x] * o[x], axis=-1, keepdims=True) for x in uh}
        for u, h in uh:
            o_ref[rws[u], vsl[h]] = _bf(o[u, h] * lax.rsqrt(ms[u, h] + EPS) * _silu(z_ref[rws[u], vsl[h]]))
        for h in hs:
            s_scr[h] = S[h]
        return carry

    lax.fori_loop(0, nchunk // U, chunks, 0)

    @pl.when(t == pl.num_programs(1) - 1)
    def _():
        s_ref[0] = s_scr[...]


def _ret(P, B, T, cos2, sin2, s0):
    L = min(T, CHUNK)
    tt = min(T, 512)
    nT = T // tt
    rowblk = lambda w, off: pl.BlockSpec((tt, w), lambda b_, t: (b_ * nT + t, off // w))
    tab = pl.BlockSpec((tt, RET_DK), lambda b_, t: (t, 0))
    st = pl.BlockSpec((1, RET_H, RET_DK, RET_DV), lambda b_, t: (b_, 0, 0, 0))
    return pl.pallas_call(
        functools.partial(_ret_kernel, L=L),
        grid=(B, nT),
        in_specs=[rowblk(512, _OFF['ret_q']), rowblk(512, _OFF['ret_k']), rowblk(BW, _OFF['ret_v']),
                  rowblk(BW, _OFF['ret_z']), tab, tab, st],
        out_specs=[pl.BlockSpec((tt, BW), lambda b_, t: (b_ * nT + t, 0)), st],
        out_shape=[jax.ShapeDtypeStruct((B * T, BW), BF16), jax.ShapeDtypeStruct(s0.shape, F32)],
        scratch_shapes=[pltpu.VMEM((RET_H, RET_DK, RET_DV), F32)],
        compiler_params=_cparams("parallel", "arbitrary"),
        name="ret_scan",
    )(P, P, P, P, cos2, sin2, s0)


def _mlstm_pre_kernel(xm_ref, cp_ref, cw_ref, cb_ref, wq_ref, wk_ref, wv_ref, wif_ref, bif_ref,
                      xc_ref, q_ref, k_ref, v_ref, g_ref, carry_scr):
    t = pl.program_id(1)
    tm = xm_ref.shape[0]
    nprev = ML_CONV - 1

    @pl.when(t == 0)
    def _():
        carry_scr[...] = jnp.zeros_like(carry_scr)
        carry_scr[8 - nprev:8, :] = cp_ref[0]

    xm = xm_ref[...]
    c8 = carry_scr[...]
    rid = _iota2((8, 1), 0)
    conv = xm * cw_ref[nprev:nprev + 1, :] + cb_ref[...]
    for j in range(1, ML_CONV):
        rolled = pltpu.roll(xm, j, 0)
        head = jnp.where(rid < j, pltpu.roll(c8, j, 0), rolled[0:8, :])
        prev = jnp.concatenate([head, rolled[8:, :]], axis=0) if tm > 8 else head
        conv = conv + prev * cw_ref[nprev - j:nprev - j + 1, :]
    carry_scr[...] = xm[tm - 8:tm, :]
    xc = _silu(conv)
    xc_ref[...] = xc
    xcb, xmb = _bf(xc), _bf(xm)
    q = _dot(xcb, wq_ref[...])
    k = _dot(xcb, wk_ref[...])
    v = _dot(xmb, wv_ref[...])
    g_ref[...] = (_dot(_bf(q), wif_ref[0]) + _dot(_bf(k), wif_ref[1]) + _dot(_bf(v), wif_ref[2])) + bif_ref[...]
    q_ref[...] = q
    k_ref[...] = k * (ML_D ** -0.5)
    v_ref[...] = v


def _mlstm_pre(P, B, T, conv_prev, p):
    tm = min(T, 512)
    nT = T // tm
    row = pl.BlockSpec((tm, BW), lambda b_, t: (b_ * nT + t, 0))
    const = lambda shape: pl.BlockSpec(shape, lambda b_, t: (0,) * len(shape))
    M = B * T
    return pl.pallas_call(
        _mlstm_pre_kernel,
        grid=(B, nT),
        in_specs=[pl.BlockSpec((tm, BW), lambda b_, t: (b_ * nT + t, _OFF['ml_x'] // BW)),
                  pl.BlockSpec((1, ML_CONV - 1, BW), lambda b_, t: (b_, 0, 0)),
                  const((ML_CONV, BW)), const((1, BW)), const((BW, BW)), const((BW, BW)), const((BW, BW)),
                  const((3, BW, LANE)), const((1, LANE))],
        out_specs=[row, row, row, row, pl.BlockSpec((tm, LANE), lambda b_, t: (b_ * nT + t, 0))],
        out_shape=[jax.ShapeDtypeStruct((M, BW), F32)] * 4 + [jax.ShapeDtypeStruct((M, LANE), F32)],
        scratch_shapes=[pltpu.VMEM((8, BW), F32)],
        compiler_params=_cparams("parallel", "arbitrary"),
        name="mlstm_pre",
    )(P, conv_prev, p['ml_conv_w'], p['ml_conv_b'], p['ml_wq'], p['ml_wk'], p['ml_wv'], p['ml_wif'], p['ml_bif'])


def _mlstm_kernel(q_ref, k_ref, v_ref, gc_ref, gr_ref, xc_ref, z_ref, ng_ref, sk_ref, c0_ref, n0_ref, m0_ref,
                  o_ref, c_ref, n_ref, m_ref, c_scr, n_scr, m_scr, *, L):
    t = pl.program_id(1)

    @pl.when(t == 0)
    def _():
        c_scr[...] = c0_ref[0]
        n_scr[...] = n0_ref[0]
        m_scr[...] = m0_ref[0]

    nchunk = q_ref.shape[0] // L
    row, col = _iota2((L, L), 0), _iota2((L, L), 1)
    tri = (row >= col).astype(BF16)
    triu = (row <= col).astype(BF16)
    causal = row >= col

    U = 2 if nchunk % 2 == 0 else 1
    hs, us = range(ML_H), range(U)
    sl = [slice(h * ML_D, (h + 1) * ML_D) for h in hs]
    uh = [(u, h) for u in us for h in hs]

    def chunks(i, carry):
        cidx = [i * U + u for u in us]
        rws = [pl.ds(pl.multiple_of(c * L, L), L) for c in cidx]
        gcb = [gc_ref[0, c] for c in cidx]
        grb = [gr_ref[0, c] for c in cidx]
        bc = [_cumsum_rows(tri, x) for x in gcb]
        br = [_cumsum_cols(x, triu) for x in grb]
        q = {(u, h): q_ref[rws[u], sl[h]] for u, h in uh}
        k = {(u, h): k_ref[rws[u], sl[h]] for u, h in uh}
        qb = {x: _bf(q[x]) for x in uh}
        vb = {(u, h): _bf(v_ref[rws[u], sl[h]]) for u, h in uh}
        qk = {x: _dot_nt(qb[x], _bf(k[x])) for x in uh}
        b_c = {(u, h): bc[u][:, ML_H + h:ML_H + h + 1] for u, h in uh}
        dlog = {(u, h): jnp.where(causal, b_c[u, h] - br[u][ML_H + h:ML_H + h + 1, :] + grb[u][h:h + 1, :], -jnp.inf)
                for u, h in uh}
        m_loc = {x: jnp.max(dlog[x], axis=-1, keepdims=True) for x in uh}
        s_loc = {x: qk[x] * jnp.exp(dlog[x] - m_loc[x]) for x in uh}
        rs = {x: jnp.sum(s_loc[x], axis=-1, keepdims=True) for x in uh}
        b_end = {x: b_c[x][L - 1:L, :] for x in uh}
        m_le = {x: m_loc[x][L - 1:L, :] for x in uh}
        kw = {(u, h): k[u, h] * jnp.exp(b_end[u, h] - b_c[u, h] + gcb[u][:, h:h + 1] - m_le[u, h]) for u, h in uh}
        sv = {x: _dot(_bf(s_loc[x]), vb[x]) for x in uh}
        kv = {x: _dot_tn(_bf(kw[x]), vb[x]) for x in uh}
        ksum = {x: jnp.sum(kw[x], axis=0, keepdims=True) for x in uh}
        C = [c_scr[h] for h in hs]
        nvec = [n_scr[h] for h in hs]
        m_prev = [m_scr[h][:, 0:1] for h in hs]
        hh = {}
        for u in us:
            qc = [_dot(qb[u, h], _bf(C[h])) for h in hs]
            qn = [jnp.sum(q[u, h] * nvec[h], axis=-1, keepdims=True) for h in hs]
            from_state = [b_c[u, h] + m_prev[h] for h in hs]
            m_t = [jnp.maximum(from_state[h], m_loc[u, h]) for h in hs]
            a_in = [jnp.exp(m_loc[u, h] - m_t[h]) for h in hs]
            w_state = [jnp.exp(from_state[h] - m_t[h]) for h in hs]
            den = [a_in[h] * rs[u, h] + w_state[h] * qn[h] for h in hs]
            for h in hs:
                hh[u, h] = ((a_in[h] * sv[u, h] + w_state[h] * qc[h])
                            / jnp.maximum(jnp.abs(den[h]), jnp.exp(-m_t[h])))
            m_new = [m_t[h][L - 1:L, :] for h in hs]
            cd = [jnp.exp(b_end[u, h] + m_prev[h] - m_new[h]) for h in hs]
            sc = [jnp.exp(m_le[u, h] - m_new[h]) for h in hs]
            C = [cd[h] * C[h] + sc[h] * kv[u, h] for h in hs]
            nvec = [cd[h] * nvec[h] + sc[h] * ksum[u, h] for h in hs]
            m_prev = m_new
        mean = {x: jnp.mean(hh[x], axis=-1, keepdims=True) for x in uh}
        cen = {x: hh[x] - mean[x] for x in uh}
        var = {x: jnp.mean(cen[x] * cen[x], axis=-1, keepdims=True) for x in uh}
        for u, h in uh:
            y = cen[u, h] * lax.rsqrt(var[u, h] + EPS)
            o_ref[rws[u], sl[h]] = _bf((y * ng_ref[:, sl[h]] + sk_ref[:, sl[h]] * xc_ref[rws[u], sl[h]])
                                       * _silu(z_ref[rws[u], sl[h]]))
        for h in hs:
            c_scr[h] = C[h]
            n_scr[h] = nvec[h]
            m_scr[h] = jnp.broadcast_to(m_prev[h], (1, LANE))
        return carry

    lax.fori_loop(0, nchunk // U, chunks, 0)

    @pl.when(t == pl.num_programs(1) - 1)
    def _():
        c_ref[0] = c_scr[...]
        n_ref[0] = n_scr[...]
        m_ref[0] = m_scr[...]


def _mlstm(q, k, v, gc, gr, xc, P, ng, sk, c0, n0, m0, B, T):
    L = min(T, CHUNK)
    tt = min(T, 512)
    nT = T // tt
    nc = tt // L
    row = pl.BlockSpec((tt, BW), lambda b_, t: (b_ * nT + t, 0))
    const = pl.BlockSpec((1, BW), lambda b_, t: (0, 0))
    stc = pl.BlockSpec((1, ML_H, ML_D, ML_D), lambda b_, t: (b_, 0, 0, 0))
    stn = pl.BlockSpec((1, ML_H, 1, ML_D), lambda b_, t: (b_, 0, 0, 0))
    stm = pl.BlockSpec((1, ML_H, 1, LANE), lambda b_, t: (b_, 0, 0, 0))
    return pl.pallas_call(
        functools.partial(_mlstm_kernel, L=L),
        grid=(B, nT),
        in_specs=[row, row, row,
                  pl.BlockSpec((1, nc, L, 8), lambda b_, t: (b_, t, 0, 0)),
                  pl.BlockSpec((1, nc, 8, L), lambda b_, t: (b_, t, 0, 0)),
                  row, pl.BlockSpec((tt, BW), lambda b_, t: (b_ * nT + t, _OFF['ml_z'] // BW)), const, const, stc, stn, stm],
        out_specs=[row, stc, stn, stm],
        out_shape=[jax.ShapeDtypeStruct((B * T, BW), BF16), jax.ShapeDtypeStruct(c0.shape, F32),
                   jax.ShapeDtypeStruct(n0.shape, F32), jax.ShapeDtypeStruct(m0.shape, F32)],
        scratch_shapes=[pltpu.VMEM((ML_H, ML_D, ML_D), F32), pltpu.VMEM((ML_H, 1, ML_D), F32),
                        pltpu.VMEM((ML_H, 1, LANE), F32)],
        compiler_params=_cparams("parallel", "arbitrary"),
        name="mlstm_scan",
    )(q, k, v, gc, gr, xc, P, ng, sk, c0, n0, m0)


def _xattn_kernel(q_ref, z_ref, mk_ref, mv_ref, o_ref):
    hs = range(XA_H)
    sl = [slice(h * XA_D, (h + 1) * XA_D) for h in hs]
    s = [_dot_nt(_bf(q_ref[:, x]), _bf(mk_ref[0, :, x])) * (XA_D ** -0.5) for x in sl]
    e = [jnp.exp(x - jnp.max(x, axis=-1, keepdims=True)) for x in s]
    prob = [_bf(x / jnp.sum(x, axis=-1, keepdims=True)) for x in e]
    o = [_dot(prob[h], _bf(mv_ref[0, :, sl[h]])) for h in hs]
    for h in hs:
        o_ref[:, sl[h]] = _bf(o[h] * _silu(z_ref[:, sl[h]]))


def _xattn(P, B, T, mk, mv):
    tt = min(T, 512)
    nT = T // tt
    n_mem = mk.shape[1]
    mem = pl.BlockSpec((1, n_mem, BW), lambda b_, t: (b_, 0, 0))
    return pl.pallas_call(
        _xattn_kernel,
        grid=(B, nT),
        in_specs=[pl.BlockSpec((tt, BW), lambda b_, t: (b_ * nT + t, _OFF['xa_q'] // BW)),
                  pl.BlockSpec((tt, BW), lambda b_, t: (b_ * nT + t, _OFF['xa_z'] // BW)), mem, mem],
        out_specs=pl.BlockSpec((tt, BW), lambda b_, t: (b_ * nT + t, 0)),
        out_shape=jax.ShapeDtypeStruct((B * T, BW), BF16),
        compiler_params=_cparams("parallel", "parallel"),
        name="xattn",
    )(P, P, mk, mv)


def _merge_kernel(o0, o1, o2, o3, o4, g_ref, w_ref, out_ref):
    ys = [_dot(o_ref[...], w_ref[i]) for i, o_ref in enumerate((o0, o1, o2, o3, o4))]
    acc = None
    for i, y in enumerate(ys):
        term = _sigmoid(g_ref[:, i * D_MODEL:(i + 1) * D_MODEL]) * y
        acc = term if acc is None else acc + term
    out_ref[...] = _bf(acc)


def _merge(ogs, P, wb, tm):
    M = ogs[0].shape[0]
    gw = N_BRANCH * D_MODEL
    og = pl.BlockSpec((tm, BW), lambda i: (i, 0))
    return pl.pallas_call(
        _merge_kernel,
        grid=(M // tm,),
        in_specs=[og] * N_BRANCH + [pl.BlockSpec((tm, gw), lambda i: (i, _OFF['gates'] // gw)),
                                    pl.BlockSpec((N_BRANCH, BW, D_MODEL), lambda i: (0, 0, 0),
                                                 pipeline_mode=pl.Buffered(1))],
        out_specs=pl.BlockSpec((tm, D_MODEL), lambda i: (i, 0)),
        out_shape=jax.ShapeDtypeStruct((M, D_MODEL), BF16),
        compiler_params=_cparams("parallel"),
        name="merge",
    )(*ogs, P, wb)


def _resid_kernel(x_ref, m_ref, w_ref, o_ref):
    o_ref[...] = x_ref[...] + _dot(m_ref[...], w_ref[...])


def _resid_out(x, merged, w, tm, tn):
    M = x.shape[0]
    return pl.pallas_call(
        _resid_kernel,
        grid=(M // tm, D_MODEL // tn),
        in_specs=[pl.BlockSpec((tm, tn), lambda i, j: (i, j)),
                  pl.BlockSpec((tm, D_MODEL), lambda i, j: (i, 0)),
                  pl.BlockSpec((D_MODEL, tn), lambda i, j: (0, j))],
        out_specs=pl.BlockSpec((tm, tn), lambda i, j: (i, j)),
        out_shape=jax.ShapeDtypeStruct((M, D_MODEL), F32),
        compiler_params=_cparams("parallel", "parallel"),
        name="resid_out",
    )(x, merged, w)


def _rmsnorm_kernel(x_ref, g_ref, o_ref):
    x = x_ref[...]
    ms = jnp.mean(x * x, axis=-1, keepdims=True)
    o_ref[...] = (x * lax.rsqrt(ms + EPS)) * g_ref[...]


def _rmsnorm(x, g, tm):
    M, K = x.shape
    return pl.pallas_call(
        _rmsnorm_kernel,
        grid=(M // tm,),
        in_specs=[pl.BlockSpec((tm, K), lambda i: (i, 0)), pl.BlockSpec((1, K), lambda i: (0, 0))],
        out_specs=pl.BlockSpec((tm, K), lambda i: (i, 0)),
        out_shape=jax.ShapeDtypeStruct((M, K), F32),
        compiler_params=_cparams("parallel"),
        name="rmsnorm",
    )(x, g)


def _blockdiag(w):
    depth, n, c, d = w.shape
    rows = w.reshape(depth, n * c, d)
    tile = (jnp.arange(d)[:, None] == jnp.arange(n * d)[None, :] % d).astype(w.dtype)
    dense = jnp.einsum('lrd,dj->lrj', rows, tile)
    mask = (jnp.arange(n * c)[:, None] // c) == (jnp.arange(n * d)[None, :] // d)
    return jnp.where(mask, dense, 0.0).astype(BF16)


def _rope_tables(pos):
    half = RET_DK // 2
    inv = ROPE_BASE ** (-jnp.linspace(0.0, 1.0, half, dtype=F32))
    ang = pos[:, None] * inv[None, :]
    cos, sin = jnp.cos(ang), jnp.sin(ang)
    return jnp.concatenate([cos, cos], axis=-1), jnp.concatenate([-sin, sin], axis=-1)


def _layer(x, B, T, tabs, st, mem_k, mem_v, p):
    M = B * T
    tm = min(M, 1024)
    P = _normproj(x, p['norm_g'], p['w_pack'], tm, PROJ_TN)
    P3 = P.reshape(B, T, NP)

    og_a, s_rwkv = _rwkv(P, B, T, p, st['rwkv'], st['rwkv_shift'])
    shift_new = P3[:, T - 1:, _OFF['rwkv_shift']:_OFF['rwkv_shift'] + RWKV_SHIFT_W]

    og_b, s_gla_t = _gla(P, B, T, p['gla_g2p'], p['gla_gb'], p['gla_norm_g'], jnp.swapaxes(st['gla'], -1, -2))
    s_gla = jnp.swapaxes(s_gla_t, -1, -2)

    og_c, s_ret = _ret(P, B, T, tabs[0], tabs[1], st['ret'])

    xc, q_m, k_m, v_m, gates = _mlstm_pre(P, B, T, st['ml_conv'], p)
    L = min(T, CHUNK)
    gcol = jnp.concatenate([gates[:, :ML_H], _log_sigmoid(gates[:, ML_H:2 * ML_H])], axis=-1).reshape(B, T // L, L, 2 * ML_H)
    grow = jnp.swapaxes(gcol, -1, -2)
    og_d, c_new, n_new, m_new = _mlstm(
        q_m, k_m, v_m, gcol, grow, xc, P, p['ml_norm_g'], p['ml_skip'], st['ml_c'], st['ml_n'][:, :, None, :],
        jnp.broadcast_to(st['ml_m'][:, :, None, None], st['ml_m'].shape + (1, LANE)), B, T)
    xm_tail = P3[:, max(T - (ML_CONV - 1), 0):, _OFF['ml_x']:_OFF['ml_x'] + BW]
    conv_new = jnp.concatenate([st['ml_conv'], xm_tail], axis=1)[:, -(ML_CONV - 1):]

    og_x = _xattn(P, B, T, mem_k.reshape(B, -1, BW), mem_v.reshape(B, -1, BW))

    merged = _merge([og_a, og_b, og_c, og_d, og_x], P, p['w_branch'], min(M, 256))
    x_new = _resid_out(x, merged, p['w_out'], min(M, 1024), 512)
    new = {'rwkv': s_rwkv, 'rwkv_shift': shift_new, 'gla': s_gla, 'ret': s_ret, 'ml_c': c_new,
           'ml_n': n_new[:, :, 0, :], 'ml_m': m_new[:, :, 0, 0], 'ml_conv': conv_new}
    return x_new, new


_STATE_KEYS = ('rwkv', 'rwkv_shift', 'gla', 'ret', 'ml_c', 'ml_n', 'ml_m', 'ml_conv')


def kernel(x_prompt, x_sample, mem_prompt, cache_mem_k, cache_mem_v, state_rwkv, state_rwkv_shift, state_gla, state_ret, state_mlstm_c, state_mlstm_n, state_mlstm_m, state_mlstm_conv, norm_g, mem_norm_g, w_in, w_mem_kv, rwkv_mu, rwkv_w0, rwkv_w2, rwkv_a0, rwkv_a2, rwkv_k_k, rwkv_k_a, rwkv_r_k, rwkv_ln_g, gla_g2, gla_gb, gla_norm_g, ml_conv_w, ml_conv_b, ml_wq, ml_wk, ml_wv, ml_w_if, ml_b_if, ml_skip, ml_norm_g, w_branch, w_out, final_norm_g):
    Bp, Tp, D = x_prompt.shape
    Bs, Ts, _ = x_sample.shape
    depth = w_in.shape[0]
    n_mem = mem_prompt.shape[1]

    params = {
        'norm_g': norm_g[:, None, :], 'mem_norm_g': mem_norm_g[:, None, :],
        'w_mem_kv': _bf(w_mem_kv),
        'rwkv_mu': rwkv_mu[:, None, :], 'rwkv_w0': rwkv_w0[:, None, :], 'rwkv_a0': rwkv_a0[:, None, :],
        'rwkv_w2p': _bf(jnp.pad(rwkv_w2, ((0, 0), (0, LANE - RWKV_LORA), (0, 0)))),
        'rwkv_a2p': _bf(jnp.pad(rwkv_a2, ((0, 0), (LANE - RWKV_LORA, 0), (0, 0)))),
        'rwkv_k_k': rwkv_k_k[:, None, :], 'rwkv_k_a': rwkv_k_a[:, None, :], 'rwkv_r_k': rwkv_r_k[:, None, :],
        'rwkv_ln_g': rwkv_ln_g[:, None, :],
        'gla_g2p': _bf(jnp.pad(gla_g2, ((0, 0), (0, LANE - GLA_LORA), (0, 0)))), 'gla_gb': gla_gb[:, None, :],
        'gla_norm_g': gla_norm_g[:, None, :],
        'ml_conv_w': ml_conv_w, 'ml_conv_b': ml_conv_b[:, None, :],
        'ml_wq': _blockdiag(ml_wq), 'ml_wk': _blockdiag(ml_wk), 'ml_wv': _blockdiag(ml_wv),
        'ml_wif': _bf(jnp.pad(ml_w_if.reshape(depth, 3, BW, 2 * ML_H), ((0, 0), (0, 0), (0, 0), (0, LANE - 2 * ML_H)))),
        'ml_bif': jnp.pad(ml_b_if, ((0, 0), (0, LANE - 2 * ML_H)))[:, None, :],
        'ml_skip': ml_skip[:, None, :], 'ml_norm_g': ml_norm_g[:, None, :],
        'w_branch': _bf(w_branch), 'w_out': _bf(w_out),
    }
    cache = {'rwkv': state_rwkv, 'rwkv_shift': state_rwkv_shift, 'gla': state_gla, 'ret': state_ret,
             'ml_c': state_mlstm_c, 'ml_n': state_mlstm_n, 'ml_m': state_mlstm_m, 'ml_conv': state_mlstm_conv,
             'mem_k': cache_mem_k, 'mem_v': cache_mem_v}
    tabs_p = _rope_tables(jnp.arange(Tp, dtype=F32))
    tabs_s = _rope_tables(PAST_LEN + jnp.arange(Ts, dtype=F32))
    zero_p = {
        'rwkv': jnp.zeros((Bp, RWKV_H, RWKV_D, RWKV_D), F32), 'rwkv_shift': jnp.zeros((Bp, 1, RWKV_SHIFT_W), F32),
        'gla': jnp.zeros((Bp, GLA_H, GLA_DK, GLA_DV), F32), 'ret': jnp.zeros((Bp, RET_H, RET_DK, RET_DV), F32),
        'ml_c': jnp.zeros((Bp, ML_H, ML_D, ML_D), F32), 'ml_n': jnp.zeros((Bp, ML_H, ML_D), F32),
        'ml_m': jnp.zeros((Bp, ML_H), F32), 'ml_conv': jnp.zeros((Bp, ML_CONV - 1, BW), F32),
    }
    mem2d = mem_prompt.reshape(Bp * n_mem, D)

    yp, ys = x_prompt.reshape(Bp * Tp, D), x_sample.reshape(Bs * Ts, D)
    outs = []
    for l in range(depth):
        p = {nm: arr[l] for nm, arr in params.items()}
        p['w_pack'] = _pack_w_in(w_in[l])
        kv = _normproj(mem2d, p['mem_norm_g'], p['w_mem_kv'], min(Bp * n_mem, 512), 512)
        mk_l = kv[:, :BW].reshape(Bp, n_mem, XA_H, XA_D)
        mv_l = kv[:, BW:].reshape(Bp, n_mem, XA_H, XA_D)
        yp, stp_l = _layer(yp, Bp, Tp, tabs_p, zero_p, mk_l, mv_l, p)
        ys, sts_l = _layer(ys, Bs, Ts, tabs_s, {nm: cache[nm][l] for nm in _STATE_KEYS},
                           cache['mem_k'][l], cache['mem_v'][l], p)
        outs.append((stp_l, mk_l, mv_l, sts_l))
    stp = {nm: jnp.stack([o[0][nm] for o in outs]) for nm in _STATE_KEYS}
    sts = {nm: jnp.stack([o[3][nm] for o in outs]) for nm in _STATE_KEYS}
    mk = jnp.stack([o[1] for o in outs])
    mv = jnp.stack([o[2] for o in outs])
    fg = final_norm_g[None, :]
    y_prompt = _rmsnorm(yp, fg, min(Bp * Tp, 1024)).reshape(Bp, Tp, D)
    y_sample = _rmsnorm(ys, fg, min(Bs * Ts, 1024)).reshape(Bs, Ts, D)
    return (y_prompt, y_sample,
            stp['rwkv'], stp['rwkv_shift'], stp['gla'], stp['ret'], stp['ml_c'], stp['ml_n'], stp['ml_m'],
            stp['ml_conv'], mk, mv,
            sts['rwkv'], sts['rwkv_shift'], sts['gla'], sts['ret'], sts['ml_c'], sts['ml_n'], sts['ml_m'],
            sts['ml_conv'])
```

```python
import functools
import math

import jax
import jax.numpy as jnp
from jax import lax
from jax.experimental import pallas as pl
from jax.experimental.pallas import tpu as pltpu

F32 = jnp.float32
BF16 = jnp.bfloat16

D_MODEL = 2048
BW = 1024
EPS = 1e-6
CHUNK = 64
N_BRANCH = 5
PAST_LEN = 2048
RWKV_H, RWKV_D, RWKV_LORA = 16, 64, 64
RWKV_SHIFT_W = 3 * BW + 2 * RWKV_LORA
RWKV_DECAY_SCALE = 0.606531
GLA_H, GLA_DK, GLA_DV, GLA_LORA = 4, 128, 256, 16
GLA_GATE_NORM = 16.0
RET_H, RET_DK, RET_DV = 4, 128, 256
ROPE_BASE = 10000.0
ML_H, ML_D, ML_CONV, ML_QK_BLOCK = 4, 256, 4, 4
XA_H, XA_D = 4, 256
LANE = 128
VMEM_LIMIT = 56 * 1024 * 1024

_IN_LAYOUT = (
    ('rwkv_shift', RWKV_SHIFT_W), ('rwkv_z', BW), ('gla_q', 512), ('gla_k', 512), ('gla_v', BW),
    ('gla_gd', GLA_LORA), ('gla_z', BW), ('ret_q', 512), ('ret_k', 512), ('ret_v', BW), ('ret_z', BW),
    ('ml_x', BW), ('ml_z', BW), ('xa_q', BW), ('xa_z', BW), ('gates', N_BRANCH * D_MODEL),
)
_PACK_ORDER = ('gates', 'rwkv_z', 'gla_z', 'ret_z', 'ml_z', 'xa_z', 'ret_q', 'ret_k', 'ret_v',
               'gla_q', 'gla_k', 'gla_v', 'ml_x', 'xa_q', 'rwkv_shift', 'gla_gd')
PROJ_TN = 1792


def _src_cols(name):
    start = 0
    for nm, size in _IN_LAYOUT:
        if nm == name:
            return start, start + size
        start += size
    raise KeyError(name)


def _pack_offsets():
    off, cur = {}, 0
    for nm in _PACK_ORDER:
        a, b = _src_cols(nm)
        off[nm] = cur
        cur += -(-(b - a) // LANE) * LANE
    total = -(-cur // PROJ_TN) * PROJ_TN
    return off, total


_OFF, NP = _pack_offsets()


def _pack_w_in(w_in):
    parts, cur = [], 0
    for nm in _PACK_ORDER:
        a, b = _src_cols(nm)
        parts.append(w_in[..., a:b].astype(BF16))
        width = -(-(b - a) // LANE) * LANE
        if width != b - a:
            parts.append(jnp.zeros(w_in.shape[:-1] + (width - (b - a),), BF16))
        cur += width
    if NP != cur:
        parts.append(jnp.zeros(w_in.shape[:-1] + (NP - cur,), BF16))
    return jnp.concatenate(parts, axis=-1)


def _cparams(*sem):
    return pltpu.CompilerParams(dimension_semantics=sem, vmem_limit_bytes=VMEM_LIMIT)


def _dot(a, b):
    return jnp.dot(a, b, preferred_element_type=F32)


def _dot_nt(a, b):
    return lax.dot_general(a, b, (((1,), (1,)), ((), ())), preferred_element_type=F32)


def _dot_tn(a, b):
    return lax.dot_general(a, b, (((0,), (0,)), ((), ())), preferred_element_type=F32)


def _bf(x):
    return x.astype(BF16)


def _split3(x):
    hi = x.astype(BF16)
    r1 = x - hi.astype(F32)
    mid = r1.astype(BF16)
    lo = (r1 - mid.astype(F32)).astype(BF16)
    return hi, mid, lo


def _cumsum_rows(tri, x):
    hi, mid, lo = _split3(x)
    return _dot(tri, hi) + _dot(tri, mid) + _dot(tri, lo)


def _cumsum_cols(x, triu):
    hi, mid, lo = _split3(x)
    return _dot(hi, triu) + _dot(mid, triu) + _dot(lo, triu)


def _iota2(shape, dim):
    return lax.broadcasted_iota(jnp.int32, shape, dim)


def _log_sigmoid(x):
    return jnp.minimum(x, 0.0) - jnp.log1p(jnp.exp(-jnp.abs(x)))


def _sigmoid(x):
    return 1.0 / (1.0 + jnp.exp(-x))


def _silu(x):
    return x * _sigmoid(x)


def _normproj_kernel(x_ref, g_ref, w_ref, o_ref, h_ref):
    @pl.when(pl.program_id(1) == 0)
    def _():
        x = x_ref[...]
        ms = jnp.mean(x * x, axis=-1, keepdims=True)
        h_ref[...] = ((x * lax.rsqrt(ms + EPS)) * g_ref[...]).astype(BF16)

    o_ref[...] = _dot(h_ref[...], w_ref[...])


def _normproj(x, g, w, tm, tn):
    M, K = x.shape
    N = w.shape[1]
    return pl.pallas_call(
        _normproj_kernel,
        grid=(M // tm, N // tn),
        in_specs=[pl.BlockSpec((tm, K), lambda i, j: (i, 0)),
                  pl.BlockSpec((1, K), lambda i, j: (0, 0)),
                  pl.BlockSpec((K, tn), lambda i, j: (0, j))],
        out_specs=pl.BlockSpec((tm, tn), lambda i, j: (i, j)),
        out_shape=jax.ShapeDtypeStruct((M, N), F32),
        scratch_shapes=[pltpu.VMEM((tm, K), BF16)],
        compiler_params=_cparams("parallel", "arbitrary"),
        name="normproj",
    )(x, g, w)


def _rwkv_kernel(xr_ref, xk_ref, xv_ref, xwa_ref, z_ref, sp_ref, mu_ref, w2_ref, a2_ref, w0_ref, a0_ref,
                 kkw_ref, ka_ref, rk_ref, lng_ref, s0_ref, og_ref, s_ref,
                 s_scr, phi_scr, psi_scr, qt_scr, egl_scr, carry_scr,
                 r_scr, lw_scr, k_scr, v_scr, kk_scr, b_scr, y_scr, *, L):
    t = pl.program_id(1)
    HG, D = RWKV_H, RWKV_D
    hs = range(HG)
    sls = [slice(h * D, (h + 1) * D) for h in hs]

    @pl.when(t == 0)
    def _():
        s_scr[...] = s0_ref[0]
        carry_scr[...] = sp_ref[0]

    def cat(parts):
        return jnp.concatenate(parts, axis=1)

    tt = xr_ref.shape[0]
    first = _iota2((tt, 1), 0) == 0

    def mix(x_ref, lo, hi):
        x = x_ref[...]
        prev = jnp.where(first, carry_scr[:, lo:hi], pltpu.roll(x, 1, 0))
        carry_scr[:, lo:hi] = x[tt - 1:tt, :]
        return x + (prev - x) * mu_ref[:, lo:hi]

    r_ = mix(xr_ref, 0, BW)
    k_ = mix(xk_ref, BW, 2 * BW)
    v_ = mix(xv_ref, 2 * BW, 3 * BW)
    wa = mix(xwa_ref, 3 * BW, RWKV_SHIFT_W)
    lw_scr[...] = -RWKV_DECAY_SCALE * _sigmoid(w0_ref[...] + _dot(_bf(jnp.tanh(wa)), w2_ref[...]))
    a_ = _sigmoid(a0_ref[...] + _dot(_bf(wa), a2_ref[...]))
    ones_blk = (_iota2((LANE, LANE), 0) // D == _iota2((LANE, LANE), 1) // D).astype(BF16)

    def headsum(x):
        hi = _bf(x)
        lo = _bf(x - hi.astype(F32))
        return cat([_dot(hi[:, j:j + LANE], ones_blk) + _dot(lo[:, j:j + LANE], ones_blk)
                    for j in range(0, BW, LANE)])

    kkf = k_ * kkw_ref[...]
    kk = kkf / jnp.maximum(jnp.sqrt(headsum(kkf * kkf)), 1e-12)
    r_scr[...] = r_
    k_scr[...] = k_ * (1.0 + (a_ - 1.0) * ka_ref[...])
    v_scr[...] = v_
    kk_scr[...] = kk
    b_scr[...] = kk * a_

    nchunk = tt // L
    row, col = _iota2((L, L), 0), _iota2((L, L), 1)
    tri = (row >= col).astype(BF16)
    strict = row > col
    incl2 = _iota2((L, 2 * L), 0) >= _iota2((L, 2 * L), 1) % L
    eye = (row == col).astype(F32)

    U1 = 4 if nchunk % 4 == 0 else 1
    it = range(U1 * HG)
    isl = [sls[i % HG] for i in it]

    def phase1(ci, carry):
        cidx = [ci * U1 + u for u in range(U1)]
        rws = [pl.ds(pl.multiple_of(c * L, L), L) for c in cidx]
        kkp, rg, kn, bn, kend, bend, v_c = [], [], [], [], [], [], []
        for u, c in enumerate(cidx):
            lw, k_c, b_c = lw_scr[rws[u], :], k_scr[rws[u], :], b_scr[rws[u], :]
            g = _cumsum_rows(tri, lw)
            gl = g[L - 1:L, :]
            kkp.append(kk_scr[rws[u], :] * jnp.exp(g - lw))
            rg.append(r_scr[rws[u], :] * jnp.exp(g))
            eng = jnp.exp(-g)
            kn.append(k_c * eng)
            bn.append(b_c * eng)
            ee = jnp.exp(gl - g)
            kend.append(k_c * ee)
            bend.append(b_c * ee)
            v_c.append(v_scr[rws[u], :])
            egl_scr[c] = jnp.exp(gl)
        kkp_h = [kkp[i // HG][:, isl[i]] for i in it]
        rg_h = [rg[i // HG][:, isl[i]] for i in it]
        bk = [_bf(jnp.concatenate([bn[i // HG][:, isl[i]], kn[i // HG][:, isl[i]]], axis=0)) for i in it]
        kr = [_bf(jnp.concatenate([kkp_h[i], rg_h[i]], axis=0)) for i in it]
        mnaq = [_dot_nt(kr[i], bk[i]) for i in it]
        mn = [x[:L, :] for x in mnaq]
        aq = [x[L:, :] for x in mnaq]
        N = [_bf(jnp.where(strict, m[:, L:], 0.0)) for m in mn]
        vb = [_bf(v_c[i // HG][:, isl[i]]) for i in it]
        nv = [_dot(N[i], vb[i]) for i in it]
        X = [jnp.where(strict, -m[:, :L], 0.0) for m in mn]
        Xb = [_bf(x) for x in X]
        P = [_dot(xb, xb) for xb in Xb]
        tinv = [eye + x for x in X]
        n = 2
        while n < L:
            Pb = [_bf(p) for p in P]
            if 2 * n >= L:
                tinv = [tinv[i] + _dot(_bf(tinv[i]), Pb[i]) for i in it]
            else:
                st = [_dot(_bf(jnp.concatenate([P[i], tinv[i]], axis=0)), Pb[i]) for i in it]
                P = [x[:L, :] for x in st]
                tinv = [tinv[i] + st[i][L:, :] for i in it]
            n *= 2
        gh = [_dot(_bf(tinv[i]), _bf(jnp.concatenate([kkp_h[i], nv[i]], axis=1))) for i in it]
        gm = [_bf(x[:, :D]) for x in gh]
        hm = [_bf(x[:, D:]) for x in gh]
        bend_b = [_bf(bend[i // HG][:, isl[i]]) for i in it]
        aqm = [_bf(jnp.where(incl2, a, 0.0)) for a in aq]
        hv = [jnp.concatenate([-hm[i], vb[i]], axis=0) for i in it]
        bke = [jnp.concatenate([bend_b[i], _bf(kend[i // HG][:, isl[i]])], axis=0) for i in it]
        phi = [_dot_tn(gm[i], bend_b[i]) for i in it]
        psi = [_dot_tn(hv[i], bke[i]) for i in it]
        qts = [rg_h[i] - _dot(aqm[i][:, :L], gm[i]) for i in it]
        y0s = [_dot(aqm[i], hv[i]) for i in it]
        for i in it:
            phi_scr[cidx[i // HG], i % HG] = _bf(phi[i])
            psi_scr[cidx[i // HG], i % HG] = psi[i]
        for u in range(U1):
            qt_scr[rws[u], :] = _bf(cat(qts[u * HG:(u + 1) * HG]))
            y_scr[rws[u], :] = cat(y0s[u * HG:(u + 1) * HG])
        return carry

    lax.fori_loop(0, nchunk // U1, phase1, 0)

    def phase2(c, carry):
        rows = pl.ds(pl.multiple_of(c * L, L), L)
        qt = qt_scr[rows, :]
        egl = egl_scr[c]
        S = [s_scr[h] for h in hs]
        Sb = [_bf(s_) for s_ in S]
        sphi = [_dot(Sb[h], phi_scr[c, h]) for h in hs]
        ys = [_dot_nt(qt[:, sls[h]], Sb[h]) for h in hs]
        for h in hs:
            s_scr[h] = S[h] * egl[:, sls[h]] - sphi[h] + psi_scr[c, h]
        y_scr[rows, :] += cat(ys)
        return carry

    lax.fori_loop(0, nchunk, phase2, 0)

    y = y_scr[...]
    rk = r_scr[...] * k_scr[...] * rk_ref[...]
    cen = y - headsum(y) * (1.0 / D)
    ln = cen * lax.rsqrt(headsum(cen * cen) * (1.0 / D) + EPS)
    out = ln * lng_ref[...] + headsum(rk) * v_scr[...]
    og_ref[...] = _bf(out * _silu(z_ref[...]))

    @pl.when(t == pl.num_programs(1) - 1)
    def _():
        s_ref[0] = s_scr[...]


def _rwkv(P, B, T, p, s0, shift_prev):
    L = min(T, CHUNK)
    tt = min(T, 256)
    nT = T // tt
    nc = tt // L
    off = _OFF['rwkv_shift']
    rowblk = lambda w, o: pl.BlockSpec((tt, w), lambda b_, t: (b_ * nT + t, o // w))
    const = lambda shape: pl.BlockSpec(shape, lambda b_, t: (0,) * len(shape))
    st = pl.BlockSpec((1, RWKV_H, RWKV_D, RWKV_D), lambda b_, t: (b_, 0, 0, 0))
    vec = const((1, BW))
    return pl.pallas_call(
        functools.partial(_rwkv_kernel, L=L),
        grid=(B, nT),
        in_specs=[rowblk(BW, off), rowblk(BW, off + BW), rowblk(BW, off + 2 * BW), rowblk(LANE, off + 3 * BW),
                  rowblk(BW, _OFF['rwkv_z']),
                  pl.BlockSpec((1, 1, RWKV_SHIFT_W), lambda b_, t: (b_, 0, 0)), const((1, RWKV_SHIFT_W)),
                  const((LANE, BW)), const((LANE, BW)), vec, vec, vec, vec, vec, vec, st],
        out_specs=[pl.BlockSpec((tt, BW), lambda b_, t: (b_ * nT + t, 0)), st],
        out_shape=[jax.ShapeDtypeStruct((B * T, BW), BF16), jax.ShapeDtypeStruct(s0.shape, F32)],
        scratch_shapes=[pltpu.VMEM((RWKV_H, RWKV_D, RWKV_D), F32),
                        pltpu.VMEM((nc, RWKV_H, RWKV_D, RWKV_D), BF16),
                        pltpu.VMEM((nc, RWKV_H, RWKV_D, RWKV_D), F32),
                        pltpu.VMEM((tt, BW), BF16),
                        pltpu.VMEM((nc, 1, BW), F32),
                        pltpu.VMEM((1, RWKV_SHIFT_W), F32)] + [pltpu.VMEM((tt, BW), F32)] * 7,
        compiler_params=_cparams("parallel", "arbitrary"),
        name="rwkv_scan",
    )(P, P, P, P, P, shift_prev, p['rwkv_mu'], p['rwkv_w2p'], p['rwkv_a2p'], p['rwkv_w0'], p['rwkv_a0'],
      p['rwkv_k_k'], p['rwkv_k_a'], p['rwkv_r_k'], p['rwkv_ln_g'], s0)


def _gla_kernel(q_ref, k_ref, v_ref, gd_ref, z_ref, g2_ref, gb_ref, ng_ref, s0_ref, o_ref, s_ref, s_scr, g_scr, *, L):
    t = pl.program_id(1)

    @pl.when(t == 0)
    def _():
        s_scr[...] = s0_ref[0]

    gk = _dot(_bf(gd_ref[...]), g2_ref[...]) + gb_ref[...]
    g_scr[...] = _log_sigmoid(gk) * (1.0 / GLA_GATE_NORM)

    nchunk = q_ref.shape[0] // L
    row, col = _iota2((L, L), 0), _iota2((L, L), 1)
    tri = (row >= col).astype(BF16)
    causal = row >= col
    scale = GLA_DK ** -0.5

    U = 2 if nchunk % 2 == 0 else 1
    hs, us = range(GLA_H), range(U)
    ksl = [slice(h * GLA_DK, (h + 1) * GLA_DK) for h in hs]
    vsl = [slice(h * GLA_DV, (h + 1) * GLA_DV) for h in hs]
    uh = [(u, h) for u in us for h in hs]

    def chunks(i, carry):
        rws = [pl.ds(pl.multiple_of((i * U + u) * L, L), L) for u in us]
        q = [q_ref[r, :] * scale for r in rws]
        k = [k_ref[r, :] for r in rws]
        b = [_cumsum_rows(tri, g_scr[r, :]) for r in rws]
        b_end = [x[L - 1:L, :] for x in b]
        qi = [_bf(q[u] * jnp.exp(b[u])) for u in us]
        ki = [_bf(k[u] * jnp.exp(-b[u])) for u in us]
        kend = [_bf(k[u] * jnp.exp(b_end[u] - b[u])) for u in us]
        eb_end = [jnp.exp(x) for x in b_end]
        vb = {(u, h): _bf(v_ref[rws[u], vsl[h]]) for u, h in uh}
        A = {(u, h): _bf(jnp.where(causal, _dot_nt(qi[u][:, ksl[h]], ki[u][:, ksl[h]]), 0.0)) for u, h in uh}
        kv = {(u, h): _dot_tn(vb[u, h], kend[u][:, ksl[h]]) for u, h in uh}
        av = {(u, h): _dot(A[u, h], vb[u, h]) for u, h in uh}
        St = [s_scr[h] for h in hs]
        o = {}
        for u in us:
            for h in hs:
                o[u, h] = av[u, h] + _dot_nt(qi[u][:, ksl[h]], _bf(St[h]))
            St = [St[h] * eb_end[u][:, ksl[h]] + kv[u, h] for h in hs]
        ms = {x: jnp.mean(o[x] * o[x], axis=-1, keepdims=True) for x in uh}
        for u, h in uh:
            o_ref[rws[u], vsl[h]] = _bf((o[u, h] * lax.rsqrt(ms[u, h] + EPS)) * ng_ref[:, vsl[h]]
                                        * _silu(z_ref[rws[u], vsl[h]]))
        for h in hs:
            s_scr[h] = St[h]
        return carry

    lax.fori_loop(0, nchunk // U, chunks, 0)

    @pl.when(t == pl.num_programs(1) - 1)
    def _():
        s_ref[0] = s_scr[...]


def _gla(P, B, T, g2p, gb, ng, s0t):
    L = min(T, CHUNK)
    tt = min(T, 512)
    nT = T // tt
    rowblk = lambda w, off: pl.BlockSpec((tt, w), lambda b_, t: (b_ * nT + t, off // w))
    const = lambda shape: pl.BlockSpec(shape, lambda b_, t: (0,) * len(shape))
    st = pl.BlockSpec((1, GLA_H, GLA_DV, GLA_DK), lambda b_, t: (b_, 0, 0, 0))
    return pl.pallas_call(
        functools.partial(_gla_kernel, L=L),
        grid=(B, nT),
        in_specs=[rowblk(512, _OFF['gla_q']), rowblk(512, _OFF['gla_k']), rowblk(BW, _OFF['gla_v']),
                  rowblk(LANE, _OFF['gla_gd']), rowblk(BW, _OFF['gla_z']), const((LANE, 512)), const((1, 512)),
                  const((1, BW)), st],
        out_specs=[pl.BlockSpec((tt, BW), lambda b_, t: (b_ * nT + t, 0)), st],
        out_shape=[jax.ShapeDtypeStruct((B * T, BW), BF16), jax.ShapeDtypeStruct(s0t.shape, F32)],
        scratch_shapes=[pltpu.VMEM((GLA_H, GLA_DV, GLA_DK), F32), pltpu.VMEM((tt, 512), F32)],
        compiler_params=_cparams("parallel", "arbitrary"),
        name="gla_scan",
    )(P, P, P, P, P, g2p, gb, ng, s0t)


def _ret_kernel(q_ref, k_ref, v_ref, z_ref, cos_ref, sin_ref, s0_ref, o_ref, s_ref, s_scr, *, L):
    t = pl.program_id(1)

    @pl.when(t == 0)
    def _():
        s_scr[...] = s0_ref[0]

    nchunk = q_ref.shape[0] // L
    rel = (_iota2((L, L), 0) - _iota2((L, L), 1)).astype(F32)
    tcol = _iota2((L, 1), 0).astype(F32)
    scale = RET_DK ** -0.5
    lgs = [math.log(1.0 - 2.0 ** (-5.0 - h)) for h in range(RET_H)]
    decay = [jnp.where(rel >= 0, jnp.exp(lg * jnp.maximum(rel, 0.0)), 0.0) for lg in lgs]
    rowd = [jnp.exp(lg * (tcol + 1.0)) for lg in lgs]
    cold = [jnp.exp(lg * (L - 1.0 - tcol)) for lg in lgs]

    def rope(x, cos2, sin2):
        return x * cos2 + pltpu.roll(x, RET_DK // 2, 1) * sin2

    U = 2 if nchunk % 2 == 0 else 1
    hs, us = range(RET_H), range(U)
    ksl = [slice(h * RET_DK, (h + 1) * RET_DK) for h in hs]
    vsl = [slice(h * RET_DV, (h + 1) * RET_DV) for h in hs]
    uh = [(u, h) for u in us for h in hs]

    def chunks(i, carry):
        rws = [pl.ds(pl.multiple_of((i * U + u) * L, L), L) for u in us]
        cs = [(cos_ref[r, :], sin_ref[r, :]) for r in rws]
        k = {(u, h): rope(k_ref[rws[u], ksl[h]], *cs[u]) for u, h in uh}
        qb = {(u, h): _bf(rope(q_ref[rws[u], ksl[h]], *cs[u]) * scale) for u, h in uh}
        kb = {x: _bf(k[x]) for x in uh}
        kcb = {(u, h): _bf(k[u, h] * cold[h]) for u, h in uh}
        vb = {(u, h): _bf(v_ref[rws[u], vsl[h]]) for u, h in uh}
        A = {(u, h): _bf(_dot_nt(qb[u, h], kb[u, h]) * decay[h]) for u, h in uh}
        kv = {x: _dot_tn(kcb[x], vb[x]) for x in uh}
        av = {x: _dot(A[x], vb[x]) for x in uh}
        S = [s_scr[h] for h in hs]
        o = {}
        for u in us:
            for h in hs:
                o[u, h] = av[u, h] + rowd[h] * _dot(qb[u, h], _bf(S[h]))
            S = [math.exp(lgs[h] * L) * S[h] + kv[u, h] for h in hs]
        ms = {x: jnp.mean(o[x] * o[x], axis=-1, keepdims=True) for x in uh}
        for u, h in uh:
            o_ref[rws[u], vsl[h]] = _bf(o[u, h] * lax.rsqrt(ms[u, h] + EPS) * _silu(z_ref[rws[u], vsl[h]]))
        for h in hs:
            s_scr[h] = S[h]
        return carry

    lax.fori_loop(0, nchunk // U, chunks, 0)

    @pl.when(t == pl.num_programs(1) - 1)
    def _():
        s_ref[0] = s_scr[...]


def _ret(P, B, T, cos2, sin2, s0):
    L = min(T, CHUNK)
    tt = min(T, 512)
    nT = T // tt
    rowblk = lambda w, off: pl.BlockSpec((tt, w), lambda b_, t: (b_ * nT + t, off // w))
    tab = pl.BlockSpec((tt, RET_DK), lambda b_, t: (t, 0))
    st = pl.BlockSpec((1, RET_H, RET_DK, RET_DV), lambda b_, t: (b_, 0, 0, 0))
    return pl.pallas_call(
        functools.partial(_ret_kernel, L=L),
        grid=(B, nT),
        in_specs=[rowblk(512, _OFF['ret_q']), rowblk(512, _OFF['ret_k']), rowblk(BW, _OFF['ret_v']),
                  rowblk(BW, _OFF['ret_z']), tab, tab, st],
        out_specs=[pl.BlockSpec((tt, BW), lambda b_, t: (b_ * nT + t, 0)), st],
        out_shape=[jax.ShapeDtypeStruct((B * T, BW), BF16), jax.ShapeDtypeStruct(s0.shape, F32)],
        scratch_shapes=[pltpu.VMEM((RET_H, RET_DK, RET_DV), F32)],
        compiler_params=_cparams("parallel", "arbitrary"),
        name="ret_scan",
    )(P, P, P, P, cos2, sin2, s0)


def _mlstm_pre_kernel(xm_ref, cp_ref, cw_ref, cb_ref, wq_ref, wk_ref, wv_ref, wif_ref, bif_ref,
                      xc_ref, q_ref, k_ref, v_ref, g_ref, carry_scr):
    t = pl.program_id(1)
    tm = xm_ref.shape[0]
    nprev = ML_CONV - 1

    @pl.when(t == 0)
    def _():
        carry_scr[...] = jnp.zeros_like(carry_scr)
        carry_scr[8 - nprev:8, :] = cp_ref[0]

    xm = xm_ref[...]
    c8 = carry_scr[...]
    rid = _iota2((8, 1), 0)
    conv = xm * cw_ref[nprev:nprev + 1, :] + cb_ref[...]
    for j in range(1, ML_CONV):
        rolled = pltpu.roll(xm, j, 0)
        head = jnp.where(rid < j, pltpu.roll(c8, j, 0), rolled[0:8, :])
        prev = jnp.concatenate([head, rolled[8:, :]], axis=0) if tm > 8 else head
        conv = conv + prev * cw_ref[nprev - j:nprev - j + 1, :]
    carry_scr[...] = xm[tm - 8:tm, :]
    xc = _silu(conv)
    xc_ref[...] = xc
    xcb, xmb = _bf(xc), _bf(xm)
    q = _dot(xcb, wq_ref[...])
    k = _dot(xcb, wk_ref[...])
    v = _dot(xmb, wv_ref[...])
    g_ref[...] = (_dot(_bf(q), wif_ref[0]) + _dot(_bf(k), wif_ref[1]) + _dot(_bf(v), wif_ref[2])) + bif_ref[...]
    q_ref[...] = q
    k_ref[...] = k * (ML_D ** -0.5)
    v_ref[...] = v


def _mlstm_pre(P, B, T, conv_prev, p):
    tm = min(T, 512)
    nT = T // tm
    row = pl.BlockSpec((tm, BW), lambda b_, t: (b_ * nT + t, 0))
    const = lambda shape: pl.BlockSpec(shape, lambda b_, t: (0,) * len(shape))
    M = B * T
    return pl.pallas_call(
        _mlstm_pre_kernel,
        grid=(B, nT),
        in_specs=[pl.BlockSpec((tm, BW), lambda b_, t: (b_ * nT + t, _OFF['ml_x'] // BW)),
                  pl.BlockSpec((1, ML_CONV - 1, BW), lambda b_, t: (b_, 0, 0)),
                  const((ML_CONV, BW)), const((1, BW)), const((BW, BW)), const((BW, BW)), const((BW, BW)),
                  const((3, BW, LANE)), const((1, LANE))],
        out_specs=[row, row, row, row, pl.BlockSpec((tm, LANE), lambda b_, t: (b_ * nT + t, 0))],
        out_shape=[jax.ShapeDtypeStruct((M, BW), F32)] * 4 + [jax.ShapeDtypeStruct((M, LANE), F32)],
        scratch_shapes=[pltpu.VMEM((8, BW), F32)],
        compiler_params=_cparams("parallel", "arbitrary"),
        name="mlstm_pre",
    )(P, conv_prev, p['ml_conv_w'], p['ml_conv_b'], p['ml_wq'], p['ml_wk'], p['ml_wv'], p['ml_wif'], p['ml_bif'])


def _mlstm_kernel(q_ref, k_ref, v_ref, gc_ref, gr_ref, xc_ref, z_ref, ng_ref, sk_ref, c0_ref, n0_ref, m0_ref,
                  o_ref, c_ref, n_ref, m_ref, c_scr, n_scr, m_scr, *, L):
    t = pl.program_id(1)

    @pl.when(t == 0)
    def _():
        c_scr[...] = c0_ref[0]
        n_scr[...] = n0_ref[0]
        m_scr[...] = m0_ref[0]

    nchunk = q_ref.shape[0] // L
    row, col = _iota2((L, L), 0), _iota2((L, L), 1)
    tri = (row >= col).astype(BF16)
    triu = (row <= col).astype(BF16)
    causal = row >= col

    U = 2 if nchunk % 2 == 0 else 1
    hs, us = range(ML_H), range(U)
    sl = [slice(h * ML_D, (h + 1) * ML_D) for h in hs]
    uh = [(u, h) for u in us for h in hs]

    def chunks(i, carry):
        cidx = [i * U + u for u in us]
        rws = [pl.ds(pl.multiple_of(c * L, L), L) for c in cidx]
        gcb = [gc_ref[0, c] for c in cidx]
        grb = [gr_ref[0, c] for c in cidx]
        bc = [_cumsum_rows(tri, x) for x in gcb]
        br = [_cumsum_cols(x, triu) for x in grb]
        q = {(u, h): q_ref[rws[u], sl[h]] for u, h in uh}
        k = {(u, h): k_ref[rws[u], sl[h]] for u, h in uh}
        qb = {x: _bf(q[x]) for x in uh}
        vb = {(u, h): _bf(v_ref[rws[u], sl[h]]) for u, h in uh}
        qk = {x: _dot_nt(qb[x], _bf(k[x])) for x in uh}
        b_c = {(u, h): bc[u][:, ML_H + h:ML_H + h + 1] for u, h in uh}
        dlog = {(u, h): jnp.where(causal, b_c[u, h] - br[u][ML_H + h:ML_H + h + 1, :] + grb[u][h:h + 1, :], -jnp.inf)
                for u, h in uh}
        m_loc = {x: jnp.max(dlog[x], axis=-1, keepdims=True) for x in uh}
        s_loc = {x: qk[x] * jnp.exp(dlog[x] - m_loc[x]) for x in uh}
        rs = {x: jnp.sum(s_loc[x], axis=-1, keepdims=True) for x in uh}
        b_end = {x: b_c[x][L - 1:L, :] for x in uh}
        m_le = {x: m_loc[x][L - 1:L, :] for x in uh}
        kw = {(u, h): k[u, h] * jnp.exp(b_end[u, h] - b_c[u, h] + gcb[u][:, h:h + 1] - m_le[u, h]) for u, h in uh}
        sv = {x: _dot(_bf(s_loc[x]), vb[x]) for x in uh}
        kv = {x: _dot_tn(_bf(kw[x]), vb[x]) for x in uh}
        ksum = {x: jnp.sum(kw[x], axis=0, keepdims=True) for x in uh}
        C = [c_scr[h] for h in hs]
        nvec = [n_scr[h] for h in hs]
        m_prev = [m_scr[h][:, 0:1] for h in hs]
        hh = {}
        for u in us:
            qc = [_dot(qb[u, h], _bf(C[h])) for h in hs]
            qn = [jnp.sum(q[u, h] * nvec[h], axis=-1, keepdims=True) for h in hs]
            from_state = [b_c[u, h] + m_prev[h] for h in hs]
            m_t = [jnp.maximum(from_state[h], m_loc[u, h]) for h in hs]
            a_in = [jnp.exp(m_loc[u, h] - m_t[h]) for h in hs]
            w_state = [jnp.exp(from_state[h] - m_t[h]) for h in hs]
            den = [a_in[h] * rs[u, h] + w_state[h] * qn[h] for h in hs]
            for h in hs:
                hh[u, h] = ((a_in[h] * sv[u, h] + w_state[h] * qc[h])
                            / jnp.maximum(jnp.abs(den[h]), jnp.exp(-m_t[h])))
            m_new = [m_t[h][L - 1:L, :] for h in hs]
            cd = [jnp.exp(b_end[u, h] + m_prev[h] - m_new[h]) for h in hs]
            sc = [jnp.exp(m_le[u, h] - m_new[h]) for h in hs]
            C = [cd[h] * C[h] + sc[h] * kv[u, h] for h in hs]
            nvec = [cd[h] * nvec[h] + sc[h] * ksum[u, h] for h in hs]
            m_prev = m_new
        mean = {x: jnp.mean(hh[x], axis=-1, keepdims=True) for x in uh}
        cen = {x: hh[x] - mean[x] for x in uh}
        var = {x: jnp.mean(cen[x] * cen[x], axis=-1, keepdims=True) for x in uh}
        for u, h in uh:
            y = cen[u, h] * lax.rsqrt(var[u, h] + EPS)
            o_ref[rws[u], sl[h]] = _bf((y * ng_ref[:, sl[h]] + sk_ref[:, sl[h]] * xc_ref[rws[u], sl[h]])
                                       * _silu(z_ref[rws[u], sl[h]]))
        for h in hs:
            c_scr[h] = C[h]
            n_scr[h] = nvec[h]
            m_scr[h] = jnp.broadcast_to(m_prev[h], (1, LANE))
        return carry

    lax.fori_loop(0, nchunk // U, chunks, 0)

    @pl.when(t == pl.num_programs(1) - 1)
    def _():
        c_ref[0] = c_scr[...]
        n_ref[0] = n_scr[...]
        m_ref[0] = m_scr[...]


def _mlstm(q, k, v, gc, gr, xc, P, ng, sk, c0, n0, m0, B, T):
    L = min(T, CHUNK)
    tt = min(T, 512)
    nT = T // tt
    nc = tt // L
    row = pl.BlockSpec((tt, BW), lambda b_, t: (b_ * nT + t, 0))
    const = pl.BlockSpec((1, BW), lambda b_, t: (0, 0))
    stc = pl.BlockSpec((1, ML_H, ML_D, ML_D), lambda b_, t: (b_, 0, 0, 0))
    stn = pl.BlockSpec((1, ML_H, 1, ML_D), lambda b_, t: (b_, 0, 0, 0))
    stm = pl.BlockSpec((1, ML_H, 1, LANE), lambda b_, t: (b_, 0, 0, 0))
    return pl.pallas_call(
        functools.partial(_mlstm_kernel, L=L),
        grid=(B, nT),
        in_specs=[row, row, row,
                  pl.BlockSpec((1, nc, L, 8), lambda b_, t: (b_, t, 0, 0)),
                  pl.BlockSpec((1, nc, 8, L), lambda b_, t: (b_, t, 0, 0)),
                  row, pl.BlockSpec((tt, BW), lambda b_, t: (b_ * nT + t, _OFF['ml_z'] // BW)), const, const, stc, stn, stm],
        out_specs=[row, stc, stn, stm],
        out_shape=[jax.ShapeDtypeStruct((B * T, BW), BF16), jax.ShapeDtypeStruct(c0.shape, F32),
                   jax.ShapeDtypeStruct(n0.shape, F32), jax.ShapeDtypeStruct(m0.shape, F32)],
        scratch_shapes=[pltpu.VMEM((ML_H, ML_D, ML_D), F32), pltpu.VMEM((ML_H, 1, ML_D), F32),
                        pltpu.VMEM((ML_H, 1, LANE), F32)],
        compiler_params=_cparams("parallel", "arbitrary"),
        name="mlstm_scan",
    )(q, k, v, gc, gr, xc, P, ng, sk, c0, n0, m0)


def _xattn_kernel(q_ref, z_ref, mk_ref, mv_ref, o_ref):
    hs = range(XA_H)
    sl = [slice(h * XA_D, (h + 1) * XA_D) for h in hs]
    s = [_dot_nt(_bf(q_ref[:, x]), _bf(mk_ref[0, :, x])) * (XA_D ** -0.5) for x in sl]
    e = [jnp.exp(x - jnp.max(x, axis=-1, keepdims=True)) for x in s]
    prob = [_bf(x / jnp.sum(x, axis=-1, keepdims=True)) for x in e]
    o = [_dot(prob[h], _bf(mv_ref[0, :, sl[h]])) for h in hs]
    for h in hs:
        o_ref[:, sl[h]] = _bf(o[h] * _silu(z_ref[:, sl[h]]))


def _xattn(P, B, T, mk, mv):
    tt = min(T, 512)
    nT = T // tt
    n_mem = mk.shape[1]
    mem = pl.BlockSpec((1, n_mem, BW), lambda b_, t: (b_, 0, 0))
    return pl.pallas_call(
        _xattn_kernel,
        grid=(B, nT),
        in_specs=[pl.BlockSpec((tt, BW), lambda b_, t: (b_ * nT + t, _OFF['xa_q'] // BW)),
                  pl.BlockSpec((tt, BW), lambda b_, t: (b_ * nT + t, _OFF['xa_z'] // BW)), mem, mem],
        out_specs=pl.BlockSpec((tt, BW), lambda b_, t: (b_ * nT + t, 0)),
        out_shape=jax.ShapeDtypeStruct((B * T, BW), BF16),
        compiler_params=_cparams("parallel", "parallel"),
        name="xattn",
    )(P, P, mk, mv)


def _merge_kernel(o0, o1, o2, o3, o4, g_ref, w_ref, out_ref):
    ys = [_dot(o_ref[...], w_ref[i]) for i, o_ref in enumerate((o0, o1, o2, o3, o4))]
    acc = None
    for i, y in enumerate(ys):
        term = _sigmoid(g_ref[:, i * D_MODEL:(i + 1) * D_MODEL]) * y
        acc = term if acc is None else acc + term
    out_ref[...] = _bf(acc)


def _merge(ogs, P, wb, tm):
    M = ogs[0].shape[0]
    gw = N_BRANCH * D_MODEL
    og = pl.BlockSpec((tm, BW), lambda i: (i, 0))
    return pl.pallas_call(
        _merge_kernel,
        grid=(M // tm,),
        in_specs=[og] * N_BRANCH + [pl.BlockSpec((tm, gw), lambda i: (i, _OFF['gates'] // gw)),
                                    pl.BlockSpec((N_BRANCH, BW, D_MODEL), lambda i: (0, 0, 0),
                                                 pipeline_mode=pl.Buffered(1))],
        out_specs=pl.BlockSpec((tm, D_MODEL), lambda i: (i, 0)),
        out_shape=jax.ShapeDtypeStruct((M, D_MODEL), BF16),
        compiler_params=_cparams("parallel"),
        name="merge",
    )(*ogs, P, wb)


def _resid_kernel(x_ref, m_ref, w_ref, o_ref):
    o_ref[...] = x_ref[...] + _dot(m_ref[...], w_ref[...])


def _resid_out(x, merged, w, tm):
    M = x.shape[0]
    row = pl.BlockSpec((tm, D_MODEL), lambda i: (i, 0))
    return pl.pallas_call(
        _resid_kernel,
        grid=(M // tm,),
        in_specs=[row, row, pl.BlockSpec((D_MODEL, D_MODEL), lambda i: (0, 0), pipeline_mode=pl.Buffered(1))],
        out_specs=row,
        out_shape=jax.ShapeDtypeStruct((M, D_MODEL), F32),
        compiler_params=_cparams("parallel"),
        name="resid_out",
    )(x, merged, w)


def _rmsnorm_kernel(x_ref, g_ref, o_ref):
    x = x_ref[...]
    ms = jnp.mean(x * x, axis=-1, keepdims=True)
    o_ref[...] = (x * lax.rsqrt(ms + EPS)) * g_ref[...]


def _rmsnorm(x, g, tm):
    M, K = x.shape
    return pl.pallas_call(
        _rmsnorm_kernel,
        grid=(M // tm,),
        in_specs=[pl.BlockSpec((tm, K), lambda i: (i, 0)), pl.BlockSpec((1, K), lambda i: (0, 0))],
        out_specs=pl.BlockSpec((tm, K), lambda i: (i, 0)),
        out_shape=jax.ShapeDtypeStruct((M, K), F32),
        compiler_params=_cparams("parallel"),
        name="rmsnorm",
    )(x, g)


def _blockdiag(w):
    depth, n, c, d = w.shape
    rows = w.reshape(depth, n * c, d)
    tile = (jnp.arange(d)[:, None] == jnp.arange(n * d)[None, :] % d).astype(w.dtype)
    dense = jnp.einsum('lrd,dj->lrj', rows, tile)
    mask = (jnp.arange(n * c)[:, None] // c) == (jnp.arange(n * d)[None, :] // d)
    return jnp.where(mask, dense, 0.0).astype(BF16)


def _rope_tables(start, T):
    half = RET_DK // 2
    inv = ROPE_BASE ** (-jnp.linspace(0.0, 1.0, half, dtype=F32))
    blk = 128
    if T % blk or T <= blk:
        ang = (start + jnp.arange(T, dtype=F32))[:, None] * inv[None, :]
        cos, sin = jnp.cos(ang), jnp.sin(ang)
    else:
        hi = (start + blk * jnp.arange(T // blk, dtype=F32))[:, None] * inv[None, :]
        lo = jnp.arange(blk, dtype=F32)[:, None] * inv[None, :]
        ch, sh, cl, sl_ = jnp.cos(hi)[:, None, :], jnp.sin(hi)[:, None, :], jnp.cos(lo)[None], jnp.sin(lo)[None]
        cos = (ch * cl - sh * sl_).reshape(T, half)
        sin = (sh * cl + ch * sl_).reshape(T, half)
    return jnp.concatenate([cos, cos], axis=-1), jnp.concatenate([-sin, sin], axis=-1)


def _layer(x, B, T, tabs, st, mem_k, mem_v, p):
    M = B * T
    tm = min(M, 1024)
    P = _normproj(x, p['norm_g'], p['w_pack'], tm, PROJ_TN)
    P3 = P.reshape(B, T, NP)

    og_a, s_rwkv = _rwkv(P, B, T, p, st['rwkv'], st['rwkv_shift'])
    shift_new = P3[:, T - 1:, _OFF['rwkv_shift']:_OFF['rwkv_shift'] + RWKV_SHIFT_W]

    og_b, s_gla_t = _gla(P, B, T, p['gla_g2p'], p['gla_gb'], p['gla_norm_g'], jnp.swapaxes(st['gla'], -1, -2))
    s_gla = jnp.swapaxes(s_gla_t, -1, -2)

    og_c, s_ret = _ret(P, B, T, tabs[0], tabs[1], st['ret'])

    xc, q_m, k_m, v_m, gates = _mlstm_pre(P, B, T, st['ml_conv'], p)
    L = min(T, CHUNK)
    gcol = jnp.concatenate([gates[:, :ML_H], _log_sigmoid(gates[:, ML_H:2 * ML_H])], axis=-1).reshape(B, T // L, L, 2 * ML_H)
    grow = jnp.swapaxes(gcol, -1, -2)
    og_d, c_new, n_new, m_new = _mlstm(
        q_m, k_m, v_m, gcol, grow, xc, P, p['ml_norm_g'], p['ml_skip'], st['ml_c'], st['ml_n'][:, :, None, :],
        jnp.broadcast_to(st['ml_m'][:, :, None, None], st['ml_m'].shape + (1, LANE)), B, T)
    xm_tail = P3[:, max(T - (ML_CONV - 1), 0):, _OFF['ml_x']:_OFF['ml_x'] + BW]
    conv_new = jnp.concatenate([st['ml_conv'], xm_tail], axis=1)[:, -(ML_CONV - 1):]

    og_x = _xattn(P, B, T, mem_k.reshape(B, -1, BW), mem_v.reshape(B, -1, BW))

    merged = _merge([og_a, og_b, og_c, og_d, og_x], P, p['w_branch'], min(M, 256))
    x_new = _resid_out(x, merged, p['w_out'], min(M, 512))
    new = {'rwkv': s_rwkv, 'rwkv_shift': shift_new, 'gla': s_gla, 'ret': s_ret, 'ml_c': c_new,
           'ml_n': n_new[:, :, 0, :], 'ml_m': m_new[:, :, 0, 0], 'ml_conv': conv_new}
    return x_new, new


_STATE_KEYS = ('rwkv', 'rwkv_shift', 'gla', 'ret', 'ml_c', 'ml_n', 'ml_m', 'ml_conv')


def kernel(x_prompt, x_sample, mem_prompt, cache_mem_k, cache_mem_v, state_rwkv, state_rwkv_shift, state_gla, state_ret, state_mlstm_c, state_mlstm_n, state_mlstm_m, state_mlstm_conv, norm_g, mem_norm_g, w_in, w_mem_kv, rwkv_mu, rwkv_w0, rwkv_w2, rwkv_a0, rwkv_a2, rwkv_k_k, rwkv_k_a, rwkv_r_k, rwkv_ln_g, gla_g2, gla_gb, gla_norm_g, ml_conv_w, ml_conv_b, ml_wq, ml_wk, ml_wv, ml_w_if, ml_b_if, ml_skip, ml_norm_g, w_branch, w_out, final_norm_g):
    Bp, Tp, D = x_prompt.shape
    Bs, Ts, _ = x_sample.shape
    depth = w_in.shape[0]
    n_mem = mem_prompt.shape[1]

    params = {
        'norm_g': norm_g[:, None, :], 'mem_norm_g': mem_norm_g[:, None, :],
        'w_mem_kv': _bf(w_mem_kv),
        'rwkv_mu': rwkv_mu[:, None, :], 'rwkv_w0': rwkv_w0[:, None, :], 'rwkv_a0': rwkv_a0[:, None, :],
        'rwkv_w2p': _bf(jnp.pad(rwkv_w2, ((0, 0), (0, LANE - RWKV_LORA), (0, 0)))),
        'rwkv_a2p': _bf(jnp.pad(rwkv_a2, ((0, 0), (LANE - RWKV_LORA, 0), (0, 0)))),
        'rwkv_k_k': rwkv_k_k[:, None, :], 'rwkv_k_a': rwkv_k_a[:, None, :], 'rwkv_r_k': rwkv_r_k[:, None, :],
        'rwkv_ln_g': rwkv_ln_g[:, None, :],
        'gla_g2p': _bf(jnp.pad(gla_g2, ((0, 0), (0, LANE - GLA_LORA), (0, 0)))), 'gla_gb': gla_gb[:, None, :],
        'gla_norm_g': gla_norm_g[:, None, :],
        'ml_conv_w': ml_conv_w, 'ml_conv_b': ml_conv_b[:, None, :],
        'ml_wq': _blockdiag(ml_wq), 'ml_wk': _blockdiag(ml_wk), 'ml_wv': _blockdiag(ml_wv),
        'ml_wif': _bf(jnp.pad(ml_w_if.reshape(depth, 3, BW, 2 * ML_H), ((0, 0), (0, 0), (0, 0), (0, LANE - 2 * ML_H)))),
        'ml_bif': jnp.pad(ml_b_if, ((0, 0), (0, LANE - 2 * ML_H)))[:, None, :],
        'ml_skip': ml_skip[:, None, :], 'ml_norm_g': ml_norm_g[:, None, :],
        'w_branch': _bf(w_branch), 'w_out': _bf(w_out),
    }
    cache = {'rwkv': state_rwkv, 'rwkv_shift': state_rwkv_shift, 'gla': state_gla, 'ret': state_ret,
             'ml_c': state_mlstm_c, 'ml_n': state_mlstm_n, 'ml_m': state_mlstm_m, 'ml_conv': state_mlstm_conv,
             'mem_k': cache_mem_k, 'mem_v': cache_mem_v}
    tabs_p = _rope_tables(0.0, Tp)
    tabs_s = _rope_tables(float(PAST_LEN), Ts)
    zero_p = {
        'rwkv': jnp.zeros((Bp, RWKV_H, RWKV_D, RWKV_D), F32), 'rwkv_shift': jnp.zeros((Bp, 1, RWKV_SHIFT_W), F32),
        'gla': jnp.zeros((Bp, GLA_H, GLA_DK, GLA_DV), F32), 'ret': jnp.zeros((Bp, RET_H, RET_DK, RET_DV), F32),
        'ml_c': jnp.zeros((Bp, ML_H, ML_D, ML_D), F32), 'ml_n': jnp.zeros((Bp, ML_H, ML_D), F32),
        'ml_m': jnp.zeros((Bp, ML_H), F32), 'ml_conv': jnp.zeros((Bp, ML_CONV - 1, BW), F32),
    }
    mem2d = mem_prompt.reshape(Bp * n_mem, D)

    yp, ys = x_prompt.reshape(Bp * Tp, D), x_sample.reshape(Bs * Ts, D)
    outs = []
    for l in range(depth):
        p = {nm: arr[l] for nm, arr in params.items()}
        p['w_pack'] = _pack_w_in(w_in[l])
        kv = _normproj(mem2d, p['mem_norm_g'], p['w_mem_kv'], min(Bp * n_mem, 512), 512)
        mk_l = kv[:, :BW].reshape(Bp, n_mem, XA_H, XA_D)
        mv_l = kv[:, BW:].reshape(Bp, n_mem, XA_H, XA_D)
        yp, stp_l = _layer(yp, Bp, Tp, tabs_p, zero_p, mk_l, mv_l, p)
        ys, sts_l = _layer(ys, Bs, Ts, tabs_s, {nm: cache[nm][l] for nm in _STATE_KEYS},
                           cache['mem_k'][l], cache['mem_v'][l], p)
        outs.append((stp_l, mk_l, mv_l, sts_l))
    stp = {nm: jnp.stack([o[0][nm] for o in outs]) for nm in _STATE_KEYS}
    sts = {nm: jnp.stack([o[3][nm] for o in outs]) for nm in _STATE_KEYS}
    mk = jnp.stack([o[1] for o in outs])
    mv = jnp.stack([o[2] for o in outs])
    fg = final_norm_g[None, :]
    y_prompt = _rmsnorm(yp, fg, min(Bp * Tp, 1024)).reshape(Bp, Tp, D)
    y_sample = _rmsnorm(ys, fg, min(Bs * Ts, 1024)).reshape(Bs, Ts, D)
    return (y_prompt, y_sample,
            stp['rwkv'], stp['rwkv_shift'], stp['gla'], stp['ret'], stp['ml_c'], stp['ml_n'], stp['ml_m'],
            stp['ml_conv'], mk, mv,
            sts['rwkv'], sts['rwkv_shift'], sts['gla'], sts['ret'], sts['ml_c'], sts['ml_n'], sts['ml_m'],
            sts['ml_conv'])
```

```python
import functools
import math

import jax
import jax.numpy as jnp
from jax import lax
from jax.experimental import pallas as pl
from jax.experimental.pallas import tpu as pltpu

F32 = jnp.float32
BF16 = jnp.bfloat16

D_MODEL = 2048
BW = 1024
EPS = 1e-6
CHUNK = 64
N_BRANCH = 5
PAST_LEN = 2048
RWKV_H, RWKV_D, RWKV_LORA = 16, 64, 64
RWKV_SHIFT_W = 3 * BW + 2 * RWKV_LORA
RWKV_DECAY_SCALE = 0.606531
GLA_H, GLA_DK, GLA_DV, GLA_LORA = 4, 128, 256, 16
GLA_GATE_NORM = 16.0
RET_H, RET_DK, RET_DV = 4, 128, 256
ROPE_BASE = 10000.0
ML_H, ML_D, ML_CONV, ML_QK_BLOCK = 4, 256, 4, 4
XA_H, XA_D = 4, 256
LANE = 128
VMEM_LIMIT = 56 * 1024 * 1024

_IN_LAYOUT = (
    ('rwkv_shift', RWKV_SHIFT_W), ('rwkv_z', BW), ('gla_q', 512), ('gla_k', 512), ('gla_v', BW),
    ('gla_gd', GLA_LORA), ('gla_z', BW), ('ret_q', 512), ('ret_k', 512), ('ret_v', BW), ('ret_z', BW),
    ('ml_x', BW), ('ml_z', BW), ('xa_q', BW), ('xa_z', BW), ('gates', N_BRANCH * D_MODEL),
)
_PACK_ORDER = ('gates', 'rwkv_z', 'gla_z', 'ret_z', 'ml_z', 'xa_z', 'ret_q', 'ret_k', 'ret_v',
               'gla_q', 'gla_k', 'gla_v', 'ml_x', 'xa_q', 'rwkv_shift', 'gla_gd')
PROJ_TN = 1792


def _src_cols(name):
    start = 0
    for nm, size in _IN_LAYOUT:
        if nm == name:
            return start, start + size
        start += size
    raise KeyError(name)


def _pack_offsets():
    off, cur = {}, 0
    for nm in _PACK_ORDER:
        a, b = _src_cols(nm)
        off[nm] = cur
        cur += -(-(b - a) // LANE) * LANE
    total = -(-cur // PROJ_TN) * PROJ_TN
    return off, total


_OFF, NP = _pack_offsets()


def _pack_w_in(w_in):
    parts, cur = [], 0
    for nm in _PACK_ORDER:
        a, b = _src_cols(nm)
        parts.append(w_in[..., a:b].astype(BF16))
        width = -(-(b - a) // LANE) * LANE
        if width != b - a:
            parts.append(jnp.zeros(w_in.shape[:-1] + (width - (b - a),), BF16))
        cur += width
    if NP != cur:
        parts.append(jnp.zeros(w_in.shape[:-1] + (NP - cur,), BF16))
    return jnp.concatenate(parts, axis=-1)


def _cparams(*sem):
    return pltpu.CompilerParams(dimension_semantics=sem, vmem_limit_bytes=VMEM_LIMIT)


def _dot(a, b):
    return jnp.dot(a, b, preferred_element_type=F32)


def _dot_nt(a, b):
    return lax.dot_general(a, b, (((1,), (1,)), ((), ())), preferred_element_type=F32)


def _dot_tn(a, b):
    return lax.dot_general(a, b, (((0,), (0,)), ((), ())), preferred_element_type=F32)


def _bf(x):
    return x.astype(BF16)


def _split3(x):
    hi = x.astype(BF16)
    r1 = x - hi.astype(F32)
    mid = r1.astype(BF16)
    lo = (r1 - mid.astype(F32)).astype(BF16)
    return hi, mid, lo


def _cumsum_rows(tri, x):
    hi, mid, lo = _split3(x)
    return _dot(tri, hi) + _dot(tri, mid) + _dot(tri, lo)


def _cumsum_cols(x, triu):
    hi, mid, lo = _split3(x)
    return _dot(hi, triu) + _dot(mid, triu) + _dot(lo, triu)


def _iota2(shape, dim):
    return lax.broadcasted_iota(jnp.int32, shape, dim)


def _log_sigmoid(x):
    return jnp.minimum(x, 0.0) - jnp.log1p(jnp.exp(-jnp.abs(x)))


def _sigmoid(x):
    return 1.0 / (1.0 + jnp.exp(-x))


def _silu(x):
    return x * _sigmoid(x)


def _normproj_kernel(x_ref, g_ref, w_ref, o_ref, h_ref):
    @pl.when(pl.program_id(1) == 0)
    def _():
        x = x_ref[...]
        ms = jnp.mean(x * x, axis=-1, keepdims=True)
        h_ref[...] = ((x * lax.rsqrt(ms + EPS)) * g_ref[...]).astype(BF16)

    o_ref[...] = _dot(h_ref[...], w_ref[...])


def _normproj(x, g, w, tm, tn):
    M, K = x.shape
    N = w.shape[1]
    return pl.pallas_call(
        _normproj_kernel,
        grid=(M // tm, N // tn),
        in_specs=[pl.BlockSpec((tm, K), lambda i, j: (i, 0)),
                  pl.BlockSpec((1, K), lambda i, j: (0, 0)),
                  pl.BlockSpec((K, tn), lambda i, j: (0, j))],
        out_specs=pl.BlockSpec((tm, tn), lambda i, j: (i, j)),
        out_shape=jax.ShapeDtypeStruct((M, N), F32),
        scratch_shapes=[pltpu.VMEM((tm, K), BF16)],
        compiler_params=_cparams("parallel", "arbitrary"),
        name="normproj",
    )(x, g, w)


def _rwkv_kernel(xr_ref, xk_ref, xv_ref, xwa_ref, z_ref, sp_ref, mu_ref, w2_ref, a2_ref, w0_ref, a0_ref,
                 kkw_ref, ka_ref, rk_ref, lng_ref, s0_ref, og_ref, s_ref,
                 s_scr, phi_scr, psi_scr, qt_scr, egl_scr, carry_scr,
                 r_scr, lw_scr, k_scr, v_scr, kk_scr, b_scr, y_scr, *, L):
    t = pl.program_id(1)
    HG, D = RWKV_H, RWKV_D
    hs = range(HG)
    sls = [slice(h * D, (h + 1) * D) for h in hs]

    @pl.when(t == 0)
    def _():
        s_scr[...] = s0_ref[0]
        carry_scr[...] = sp_ref[0]

    def cat(parts):
        return jnp.concatenate(parts, axis=1)

    tt = xr_ref.shape[0]
    first = _iota2((tt, 1), 0) == 0

    def mix(x_ref, lo, hi):
        x = x_ref[...]
        prev = jnp.where(first, carry_scr[:, lo:hi], pltpu.roll(x, 1, 0))
        carry_scr[:, lo:hi] = x[tt - 1:tt, :]
        return x + (prev - x) * mu_ref[:, lo:hi]

    r_ = mix(xr_ref, 0, BW)
    k_ = mix(xk_ref, BW, 2 * BW)
    v_ = mix(xv_ref, 2 * BW, 3 * BW)
    wa = mix(xwa_ref, 3 * BW, RWKV_SHIFT_W)
    lw_scr[...] = -RWKV_DECAY_SCALE * _sigmoid(w0_ref[...] + _dot(_bf(jnp.tanh(wa)), w2_ref[...]))
    a_ = _sigmoid(a0_ref[...] + _dot(_bf(wa), a2_ref[...]))
    ones_blk = (_iota2((LANE, LANE), 0) // D == _iota2((LANE, LANE), 1) // D).astype(BF16)

    def headsum(x):
        hi = _bf(x)
        lo = _bf(x - hi.astype(F32))
        return cat([_dot(hi[:, j:j + LANE], ones_blk) + _dot(lo[:, j:j + LANE], ones_blk)
                    for j in range(0, BW, LANE)])

    kkf = k_ * kkw_ref[...]
    kk = kkf * lax.rsqrt(jnp.maximum(headsum(kkf * kkf), 1e-24))
    r_scr[...] = r_
    k_scr[...] = k_ * (1.0 + (a_ - 1.0) * ka_ref[...])
    v_scr[...] = v_
    kk_scr[...] = kk
    b_scr[...] = kk * a_

    nchunk = tt // L
    row, col = _iota2((L, L), 0), _iota2((L, L), 1)
    tri = (row >= col).astype(BF16)
    strict = row > col
    incl2 = _iota2((L, 2 * L), 0) >= _iota2((L, 2 * L), 1) % L
    eye = (row == col).astype(F32)

    U1 = 4 if nchunk % 4 == 0 else 1
    it = range(U1 * HG)
    isl = [sls[i % HG] for i in it]

    def phase1(ci, carry):
        cidx = [ci * U1 + u for u in range(U1)]
        rws = [pl.ds(pl.multiple_of(c * L, L), L) for c in cidx]
        kkp, rg, kn, bn, kend, bend, v_c = [], [], [], [], [], [], []
        for u, c in enumerate(cidx):
            lw, k_c, b_c = lw_scr[rws[u], :], k_scr[rws[u], :], b_scr[rws[u], :]
            g = _cumsum_rows(tri, lw)
            gl = g[L - 1:L, :]
            kkp.append(kk_scr[rws[u], :] * jnp.exp(g - lw))
            rg.append(r_scr[rws[u], :] * jnp.exp(g))
            eng = jnp.exp(-g)
            kn.append(k_c * eng)
            bn.append(b_c * eng)
            ee = jnp.exp(gl - g)
            kend.append(k_c * ee)
            bend.append(b_c * ee)
            v_c.append(v_scr[rws[u], :])
            egl_scr[c] = jnp.exp(gl)
        kkp_h = [kkp[i // HG][:, isl[i]] for i in it]
        rg_h = [rg[i // HG][:, isl[i]] for i in it]
        bk = [_bf(jnp.concatenate([bn[i // HG][:, isl[i]], kn[i // HG][:, isl[i]]], axis=0)) for i in it]
        kr = [_bf(jnp.concatenate([kkp_h[i], rg_h[i]], axis=0)) for i in it]
        mnaq = [_dot_nt(kr[i], bk[i]) for i in it]
        mn = [x[:L, :] for x in mnaq]
        aq = [x[L:, :] for x in mnaq]
        N = [_bf(jnp.where(strict, m[:, L:], 0.0)) for m in mn]
        vb = [_bf(v_c[i // HG][:, isl[i]]) for i in it]
        nv = [_dot(N[i], vb[i]) for i in it]
        X = [jnp.where(strict, -m[:, :L], 0.0) for m in mn]
        Xb = [_bf(x) for x in X]
        P = [_dot(xb, xb) for xb in Xb]
        tinv = [eye + x for x in X]
        n = 2
        while n < L:
            Pb = [_bf(p) for p in P]
            if 2 * n >= L:
                tinv = [tinv[i] + _dot(_bf(tinv[i]), Pb[i]) for i in it]
            else:
                st = [_dot(_bf(jnp.concatenate([P[i], tinv[i]], axis=0)), Pb[i]) for i in it]
                P = [x[:L, :] for x in st]
                tinv = [tinv[i] + st[i][L:, :] for i in it]
            n *= 2
        gh = [_dot(_bf(tinv[i]), _bf(jnp.concatenate([kkp_h[i], nv[i]], axis=1))) for i in it]
        gm = [_bf(x[:, :D]) for x in gh]
        hm = [_bf(x[:, D:]) for x in gh]
        bend_b = [_bf(bend[i // HG][:, isl[i]]) for i in it]
        aqm = [_bf(jnp.where(incl2, a, 0.0)) for a in aq]
        hv = [jnp.concatenate([-hm[i], vb[i]], axis=0) for i in it]
        bke = [jnp.concatenate([bend_b[i], _bf(kend[i // HG][:, isl[i]])], axis=0) for i in it]
        phi = [_dot_tn(gm[i], bend_b[i]) for i in it]
        psi = [_dot_tn(hv[i], bke[i]) for i in it]
        qts = [rg_h[i] - _dot(aqm[i][:, :L], gm[i]) for i in it]
        y0s = [_dot(aqm[i], hv[i]) for i in it]
        for i in it:
            phi_scr[cidx[i // HG], i % HG] = _bf(phi[i])
            psi_scr[cidx[i // HG], i % HG] = psi[i]
        for u in range(U1):
            qt_scr[rws[u], :] = _bf(cat(qts[u * HG:(u + 1) * HG]))
            y_scr[rws[u], :] = cat(y0s[u * HG:(u + 1) * HG])
        return carry

    lax.fori_loop(0, nchunk // U1, phase1, 0)

    def phase2(c, carry):
        rows = pl.ds(pl.multiple_of(c * L, L), L)
        qt = qt_scr[rows, :]
        egl = egl_scr[c]
        S = [s_scr[h] for h in hs]
        Sb = [_bf(s_) for s_ in S]
        sphi = [_dot(Sb[h], phi_scr[c, h]) for h in hs]
        ys = [_dot_nt(qt[:, sls[h]], Sb[h]) for h in hs]
        for h in hs:
            s_scr[h] = S[h] * egl[:, sls[h]] - sphi[h] + psi_scr[c, h]
        y_scr[rows, :] += cat(ys)
        return carry

    lax.fori_loop(0, nchunk, phase2, 0)

    y = y_scr[...]
    rk = r_scr[...] * k_scr[...] * rk_ref[...]
    cen = y - headsum(y) * (1.0 / D)
    ln = cen * lax.rsqrt(headsum(cen * cen) * (1.0 / D) + EPS)
    out = ln * lng_ref[...] + headsum(rk) * v_scr[...]
    og_ref[...] = _bf(out * _silu(z_ref[...]))

    @pl.when(t == pl.num_programs(1) - 1)
    def _():
        s_ref[0] = s_scr[...]


def _rwkv(P, B, T, p, s0, shift_prev):
    L = min(T, CHUNK)
    tt = min(T, 256)
    nT = T // tt
    nc = tt // L
    off = _OFF['rwkv_shift']
    rowblk = lambda w, o: pl.BlockSpec((tt, w), lambda b_, t: (b_ * nT + t, o // w))
    const = lambda shape: pl.BlockSpec(shape, lambda b_, t: (0,) * len(shape))
    st = pl.BlockSpec((1, RWKV_H, RWKV_D, RWKV_D), lambda b_, t: (b_, 0, 0, 0))
    vec = const((1, BW))
    return pl.pallas_call(
        functools.partial(_rwkv_kernel, L=L),
        grid=(B, nT),
        in_specs=[rowblk(BW, off), rowblk(BW, off + BW), rowblk(BW, off + 2 * BW), rowblk(LANE, off + 3 * BW),
                  rowblk(BW, _OFF['rwkv_z']),
                  pl.BlockSpec((1, 1, RWKV_SHIFT_W), lambda b_, t: (b_, 0, 0)), const((1, RWKV_SHIFT_W)),
                  const((LANE, BW)), const((LANE, BW)), vec, vec, vec, vec, vec, vec, st],
        out_specs=[pl.BlockSpec((tt, BW), lambda b_, t: (b_ * nT + t, 0)), st],
        out_shape=[jax.ShapeDtypeStruct((B * T, BW), BF16), jax.ShapeDtypeStruct(s0.shape, F32)],
        scratch_shapes=[pltpu.VMEM((RWKV_H, RWKV_D, RWKV_D), F32),
                        pltpu.VMEM((nc, RWKV_H, RWKV_D, RWKV_D), BF16),
                        pltpu.VMEM((nc, RWKV_H, RWKV_D, RWKV_D), F32),
                        pltpu.VMEM((tt, BW), BF16),
                        pltpu.VMEM((nc, 1, BW), F32),
                        pltpu.VMEM((1, RWKV_SHIFT_W), F32)] + [pltpu.VMEM((tt, BW), F32)] * 7,
        compiler_params=_cparams("parallel", "arbitrary"),
        name="rwkv_scan",
    )(P, P, P, P, P, shift_prev, p['rwkv_mu'], p['rwkv_w2p'], p['rwkv_a2p'], p['rwkv_w0'], p['rwkv_a0'],
      p['rwkv_k_k'], p['rwkv_k_a'], p['rwkv_r_k'], p['rwkv_ln_g'], s0)


def _gla_kernel(q_ref, k_ref, v_ref, gd_ref, z_ref, g2_ref, gb_ref, ng_ref, s0_ref, o_ref, s_ref, s_scr, g_scr, *, L):
    t = pl.program_id(1)

    @pl.when(t == 0)
    def _():
        s_scr[...] = s0_ref[0]

    gk = _dot(_bf(gd_ref[...]), g2_ref[...]) + gb_ref[...]
    g_scr[...] = _log_sigmoid(gk) * (1.0 / GLA_GATE_NORM)

    nchunk = q_ref.shape[0] // L
    row, col = _iota2((L, L), 0), _iota2((L, L), 1)
    tri = (row >= col).astype(BF16)
    causal = row >= col
    scale = GLA_DK ** -0.5

    U = 4 if nchunk % 4 == 0 else 1
    hs, us = range(GLA_H), range(U)
    ksl = [slice(h * GLA_DK, (h + 1) * GLA_DK) for h in hs]
    vsl = [slice(h * GLA_DV, (h + 1) * GLA_DV) for h in hs]
    uh = [(u, h) for u in us for h in hs]

    def chunks(i, carry):
        rws = [pl.ds(pl.multiple_of((i * U + u) * L, L), L) for u in us]
        q = [q_ref[r, :] * scale for r in rws]
        k = [k_ref[r, :] for r in rws]
        b = [_cumsum_rows(tri, g_scr[r, :]) for r in rws]
        b_end = [x[L - 1:L, :] for x in b]
        qi = [_bf(q[u] * jnp.exp(b[u])) for u in us]
        ki = [_bf(k[u] * jnp.exp(-b[u])) for u in us]
        kend = [_bf(k[u] * jnp.exp(b_end[u] - b[u])) for u in us]
        eb_end = [jnp.exp(x) for x in b_end]
        vb = {(u, h): _bf(v_ref[rws[u], vsl[h]]) for u, h in uh}
        A = {(u, h): _bf(jnp.where(causal, _dot_nt(qi[u][:, ksl[h]], ki[u][:, ksl[h]]), 0.0)) for u, h in uh}
        kv = {(u, h): _dot_tn(vb[u, h], kend[u][:, ksl[h]]) for u, h in uh}
        av = {(u, h): _dot(A[u, h], vb[u, h]) for u, h in uh}
        St = [s_scr[h] for h in hs]
        o = {}
        for u in us:
            for h in hs:
                o[u, h] = av[u, h] + _dot_nt(qi[u][:, ksl[h]], _bf(St[h]))
            St = [St[h] * eb_end[u][:, ksl[h]] + kv[u, h] for h in hs]
        ms = {x: jnp.mean(o[x] * o[x], axis=-1, keepdims=True) for x in uh}
        for u, h in uh:
            o_ref[rws[u], vsl[h]] = _bf((o[u, h] * lax.rsqrt(ms[u, h] + EPS)) * ng_ref[:, vsl[h]]
                                        * _silu(z_ref[rws[u], vsl[h]]))
        for h in hs:
            s_scr[h] = St[h]
        return carry

    lax.fori_loop(0, nchunk // U, chunks, 0)

    @pl.when(t == pl.num_programs(1) - 1)
    def _():
        s_ref[0] = s_scr[...]


def _gla(P, B, T, g2p, gb, ng, s0t):
    L = min(T, CHUNK)
    tt = min(T, 512)
    nT = T // tt
    rowblk = lambda w, off: pl.BlockSpec((tt, w), lambda b_, t: (b_ * nT + t, off // w))
    const = lambda shape: pl.BlockSpec(shape, lambda b_, t: (0,) * len(shape))
    st = pl.BlockSpec((1, GLA_H, GLA_DV, GLA_DK), lambda b_, t: (b_, 0, 0, 0))
    return pl.pallas_call(
        functools.partial(_gla_kernel, L=L),
        grid=(B, nT),
        in_specs=[rowblk(512, _OFF['gla_q']), rowblk(512, _OFF['gla_k']), rowblk(BW, _OFF['gla_v']),
                  rowblk(LANE, _OFF['gla_gd']), rowblk(BW, _OFF['gla_z']), const((LANE, 512)), const((1, 512)),
                  const((1, BW)), st],
        out_specs=[pl.BlockSpec((tt, BW), lambda b_, t: (b_ * nT + t, 0)), st],
        out_shape=[jax.ShapeDtypeStruct((B * T, BW), BF16), jax.ShapeDtypeStruct(s0t.shape, F32)],
        scratch_shapes=[pltpu.VMEM((GLA_H, GLA_DV, GLA_DK), F32), pltpu.VMEM((tt, 512), F32)],
        compiler_params=_cparams("parallel", "arbitrary"),
        name="gla_scan",
    )(P, P, P, P, P, g2p, gb, ng, s0t)


def _ret_kernel(q_ref, k_ref, v_ref, z_ref, cos_ref, sin_ref, s0_ref, o_ref, s_ref, s_scr, *, L):
    t = pl.program_id(1)

    @pl.when(t == 0)
    def _():
        s_scr[...] = s0_ref[0]

    nchunk = q_ref.shape[0] // L
    rel = (_iota2((L, L), 0) - _iota2((L, L), 1)).astype(F32)
    tcol = _iota2((L, 1), 0).astype(F32)
    scale = RET_DK ** -0.5
    lgs = [math.log(1.0 - 2.0 ** (-5.0 - h)) for h in range(RET_H)]
    decay = [jnp.where(rel >= 0, jnp.exp(lg * jnp.maximum(rel, 0.0)), 0.0) for lg in lgs]
    rowd = [jnp.exp(lg * (tcol + 1.0)) for lg in lgs]
    cold = [jnp.exp(lg * (L - 1.0 - tcol)) for lg in lgs]

    def rope(x, cos2, sin2):
        return x * cos2 + pltpu.roll(x, RET_DK // 2, 1) * sin2

    U = 4 if nchunk % 4 == 0 else 1
    hs, us = range(RET_H), range(U)
    ksl = [slice(h * RET_DK, (h + 1) * RET_DK) for h in hs]
    vsl = [slice(h * RET_DV, (h + 1) * RET_DV) for h in hs]
    uh = [(u, h) for u in us for h in hs]

    def chunks(i, carry):
        rws = [pl.ds(pl.multiple_of((i * U + u) * L, L), L) for u in us]
        cs = [(cos_ref[r, :], sin_ref[r, :]) for r in rws]
        k = {(u, h): rope(k_ref[rws[u], ksl[h]], *cs[u]) for u, h in uh}
        qb = {(u, h): _bf(rope(q_ref[rws[u], ksl[h]], *cs[u]) * scale) for u, h in uh}
        kb = {x: _bf(k[x]) for x in uh}
        kcb = {(u, h): _bf(k[u, h] * cold[h]) for u, h in uh}
        vb = {(u, h): _bf(v_ref[rws[u], vsl[h]]) for u, h in uh}
        A = {(u, h): _bf(_dot_nt(qb[u, h], kb[u, h]) * decay[h]) for u, h in uh}
        kv = {x: _dot_tn(kcb[x], vb[x]) for x in uh}
        av = {x: _dot(A[x], vb[x]) for x in uh}
        S = [s_scr[h] for h in hs]
        o = {}
        for u in us:
            for h in hs:
                o[u, h] = av[u, h] + rowd[h] * _dot(qb[u, h], _bf(S[h]))
            S = [math.exp(lgs[h] * L) * S[h] + kv[u, h] for h in hs]
        ms = {x: jnp.mean(o[x] * o[x], axis=-1, keepdims=True) for x in uh}
        for u, h in uh:
            o_ref[rws[u], vsl[h]] = _bf(o[u, h] * lax.rsqrt(ms[u, h] + EPS) * _silu(z_ref[rws[u], vsl[h]]))
        for h in hs:
            s_scr[h] = S[h]
        return carry

    lax.fori_loop(0, nchunk // U, chunks, 0)

    @pl.when(t == pl.num_programs(1) - 1)
    def _():
        s_ref[0] = s_scr[...]


def _ret(P, B, T, cos2, sin2, s0):
    L = min(T, CHUNK)
    tt = min(T, 512)
    nT = T // tt
    rowblk = lambda w, off: pl.BlockSpec((tt, w), lambda b_, t: (b_ * nT + t, off // w))
    tab = pl.BlockSpec((tt, RET_DK), lambda b_, t: (t, 0))
    st = pl.BlockSpec((1, RET_H, RET_DK, RET_DV), lambda b_, t: (b_, 0, 0, 0))
    return pl.pallas_call(
        functools.partial(_ret_kernel, L=L),
        grid=(B, nT),
        in_specs=[rowblk(512, _OFF['ret_q']), rowblk(512, _OFF['ret_k']), rowblk(BW, _OFF['ret_v']),
                  rowblk(BW, _OFF['ret_z']), tab, tab, st],
        out_specs=[pl.BlockSpec((tt, BW), lambda b_, t: (b_ * nT + t, 0)), st],
        out_shape=[jax.ShapeDtypeStruct((B * T, BW), BF16), jax.ShapeDtypeStruct(s0.shape, F32)],
        scratch_shapes=[pltpu.VMEM((RET_H, RET_DK, RET_DV), F32)],
        compiler_params=_cparams("parallel", "arbitrary"),
        name="ret_scan",
    )(P, P, P, P, cos2, sin2, s0)


def _mlstm_pre_kernel(xm_ref, cp_ref, cw_ref, cb_ref, wq_ref, wk_ref, wv_ref, wif_ref, bif_ref,
                      xc_ref, q_ref, k_ref, v_ref, g_ref, carry_scr):
    t = pl.program_id(1)
    tm = xm_ref.shape[0]
    nprev = ML_CONV - 1

    @pl.when(t == 0)
    def _():
        carry_scr[...] = jnp.zeros_like(carry_scr)
        carry_scr[8 - nprev:8, :] = cp_ref[0]

    xm = xm_ref[...]
    c8 = carry_scr[...]
    rid = _iota2((8, 1), 0)
    conv = xm * cw_ref[nprev:nprev + 1, :] + cb_ref[...]
    for j in range(1, ML_CONV):
        rolled = pltpu.roll(xm, j, 0)
        head = jnp.where(rid < j, pltpu.roll(c8, j, 0), rolled[0:8, :])
        prev = jnp.concatenate([head, rolled[8:, :]], axis=0) if tm > 8 else head
        conv = conv + prev * cw_ref[nprev - j:nprev - j + 1, :]
    carry_scr[...] = xm[tm - 8:tm, :]
    xc = _silu(conv)
    xc_ref[...] = xc
    xcb, xmb = _bf(xc), _bf(xm)
    q = _dot(xcb, wq_ref[...])
    k = _dot(xcb, wk_ref[...])
    v = _dot(xmb, wv_ref[...])
    g_ref[...] = (_dot(_bf(q), wif_ref[0]) + _dot(_bf(k), wif_ref[1]) + _dot(_bf(v), wif_ref[2])) + bif_ref[...]
    q_ref[...] = q
    k_ref[...] = k * (ML_D ** -0.5)
    v_ref[...] = v


def _mlstm_pre(P, B, T, conv_prev, p):
    tm = min(T, 512)
    nT = T // tm
    row = pl.BlockSpec((tm, BW), lambda b_, t: (b_ * nT + t, 0))
    const = lambda shape: pl.BlockSpec(shape, lambda b_, t: (0,) * len(shape))
    M = B * T
    return pl.pallas_call(
        _mlstm_pre_kernel,
        grid=(B, nT),
        in_specs=[pl.BlockSpec((tm, BW), lambda b_, t: (b_ * nT + t, _OFF['ml_x'] // BW)),
                  pl.BlockSpec((1, ML_CONV - 1, BW), lambda b_, t: (b_, 0, 0)),
                  const((ML_CONV, BW)), const((1, BW)), const((BW, BW)), const((BW, BW)), const((BW, BW)),
                  const((3, BW, LANE)), const((1, LANE))],
        out_specs=[row, row, row, row, pl.BlockSpec((tm, LANE), lambda b_, t: (b_ * nT + t, 0))],
        out_shape=[jax.ShapeDtypeStruct((M, BW), F32)] * 4 + [jax.ShapeDtypeStruct((M, LANE), F32)],
        scratch_shapes=[pltpu.VMEM((8, BW), F32)],
        compiler_params=_cparams("parallel", "arbitrary"),
        name="mlstm_pre",
    )(P, conv_prev, p['ml_conv_w'], p['ml_conv_b'], p['ml_wq'], p['ml_wk'], p['ml_wv'], p['ml_wif'], p['ml_bif'])


def _mlstm_kernel(q_ref, k_ref, v_ref, gc_ref, gr_ref, xc_ref, z_ref, ng_ref, sk_ref, c0_ref, n0_ref, m0_ref,
                  o_ref, c_ref, n_ref, m_ref, c_scr, n_scr, m_scr, *, L):
    t = pl.program_id(1)

    @pl.when(t == 0)
    def _():
        c_scr[...] = c0_ref[0]
        n_scr[...] = n0_ref[0]
        m_scr[...] = m0_ref[0]

    nchunk = q_ref.shape[0] // L
    row, col = _iota2((L, L), 0), _iota2((L, L), 1)
    tri = (row >= col).astype(BF16)
    triu = (row <= col).astype(BF16)
    causal = row >= col

    U = 2 if nchunk % 2 == 0 else 1
    hs, us = range(ML_H), range(U)
    sl = [slice(h * ML_D, (h + 1) * ML_D) for h in hs]
    uh = [(u, h) for u in us for h in hs]

    def chunks(i, carry):
        cidx = [i * U + u for u in us]
        rws = [pl.ds(pl.multiple_of(c * L, L), L) for c in cidx]
        gcb = [gc_ref[0, c] for c in cidx]
        grb = [gr_ref[0, c] for c in cidx]
        bc = [_cumsum_rows(tri, x) for x in gcb]
        br = [_cumsum_cols(x, triu) for x in grb]
        q = {(u, h): q_ref[rws[u], sl[h]] for u, h in uh}
        k = {(u, h): k_ref[rws[u], sl[h]] for u, h in uh}
        qb = {x: _bf(q[x]) for x in uh}
        vb = {(u, h): _bf(v_ref[rws[u], sl[h]]) for u, h in uh}
        qk = {x: _dot_nt(qb[x], _bf(k[x])) for x in uh}
        b_c = {(u, h): bc[u][:, ML_H + h:ML_H + h + 1] for u, h in uh}
        dlog = {(u, h): jnp.where(causal, b_c[u, h] - br[u][ML_H + h:ML_H + h + 1, :] + grb[u][h:h + 1, :], -jnp.inf)
                for u, h in uh}
        m_loc = {x: jnp.max(dlog[x], axis=-1, keepdims=True) for x in uh}
        s_loc = {x: qk[x] * jnp.exp(dlog[x] - m_loc[x]) for x in uh}
        rs = {x: jnp.sum(s_loc[x], axis=-1, keepdims=True) for x in uh}
        b_end = {x: b_c[x][L - 1:L, :] for x in uh}
        m_le = {x: m_loc[x][L - 1:L, :] for x in uh}
        kw = {(u, h): k[u, h] * jnp.exp(b_end[u, h] - b_c[u, h] + gcb[u][:, h:h + 1] - m_le[u, h]) for u, h in uh}
        sv = {x: _dot(_bf(s_loc[x]), vb[x]) for x in uh}
        kv = {x: _dot_tn(_bf(kw[x]), vb[x]) for x in uh}
        ksum = {x: jnp.sum(kw[x], axis=0, keepdims=True) for x in uh}
        C = [c_scr[h] for h in hs]
        nvec = [n_scr[h] for h in hs]
        m_prev = [m_scr[h][:, 0:1] for h in hs]
        hh = {}
        for u in us:
            qc = [_dot(qb[u, h], _bf(C[h])) for h in hs]
            qn = [jnp.sum(q[u, h] * nvec[h], axis=-1, keepdims=True) for h in hs]
            from_state = [b_c[u, h] + m_prev[h] for h in hs]
            m_t = [jnp.maximum(from_state[h], m_loc[u, h]) for h in hs]
            a_in = [jnp.exp(m_loc[u, h] - m_t[h]) for h in hs]
            w_state = [jnp.exp(from_state[h] - m_t[h]) for h in hs]
            den = [a_in[h] * rs[u, h] + w_state[h] * qn[h] for h in hs]
            for h in hs:
                hh[u, h] = ((a_in[h] * sv[u, h] + w_state[h] * qc[h])
                            / jnp.maximum(jnp.abs(den[h]), jnp.exp(-m_t[h])))
            m_new = [m_t[h][L - 1:L, :] for h in hs]
            cd = [jnp.exp(b_end[u, h] + m_prev[h] - m_new[h]) for h in hs]
            sc = [jnp.exp(m_le[u, h] - m_new[h]) for h in hs]
            C = [cd[h] * C[h] + sc[h] * kv[u, h] for h in hs]
            nvec = [cd[h] * nvec[h] + sc[h] * ksum[u, h] for h in hs]
            m_prev = m_new
        mean = {x: jnp.mean(hh[x], axis=-1, keepdims=True) for x in uh}
        cen = {x: hh[x] - mean[x] for x in uh}
        var = {x: jnp.mean(cen[x] * cen[x], axis=-1, keepdims=True) for x in uh}
        for u, h in uh:
            y = cen[u, h] * lax.rsqrt(var[u, h] + EPS)
            o_ref[rws[u], sl[h]] = _bf((y * ng_ref[:, sl[h]] + sk_ref[:, sl[h]] * xc_ref[rws[u], sl[h]])
                                       * _silu(z_ref[rws[u], sl[h]]))
        for h in hs:
            c_scr[h] = C[h]
            n_scr[h] = nvec[h]
            m_scr[h] = jnp.broadcast_to(m_prev[h], (1, LANE))
        return carry

    lax.fori_loop(0, nchunk // U, chunks, 0)

    @pl.when(t == pl.num_programs(1) - 1)
    def _():
        c_ref[0] = c_scr[...]
        n_ref[0] = n_scr[...]
        m_ref[0] = m_scr[...]


def _mlstm(q, k, v, gc, gr, xc, P, ng, sk, c0, n0, m0, B, T):
    L = min(T, CHUNK)
    tt = min(T, 512)
    nT = T // tt
    nc = tt // L
    row = pl.BlockSpec((tt, BW), lambda b_, t: (b_ * nT + t, 0))
    const = pl.BlockSpec((1, BW), lambda b_, t: (0, 0))
    stc = pl.BlockSpec((1, ML_H, ML_D, ML_D), lambda b_, t: (b_, 0, 0, 0))
    stn = pl.BlockSpec((1, ML_H, 1, ML_D), lambda b_, t: (b_, 0, 0, 0))
    stm = pl.BlockSpec((1, ML_H, 1, LANE), lambda b_, t: (b_, 0, 0, 0))
    return pl.pallas_call(
        functools.partial(_mlstm_kernel, L=L),
        grid=(B, nT),
        in_specs=[row, row, row,
                  pl.BlockSpec((1, nc, L, 8), lambda b_, t: (b_, t, 0, 0)),
                  pl.BlockSpec((1, nc, 8, L), lambda b_, t: (b_, t, 0, 0)),
                  row, pl.BlockSpec((tt, BW), lambda b_, t: (b_ * nT + t, _OFF['ml_z'] // BW)), const, const, stc, stn, stm],
        out_specs=[row, stc, stn, stm],
        out_shape=[jax.ShapeDtypeStruct((B * T, BW), BF16), jax.ShapeDtypeStruct(c0.shape, F32),
                   jax.ShapeDtypeStruct(n0.shape, F32), jax.ShapeDtypeStruct(m0.shape, F32)],
        scratch_shapes=[pltpu.VMEM((ML_H, ML_D, ML_D), F32), pltpu.VMEM((ML_H, 1, ML_D), F32),
                        pltpu.VMEM((ML_H, 1, LANE), F32)],
        compiler_params=_cparams("parallel", "arbitrary"),
        name="mlstm_scan",
    )(q, k, v, gc, gr, xc, P, ng, sk, c0, n0, m0)


def _xattn_kernel(q_ref, z_ref, mk_ref, mv_ref, o_ref):
    hs = range(XA_H)
    sl = [slice(h * XA_D, (h + 1) * XA_D) for h in hs]
    s = [_dot_nt(_bf(q_ref[:, x]), _bf(mk_ref[0, :, x])) * (XA_D ** -0.5) for x in sl]
    e = [jnp.exp(x - jnp.max(x, axis=-1, keepdims=True)) for x in s]
    prob = [_bf(x / jnp.sum(x, axis=-1, keepdims=True)) for x in e]
    o = [_dot(prob[h], _bf(mv_ref[0, :, sl[h]])) for h in hs]
    for h in hs:
        o_ref[:, sl[h]] = _bf(o[h] * _silu(z_ref[:, sl[h]]))


def _xattn(P, B, T, mk, mv):
    tt = min(T, 512)
    nT = T // tt
    n_mem = mk.shape[1]
    mem = pl.BlockSpec((1, n_mem, BW), lambda b_, t: (b_, 0, 0))
    return pl.pallas_call(
        _xattn_kernel,
        grid=(B, nT),
        in_specs=[pl.BlockSpec((tt, BW), lambda b_, t: (b_ * nT + t, _OFF['xa_q'] // BW)),
                  pl.BlockSpec((tt, BW), lambda b_, t: (b_ * nT + t, _OFF['xa_z'] // BW)), mem, mem],
        out_specs=pl.BlockSpec((tt, BW), lambda b_, t: (b_ * nT + t, 0)),
        out_shape=jax.ShapeDtypeStruct((B * T, BW), BF16),
        compiler_params=_cparams("parallel", "parallel"),
        name="xattn",
    )(P, P, mk, mv)


def _merge_kernel(o0, o1, o2, o3, o4, g_ref, w_ref, out_ref):
    ys = [_dot(o_ref[...], w_ref[i]) for i, o_ref in enumerate((o0, o1, o2, o3, o4))]
    acc = None
    for i, y in enumerate(ys):
        term = _sigmoid(g_ref[:, i * D_MODEL:(i + 1) * D_MODEL]) * y
        acc = term if acc is None else acc + term
    out_ref[...] = _bf(acc)


def _merge(ogs, P, wb, tm):
    M = ogs[0].shape[0]
    gw = N_BRANCH * D_MODEL
    og = pl.BlockSpec((tm, BW), lambda i: (i, 0))
    return pl.pallas_call(
        _merge_kernel,
        grid=(M // tm,),
        in_specs=[og] * N_BRANCH + [pl.BlockSpec((tm, gw), lambda i: (i, _OFF['gates'] // gw)),
                                    pl.BlockSpec((N_BRANCH, BW, D_MODEL), lambda i: (0, 0, 0),
                                                 pipeline_mode=pl.Buffered(1))],
        out_specs=pl.BlockSpec((tm, D_MODEL), lambda i: (i, 0)),
        out_shape=jax.ShapeDtypeStruct((M, D_MODEL), BF16),
        compiler_params=_cparams("parallel"),
        name="merge",
    )(*ogs, P, wb)


def _resid_kernel(x_ref, m_ref, w_ref, o_ref):
    o_ref[...] = x_ref[...] + _dot(m_ref[...], w_ref[...])


def _resid_norm_kernel(x_ref, m_ref, w_ref, g_ref, o_ref):
    y = x_ref[...] + _dot(m_ref[...], w_ref[...])
    ms = jnp.mean(y * y, axis=-1, keepdims=True)
    o_ref[...] = (y * lax.rsqrt(ms + EPS)) * g_ref[...]


def _resid_out(x, merged, w, tm, final_g=None):
    M = x.shape[0]
    row = pl.BlockSpec((tm, D_MODEL), lambda i: (i, 0))
    in_specs = [row, row, pl.BlockSpec((D_MODEL, D_MODEL), lambda i: (0, 0), pipeline_mode=pl.Buffered(1))]
    args = [x, merged, w]
    if final_g is not None:
        in_specs.append(pl.BlockSpec((1, D_MODEL), lambda i: (0, 0)))
        args.append(final_g)
    return pl.pallas_call(
        _resid_kernel if final_g is None else _resid_norm_kernel,
        grid=(M // tm,),
        in_specs=in_specs,
        out_specs=row,
        out_shape=jax.ShapeDtypeStruct((M, D_MODEL), F32),
        compiler_params=_cparams("parallel"),
        name="resid_out",
    )(*args)


def _blockdiag(w):
    depth, n, c, d = w.shape
    rows = w.reshape(depth, n * c, d)
    tile = (jnp.arange(d)[:, None] == jnp.arange(n * d)[None, :] % d).astype(w.dtype)
    dense = jnp.einsum('lrd,dj->lrj', rows, tile)
    mask = (jnp.arange(n * c)[:, None] // c) == (jnp.arange(n * d)[None, :] // d)
    return jnp.where(mask, dense, 0.0).astype(BF16)


def _rope_tables(start, T):
    half = RET_DK // 2
    inv = ROPE_BASE ** (-jnp.linspace(0.0, 1.0, half, dtype=F32))
    blk = 128
    if T % blk or T <= blk:
        ang = (start + jnp.arange(T, dtype=F32))[:, None] * inv[None, :]
        cos, sin = jnp.cos(ang), jnp.sin(ang)
    else:
        hi = (start + blk * jnp.arange(T // blk, dtype=F32))[:, None] * inv[None, :]
        lo = jnp.arange(blk, dtype=F32)[:, None] * inv[None, :]
        ch, sh, cl, sl_ = jnp.cos(hi)[:, None, :], jnp.sin(hi)[:, None, :], jnp.cos(lo)[None], jnp.sin(lo)[None]
        cos = (ch * cl - sh * sl_).reshape(T, half)
        sin = (sh * cl + ch * sl_).reshape(T, half)
    return jnp.concatenate([cos, cos], axis=-1), jnp.concatenate([-sin, sin], axis=-1)


def _layer(x, B, T, tabs, st, mem_k, mem_v, p, final_g=None):
    M = B * T
    tm = min(M, 1024)
    P = _normproj(x, p['norm_g'], p['w_pack'], tm, PROJ_TN)
    P3 = P.reshape(B, T, NP)

    og_a, s_rwkv = _rwkv(P, B, T, p, st['rwkv'], st['rwkv_shift'])
    shift_new = P3[:, T - 1:, _OFF['rwkv_shift']:_OFF['rwkv_shift'] + RWKV_SHIFT_W]

    og_b, s_gla_t = _gla(P, B, T, p['gla_g2p'], p['gla_gb'], p['gla_norm_g'], jnp.swapaxes(st['gla'], -1, -2))
    s_gla = jnp.swapaxes(s_gla_t, -1, -2)

    og_c, s_ret = _ret(P, B, T, tabs[0], tabs[1], st['ret'])

    xc, q_m, k_m, v_m, gates = _mlstm_pre(P, B, T, st['ml_conv'], p)
    L = min(T, CHUNK)
    gcol = jnp.concatenate([gates[:, :ML_H], _log_sigmoid(gates[:, ML_H:2 * ML_H])], axis=-1).reshape(B, T // L, L, 2 * ML_H)
    grow = jnp.swapaxes(gcol, -1, -2)
    og_d, c_new, n_new, m_new = _mlstm(
        q_m, k_m, v_m, gcol, grow, xc, P, p['ml_norm_g'], p['ml_skip'], st['ml_c'], st['ml_n'][:, :, None, :],
        jnp.broadcast_to(st['ml_m'][:, :, None, None], st['ml_m'].shape + (1, LANE)), B, T)
    xm_tail = P3[:, max(T - (ML_CONV - 1), 0):, _OFF['ml_x']:_OFF['ml_x'] + BW]
    conv_new = jnp.concatenate([st['ml_conv'], xm_tail], axis=1)[:, -(ML_CONV - 1):]

    og_x = _xattn(P, B, T, mem_k.reshape(B, -1, BW), mem_v.reshape(B, -1, BW))

    merged = _merge([og_a, og_b, og_c, og_d, og_x], P, p['w_branch'], min(M, 256))
    x_new = _resid_out(x, merged, p['w_out'], min(M, 512), final_g)
    new = {'rwkv': s_rwkv, 'rwkv_shift': shift_new, 'gla': s_gla, 'ret': s_ret, 'ml_c': c_new,
           'ml_n': n_new[:, :, 0, :], 'ml_m': m_new[:, :, 0, 0], 'ml_conv': conv_new}
    return x_new, new


_STATE_KEYS = ('rwkv', 'rwkv_shift', 'gla', 'ret', 'ml_c', 'ml_n', 'ml_m', 'ml_conv')


def kernel(x_prompt, x_sample, mem_prompt, cache_mem_k, cache_mem_v, state_rwkv, state_rwkv_shift, state_gla, state_ret, state_mlstm_c, state_mlstm_n, state_mlstm_m, state_mlstm_conv, norm_g, mem_norm_g, w_in, w_mem_kv, rwkv_mu, rwkv_w0, rwkv_w2, rwkv_a0, rwkv_a2, rwkv_k_k, rwkv_k_a, rwkv_r_k, rwkv_ln_g, gla_g2, gla_gb, gla_norm_g, ml_conv_w, ml_conv_b, ml_wq, ml_wk, ml_wv, ml_w_if, ml_b_if, ml_skip, ml_norm_g, w_branch, w_out, final_norm_g):
    Bp, Tp, D = x_prompt.shape
    Bs, Ts, _ = x_sample.shape
    depth = w_in.shape[0]
    n_mem = mem_prompt.shape[1]

    params = {
        'norm_g': norm_g[:, None, :], 'mem_norm_g': mem_norm_g[:, None, :],
        'w_mem_kv': _bf(w_mem_kv),
        'rwkv_mu': rwkv_mu[:, None, :], 'rwkv_w0': rwkv_w0[:, None, :], 'rwkv_a0': rwkv_a0[:, None, :],
        'rwkv_w2p': _bf(jnp.pad(rwkv_w2, ((0, 0), (0, LANE - RWKV_LORA), (0, 0)))),
        'rwkv_a2p': _bf(jnp.pad(rwkv_a2, ((0, 0), (LANE - RWKV_LORA, 0), (0, 0)))),
        'rwkv_k_k': rwkv_k_k[:, None, :], 'rwkv_k_a': rwkv_k_a[:, None, :], 'rwkv_r_k': rwkv_r_k[:, None, :],
        'rwkv_ln_g': rwkv_ln_g[:, None, :],
        'gla_g2p': _bf(jnp.pad(gla_g2, ((0, 0), (0, LANE - GLA_LORA), (0, 0)))), 'gla_gb': gla_gb[:, None, :],
        'gla_norm_g': gla_norm_g[:, None, :],
        'ml_conv_w': ml_conv_w, 'ml_conv_b': ml_conv_b[:, None, :],
        'ml_wq': _blockdiag(ml_wq), 'ml_wk': _blockdiag(ml_wk), 'ml_wv': _blockdiag(ml_wv),
        'ml_wif': _bf(jnp.pad(ml_w_if.reshape(depth, 3, BW, 2 * ML_H), ((0, 0), (0, 0), (0, 0), (0, LANE - 2 * ML_H)))),
        'ml_bif': jnp.pad(ml_b_if, ((0, 0), (0, LANE - 2 * ML_H)))[:, None, :],
        'ml_skip': ml_skip[:, None, :], 'ml_norm_g': ml_norm_g[:, None, :],
        'w_branch': _bf(w_branch), 'w_out': _bf(w_out),
    }
    cache = {'rwkv': state_rwkv, 'rwkv_shift': state_rwkv_shift, 'gla': state_gla, 'ret': state_ret,
             'ml_c': state_mlstm_c, 'ml_n': state_mlstm_n, 'ml_m': state_mlstm_m, 'ml_conv': state_mlstm_conv,
             'mem_k': cache_mem_k, 'mem_v': cache_mem_v}
    tabs_p = _rope_tables(0.0, Tp)
    tabs_s = _rope_tables(float(PAST_LEN), Ts)
    zero_p = {
        'rwkv': jnp.zeros((Bp, RWKV_H, RWKV_D, RWKV_D), F32), 'rwkv_shift': jnp.zeros((Bp, 1, RWKV_SHIFT_W), F32),
        'gla': jnp.zeros((Bp, GLA_H, GLA_DK, GLA_DV), F32), 'ret': jnp.zeros((Bp, RET_H, RET_DK, RET_DV), F32),
        'ml_c': jnp.zeros((Bp, ML_H, ML_D, ML_D), F32), 'ml_n': jnp.zeros((Bp, ML_H, ML_D), F32),
        'ml_m': jnp.zeros((Bp, ML_H), F32), 'ml_conv': jnp.zeros((Bp, ML_CONV - 1, BW), F32),
    }
    mem2d = mem_prompt.reshape(Bp * n_mem, D)

    yp, ys = x_prompt.reshape(Bp * Tp, D), x_sample.reshape(Bs * Ts, D)
    outs = []
    for l in range(depth):
        p = {nm: arr[l] for nm, arr in params.items()}
        p['w_pack'] = _pack_w_in(w_in[l])
        kv = _normproj(mem2d, p['mem_norm_g'], p['w_mem_kv'], min(Bp * n_mem, 512), 512)
        mk_l = kv[:, :BW].reshape(Bp, n_mem, XA_H, XA_D)
        mv_l = kv[:, BW:].reshape(Bp, n_mem, XA_H, XA_D)
        fg = final_norm_g[None, :] if l == depth - 1 else None
        yp, stp_l = _layer(yp, Bp, Tp, tabs_p, zero_p, mk_l, mv_l, p, fg)
        ys, sts_l = _layer(ys, Bs, Ts, tabs_s, {nm: cache[nm][l] for nm in _STATE_KEYS},
                           cache['mem_k'][l], cache['mem_v'][l], p, fg)
        outs.append((stp_l, mk_l, mv_l, sts_l))
    stp = {nm: jnp.stack([o[0][nm] for o in outs]) for nm in _STATE_KEYS}
    sts = {nm: jnp.stack([o[3][nm] for o in outs]) for nm in _STATE_KEYS}
    mk = jnp.stack([o[1] for o in outs])
    mv = jnp.stack([o[2] for o in outs])
    y_prompt = yp.reshape(Bp, Tp, D)
    y_sample = ys.reshape(Bs, Ts, D)
    return (y_prompt, y_sample,
            stp['rwkv'], stp['rwkv_shift'], stp['gla'], stp['ret'], stp['ml_c'], stp['ml_n'], stp['ml_m'],
            stp['ml_conv'], mk, mv,
            sts['rwkv'], sts['rwkv_shift'], sts['gla'], sts['ret'], sts['ml_c'], sts['ml_n'], sts['ml_m'],
            sts['ml_conv'])
```

```python
import functools
import math

import jax
import jax.numpy as jnp
from jax import lax
from jax.experimental import pallas as pl
from jax.experimental.pallas import tpu as pltpu

F32 = jnp.float32
BF16 = jnp.bfloat16

D_MODEL = 2048
BW = 1024
EPS = 1e-6
CHUNK = 64
N_BRANCH = 5
PAST_LEN = 2048
RWKV_H, RWKV_D, RWKV_LORA = 16, 64, 64
RWKV_SHIFT_W = 3 * BW + 2 * RWKV_LORA
RWKV_DECAY_SCALE = 0.606531
GLA_H, GLA_DK, GLA_DV, GLA_LORA = 4, 128, 256, 16
GLA_GATE_NORM = 16.0
RET_H, RET_DK, RET_DV = 4, 128, 256
ROPE_BASE = 10000.0
ML_H, ML_D, ML_CONV = 4, 256, 4
XA_H, XA_D = 4, 256
LANE = 128
SUBLANE = 8
VMEM_LIMIT = 56 * 1024 * 1024

PROJ_TM = 1024
MEM_TM, MEM_TN = 512, 512
RWKV_TT = 256
SCAN_TT = 512
MERGE_TM = 256
RESID_TM = 512
ROPE_BLK = 128

_IN_LAYOUT = (
    ('rwkv_shift', RWKV_SHIFT_W), ('rwkv_z', BW), ('gla_q', 512), ('gla_k', 512), ('gla_v', BW),
    ('gla_gd', GLA_LORA), ('gla_z', BW), ('ret_q', 512), ('ret_k', 512), ('ret_v', BW), ('ret_z', BW),
    ('ml_x', BW), ('ml_z', BW), ('xa_q', BW), ('xa_z', BW), ('gates', N_BRANCH * D_MODEL),
)
_PACK_ORDER = ('gates', 'rwkv_z', 'gla_z', 'ret_z', 'ml_z', 'xa_z', 'ret_q', 'ret_k', 'ret_v',
               'gla_q', 'gla_k', 'gla_v', 'ml_x', 'xa_q', 'rwkv_shift', 'gla_gd')
PROJ_TN = 1792


def _src_cols(name):
    start = 0
    for nm, size in _IN_LAYOUT:
        if nm == name:
            return start, start + size
        start += size
    raise KeyError(name)


def _pack_offsets():
    off, cur = {}, 0
    for nm in _PACK_ORDER:
        a, b = _src_cols(nm)
        off[nm] = cur
        cur += -(-(b - a) // LANE) * LANE
    total = -(-cur // PROJ_TN) * PROJ_TN
    return off, total


_OFF, NP = _pack_offsets()


def _pack_w_in(w_in):
    parts, cur = [], 0
    for nm in _PACK_ORDER:
        a, b = _src_cols(nm)
        parts.append(w_in[..., a:b].astype(BF16))
        width = -(-(b - a) // LANE) * LANE
        if width != b - a:
            parts.append(jnp.zeros(w_in.shape[:-1] + (width - (b - a),), BF16))
        cur += width
    if NP != cur:
        parts.append(jnp.zeros(w_in.shape[:-1] + (NP - cur,), BF16))
    return jnp.concatenate(parts, axis=-1)


def _cparams(*sem):
    return pltpu.CompilerParams(dimension_semantics=sem, vmem_limit_bytes=VMEM_LIMIT)


def _dot(a, b):
    return jnp.dot(a, b, preferred_element_type=F32)


def _dot_nt(a, b):
    return lax.dot_general(a, b, (((1,), (1,)), ((), ())), preferred_element_type=F32)


def _dot_tn(a, b):
    return lax.dot_general(a, b, (((0,), (0,)), ((), ())), preferred_element_type=F32)


def _bf(x):
    return x.astype(BF16)


def _split3(x):
    hi = x.astype(BF16)
    r1 = x - hi.astype(F32)
    mid = r1.astype(BF16)
    lo = (r1 - mid.astype(F32)).astype(BF16)
    return hi, mid, lo


def _cumsum_rows(tri, x):
    hi, mid, lo = _split3(x)
    return _dot(tri, hi) + _dot(tri, mid) + _dot(tri, lo)


def _cumsum_cols(x, triu):
    hi, mid, lo = _split3(x)
    return _dot(hi, triu) + _dot(mid, triu) + _dot(lo, triu)


def _iota2(shape, dim):
    return lax.broadcasted_iota(jnp.int32, shape, dim)


def _log_sigmoid(x):
    return jnp.minimum(x, 0.0) - jnp.log1p(jnp.exp(-jnp.abs(x)))


def _sigmoid(x):
    return 1.0 / (1.0 + jnp.exp(-x))


def _silu(x):
    return x * _sigmoid(x)


def _normproj_kernel(x_ref, g_ref, w_ref, o_ref, h_ref):
    @pl.when(pl.program_id(1) == 0)
    def _():
        x = x_ref[...]
        ms = jnp.mean(x * x, axis=-1, keepdims=True)
        h_ref[...] = ((x * lax.rsqrt(ms + EPS)) * g_ref[...]).astype(BF16)

    o_ref[...] = _dot(h_ref[...], w_ref[...])


def _normproj(x, g, w, tm, tn):
    M, K = x.shape
    N = w.shape[1]
    return pl.pallas_call(
        _normproj_kernel,
        grid=(M // tm, N // tn),
        in_specs=[pl.BlockSpec((tm, K), lambda i, j: (i, 0)),
                  pl.BlockSpec((1, K), lambda i, j: (0, 0)),
                  pl.BlockSpec((K, tn), lambda i, j: (0, j))],
        out_specs=pl.BlockSpec((tm, tn), lambda i, j: (i, j)),
        out_shape=jax.ShapeDtypeStruct((M, N), F32),
        scratch_shapes=[pltpu.VMEM((tm, K), BF16)],
        compiler_params=_cparams("parallel", "arbitrary"),
        name="normproj",
    )(x, g, w)


def _rwkv_kernel(xr_ref, xk_ref, xv_ref, xwa_ref, z_ref, sp_ref, mu_ref, w2_ref, a2_ref, w0_ref, a0_ref,
                 kkw_ref, ka_ref, rk_ref, lng_ref, s0_ref, og_ref, s_ref,
                 s_scr, phi_scr, psi_scr, qt_scr, egl_scr, carry_scr,
                 r_scr, lw_scr, k_scr, v_scr, kk_scr, b_scr, y_scr, *, L):
    t = pl.program_id(1)
    HG, D = RWKV_H, RWKV_D
    hs = range(HG)
    sls = [slice(h * D, (h + 1) * D) for h in hs]

    @pl.when(t == 0)
    def _():
        s_scr[...] = s0_ref[0]
        carry_scr[...] = sp_ref[0]

    def cat(parts):
        return jnp.concatenate(parts, axis=1)

    tt = xr_ref.shape[0]
    first = _iota2((tt, 1), 0) == 0

    def mix(x_ref, lo, hi):
        x = x_ref[...]
        prev = jnp.where(first, carry_scr[:, lo:hi], pltpu.roll(x, 1, 0))
        carry_scr[:, lo:hi] = x[tt - 1:tt, :]
        return x + (prev - x) * mu_ref[:, lo:hi]

    r_ = mix(xr_ref, 0, BW)
    k_ = mix(xk_ref, BW, 2 * BW)
    v_ = mix(xv_ref, 2 * BW, 3 * BW)
    wa = mix(xwa_ref, 3 * BW, RWKV_SHIFT_W)
    lw_scr[...] = -RWKV_DECAY_SCALE * _sigmoid(w0_ref[...] + _dot(_bf(jnp.tanh(wa)), w2_ref[...]))
    a_ = _sigmoid(a0_ref[...] + _dot(_bf(wa), a2_ref[...]))
    ones_blk = (_iota2((LANE, LANE), 0) // D == _iota2((LANE, LANE), 1) // D).astype(BF16)

    def headsum(x):
        hi = _bf(x)
        lo = _bf(x - hi.astype(F32))
        return cat([_dot(hi[:, j:j + LANE], ones_blk) + _dot(lo[:, j:j + LANE], ones_blk)
                    for j in range(0, BW, LANE)])

    kkf = k_ * kkw_ref[...]
    kk = kkf * lax.rsqrt(jnp.maximum(headsum(kkf * kkf), 1e-24))
    r_scr[...] = r_
    k_scr[...] = k_ * (1.0 + (a_ - 1.0) * ka_ref[...])
    v_scr[...] = v_
    kk_scr[...] = kk
    b_scr[...] = kk * a_

    nchunk = tt // L
    row, col = _iota2((L, L), 0), _iota2((L, L), 1)
    tri = (row >= col).astype(BF16)
    strict = row > col
    incl2 = _iota2((L, 2 * L), 0) >= _iota2((L, 2 * L), 1) % L
    eye = (row == col).astype(F32)

    U1 = 4 if nchunk % 4 == 0 else 1
    it = range(U1 * HG)
    isl = [sls[i % HG] for i in it]

    def phase1(ci, carry):
        cidx = [ci * U1 + u for u in range(U1)]
        rws = [pl.ds(pl.multiple_of(c * L, L), L) for c in cidx]
        kkp, rg, kn, bn, kend, bend, v_c = [], [], [], [], [], [], []
        for u, c in enumerate(cidx):
            lw, k_c, b_c = lw_scr[rws[u], :], k_scr[rws[u], :], b_scr[rws[u], :]
            g = _cumsum_rows(tri, lw)
            gl = g[L - 1:L, :]
            kkp.append(kk_scr[rws[u], :] * jnp.exp(g - lw))
            rg.append(r_scr[rws[u], :] * jnp.exp(g))
            eng = jnp.exp(-g)
            kn.append(k_c * eng)
            bn.append(b_c * eng)
            ee = jnp.exp(gl - g)
            kend.append(k_c * ee)
            bend.append(b_c * ee)
            v_c.append(v_scr[rws[u], :])
            egl_scr[c] = jnp.exp(gl)
        kkp_h = [kkp[i // HG][:, isl[i]] for i in it]
        rg_h = [rg[i // HG][:, isl[i]] for i in it]
        bk = [_bf(jnp.concatenate([bn[i // HG][:, isl[i]], kn[i // HG][:, isl[i]]], axis=0)) for i in it]
        kr = [_bf(jnp.concatenate([kkp_h[i], rg_h[i]], axis=0)) for i in it]
        mnaq = [_dot_nt(kr[i], bk[i]) for i in it]
        mn = [x[:L, :] for x in mnaq]
        aq = [x[L:, :] for x in mnaq]
        N = [_bf(jnp.where(strict, m[:, L:], 0.0)) for m in mn]
        vb = [_bf(v_c[i // HG][:, isl[i]]) for i in it]
        nv = [_dot(N[i], vb[i]) for i in it]
        X = [jnp.where(strict, -m[:, :L], 0.0) for m in mn]
        Xb = [_bf(x) for x in X]
        P = [_dot(xb, xb) for xb in Xb]
        tinv = [eye + x for x in X]
        n = 2
        while n < L:
            Pb = [_bf(p) for p in P]
            if 2 * n >= L:
                tinv = [tinv[i] + _dot(_bf(tinv[i]), Pb[i]) for i in it]
            else:
                st = [_dot(_bf(jnp.concatenate([P[i], tinv[i]], axis=0)), Pb[i]) for i in it]
                P = [x[:L, :] for x in st]
                tinv = [tinv[i] + st[i][L:, :] for i in it]
            n *= 2
        gh = [_dot(_bf(tinv[i]), _bf(jnp.concatenate([kkp_h[i], nv[i]], axis=1))) for i in it]
        gm = [_bf(x[:, :D]) for x in gh]
        hm = [_bf(x[:, D:]) for x in gh]
        bend_b = [_bf(bend[i // HG][:, isl[i]]) for i in it]
        aqm = [_bf(jnp.where(incl2, a, 0.0)) for a in aq]
        hv = [jnp.concatenate([-hm[i], vb[i]], axis=0) for i in it]
        bke = [jnp.concatenate([bend_b[i], _bf(kend[i // HG][:, isl[i]])], axis=0) for i in it]
        phi = [_dot_tn(gm[i], bend_b[i]) for i in it]
        psi = [_dot_tn(hv[i], bke[i]) for i in it]
        qts = [rg_h[i] - _dot(aqm[i][:, :L], gm[i]) for i in it]
        y0s = [_dot(aqm[i], hv[i]) for i in it]
        for i in it:
            phi_scr[cidx[i // HG], i % HG] = _bf(phi[i])
            psi_scr[cidx[i // HG], i % HG] = psi[i]
        for u in range(U1):
            qt_scr[rws[u], :] = _bf(cat(qts[u * HG:(u + 1) * HG]))
            y_scr[rws[u], :] = cat(y0s[u * HG:(u + 1) * HG])
        return carry

    lax.fori_loop(0, nchunk // U1, phase1, 0)

    def phase2(c, carry):
        rows = pl.ds(pl.multiple_of(c * L, L), L)
        qt = qt_scr[rows, :]
        egl = egl_scr[c]
        S = [s_scr[h] for h in hs]
        Sb = [_bf(s_) for s_ in S]
        sphi = [_dot(Sb[h], phi_scr[c, h]) for h in hs]
        ys = [_dot_nt(qt[:, sls[h]], Sb[h]) for h in hs]
        for h in hs:
            s_scr[h] = S[h] * egl[:, sls[h]] - sphi[h] + psi_scr[c, h]
        y_scr[rows, :] += cat(ys)
        return carry

    lax.fori_loop(0, nchunk, phase2, 0)

    y = y_scr[...]
    rk = r_scr[...] * k_scr[...] * rk_ref[...]
    cen = y - headsum(y) * (1.0 / D)
    ln = cen * lax.rsqrt(headsum(cen * cen) * (1.0 / D) + EPS)
    out = ln * lng_ref[...] + headsum(rk) * v_scr[...]
    og_ref[...] = _bf(out * _silu(z_ref[...]))

    @pl.when(t == pl.num_programs(1) - 1)
    def _():
        s_ref[0] = s_scr[...]


def _rwkv(P, B, T, p, s0, shift_prev):
    L = min(T, CHUNK)
    tt = min(T, RWKV_TT)
    nT = T // tt
    nc = tt // L
    off = _OFF['rwkv_shift']
    rowblk = lambda w, o: pl.BlockSpec((tt, w), lambda b_, t: (b_ * nT + t, o // w))
    const = lambda shape: pl.BlockSpec(shape, lambda b_, t: (0,) * len(shape))
    st = pl.BlockSpec((1, RWKV_H, RWKV_D, RWKV_D), lambda b_, t: (b_, 0, 0, 0))
    vec = const((1, BW))
    return pl.pallas_call(
        functools.partial(_rwkv_kernel, L=L),
        grid=(B, nT),
        in_specs=[rowblk(BW, off), rowblk(BW, off + BW), rowblk(BW, off + 2 * BW), rowblk(LANE, off + 3 * BW),
                  rowblk(BW, _OFF['rwkv_z']),
                  pl.BlockSpec((1, 1, RWKV_SHIFT_W), lambda b_, t: (b_, 0, 0)), const((1, RWKV_SHIFT_W)),
                  const((LANE, BW)), const((LANE, BW)), vec, vec, vec, vec, vec, vec, st],
        out_specs=[pl.BlockSpec((tt, BW), lambda b_, t: (b_ * nT + t, 0)), st],
        out_shape=[jax.ShapeDtypeStruct((B * T, BW), BF16), jax.ShapeDtypeStruct(s0.shape, F32)],
        scratch_shapes=[pltpu.VMEM((RWKV_H, RWKV_D, RWKV_D), F32),
                        pltpu.VMEM((nc, RWKV_H, RWKV_D, RWKV_D), BF16),
                        pltpu.VMEM((nc, RWKV_H, RWKV_D, RWKV_D), F32),
                        pltpu.VMEM((tt, BW), BF16),
                        pltpu.VMEM((nc, 1, BW), F32),
                        pltpu.VMEM((1, RWKV_SHIFT_W), F32)] + [pltpu.VMEM((tt, BW), F32)] * 7,
        compiler_params=_cparams("parallel", "arbitrary"),
        name="rwkv_scan",
    )(P, P, P, P, P, shift_prev, p['rwkv_mu'], p['rwkv_w2p'], p['rwkv_a2p'], p['rwkv_w0'], p['rwkv_a0'],
      p['rwkv_k_k'], p['rwkv_k_a'], p['rwkv_r_k'], p['rwkv_ln_g'], s0)


def _gla_kernel(q_ref, k_ref, v_ref, gd_ref, z_ref, g2_ref, gb_ref, ng_ref, s0_ref, o_ref, s_ref, s_scr, g_scr, *, L):
    t = pl.program_id(1)

    @pl.when(t == 0)
    def _():
        s_scr[...] = s0_ref[0]

    gk = _dot(_bf(gd_ref[...]), g2_ref[...]) + gb_ref[...]
    g_scr[...] = _log_sigmoid(gk) * (1.0 / GLA_GATE_NORM)

    nchunk = q_ref.shape[0] // L
    row, col = _iota2((L, L), 0), _iota2((L, L), 1)
    tri = (row >= col).astype(BF16)
    causal = row >= col
    scale = GLA_DK ** -0.5

    U = 4 if nchunk % 4 == 0 else 1
    hs, us = range(GLA_H), range(U)
    ksl = [slice(h * GLA_DK, (h + 1) * GLA_DK) for h in hs]
    vsl = [slice(h * GLA_DV, (h + 1) * GLA_DV) for h in hs]
    uh = [(u, h) for u in us for h in hs]

    def chunks(i, carry):
        rws = [pl.ds(pl.multiple_of((i * U + u) * L, L), L) for u in us]
        q = [q_ref[r, :] * scale for r in rws]
        k = [k_ref[r, :] for r in rws]
        b = [_cumsum_rows(tri, g_scr[r, :]) for r in rws]
        b_end = [x[L - 1:L, :] for x in b]
        qi = [_bf(q[u] * jnp.exp(b[u])) for u in us]
        ki = [_bf(k[u] * jnp.exp(-b[u])) for u in us]
        kend = [_bf(k[u] * jnp.exp(b_end[u] - b[u])) for u in us]
        eb_end = [jnp.exp(x) for x in b_end]
        vb = {(u, h): _bf(v_ref[rws[u], vsl[h]]) for u, h in uh}
        A = {(u, h): _bf(jnp.where(causal, _dot_nt(qi[u][:, ksl[h]], ki[u][:, ksl[h]]), 0.0)) for u, h in uh}
        kv = {(u, h): _dot_tn(vb[u, h], kend[u][:, ksl[h]]) for u, h in uh}
        av = {(u, h): _dot(A[u, h], vb[u, h]) for u, h in uh}
        St = [s_scr[h] for h in hs]
        o = {}
        for u in us:
            for h in hs:
                o[u, h] = av[u, h] + _dot_nt(qi[u][:, ksl[h]], _bf(St[h]))
            St = [St[h] * eb_end[u][:, ksl[h]] + kv[u, h] for h in hs]
        ms = {x: jnp.mean(o[x] * o[x], axis=-1, keepdims=True) for x in uh}
        for u, h in uh:
            o_ref[rws[u], vsl[h]] = _bf((o[u, h] * lax.rsqrt(ms[u, h] + EPS)) * ng_ref[:, vsl[h]]
                                        * _silu(z_ref[rws[u], vsl[h]]))
        for h in hs:
            s_scr[h] = St[h]
        return carry

    lax.fori_loop(0, nchunk // U, chunks, 0)

    @pl.when(t == pl.num_programs(1) - 1)
    def _():
        s_ref[0] = s_scr[...]


def _gla(P, B, T, g2p, gb, ng, s0t):
    L = min(T, CHUNK)
    tt = min(T, SCAN_TT)
    nT = T // tt
    rowblk = lambda w, off: pl.BlockSpec((tt, w), lambda b_, t: (b_ * nT + t, off // w))
    const = lambda shape: pl.BlockSpec(shape, lambda b_, t: (0,) * len(shape))
    st = pl.BlockSpec((1, GLA_H, GLA_DV, GLA_DK), lambda b_, t: (b_, 0, 0, 0))
    return pl.pallas_call(
        functools.partial(_gla_kernel, L=L),
        grid=(B, nT),
        in_specs=[rowblk(512, _OFF['gla_q']), rowblk(512, _OFF['gla_k']), rowblk(BW, _OFF['gla_v']),
                  rowblk(LANE, _OFF['gla_gd']), rowblk(BW, _OFF['gla_z']), const((LANE, 512)), const((1, 512)),
                  const((1, BW)), st],
        out_specs=[pl.BlockSpec((tt, BW), lambda b_, t: (b_ * nT + t, 0)), st],
        out_shape=[jax.ShapeDtypeStruct((B * T, BW), BF16), jax.ShapeDtypeStruct(s0t.shape, F32)],
        scratch_shapes=[pltpu.VMEM((GLA_H, GLA_DV, GLA_DK), F32), pltpu.VMEM((tt, 512), F32)],
        compiler_params=_cparams("parallel", "arbitrary"),
        name="gla_scan",
    )(P, P, P, P, P, g2p, gb, ng, s0t)


def _ret_kernel(q_ref, k_ref, v_ref, z_ref, cos_ref, sin_ref, s0_ref, o_ref, s_ref, s_scr, *, L):
    t = pl.program_id(1)

    @pl.when(t == 0)
    def _():
        s_scr[...] = s0_ref[0]

    nchunk = q_ref.shape[0] // L
    rel = (_iota2((L, L), 0) - _iota2((L, L), 1)).astype(F32)
    tcol = _iota2((L, 1), 0).astype(F32)
    scale = RET_DK ** -0.5
    lgs = [math.log(1.0 - 2.0 ** (-5.0 - h)) for h in range(RET_H)]
    decay = [jnp.where(rel >= 0, jnp.exp(lg * jnp.maximum(rel, 0.0)), 0.0) for lg in lgs]
    rowd = [jnp.exp(lg * (tcol + 1.0)) for lg in lgs]
    cold = [jnp.exp(lg * (L - 1.0 - tcol)) for lg in lgs]

    def rope(x, cos2, sin2):
        return x * cos2 + pltpu.roll(x, RET_DK // 2, 1) * sin2

    U = 4 if nchunk % 4 == 0 else 1
    hs, us = range(RET_H), range(U)
    ksl = [slice(h * RET_DK, (h + 1) * RET_DK) for h in hs]
    vsl = [slice(h * RET_DV, (h + 1) * RET_DV) for h in hs]
    uh = [(u, h) for u in us for h in hs]

    def chunks(i, carry):
        rws = [pl.ds(pl.multiple_of((i * U + u) * L, L), L) for u in us]
        cs = [(cos_ref[r, :], sin_ref[r, :]) for r in rws]
        k = {(u, h): rope(k_ref[rws[u], ksl[h]], *cs[u]) for u, h in uh}
        qb = {(u, h): _bf(rope(q_ref[rws[u], ksl[h]], *cs[u]) * scale) for u, h in uh}
        kb = {x: _bf(k[x]) for x in uh}
        kcb = {(u, h): _bf(k[u, h] * cold[h]) for u, h in uh}
        vb = {(u, h): _bf(v_ref[rws[u], vsl[h]]) for u, h in uh}
        A = {(u, h): _bf(_dot_nt(qb[u, h], kb[u, h]) * decay[h]) for u, h in uh}
        kv = {x: _dot_tn(kcb[x], vb[x]) for x in uh}
        av = {x: _dot(A[x], vb[x]) for x in uh}
        S = [s_scr[h] for h in hs]
        o = {}
        for u in us:
            for h in hs:
                o[u, h] = av[u, h] + rowd[h] * _dot(qb[u, h], _bf(S[h]))
            S = [math.exp(lgs[h] * L) * S[h] + kv[u, h] for h in hs]
        ms = {x: jnp.mean(o[x] * o[x], axis=-1, keepdims=True) for x in uh}
        for u, h in uh:
            o_ref[rws[u], vsl[h]] = _bf(o[u, h] * lax.rsqrt(ms[u, h] + EPS) * _silu(z_ref[rws[u], vsl[h]]))
        for h in hs:
            s_scr[h] = S[h]
        return carry

    lax.fori_loop(0, nchunk // U, chunks, 0)

    @pl.when(t == pl.num_programs(1) - 1)
    def _():
        s_ref[0] = s_scr[...]


def _ret(P, B, T, cos2, sin2, s0):
    L = min(T, CHUNK)
    tt = min(T, SCAN_TT)
    nT = T // tt
    rowblk = lambda w, off: pl.BlockSpec((tt, w), lambda b_, t: (b_ * nT + t, off // w))
    tab = pl.BlockSpec((tt, RET_DK), lambda b_, t: (t, 0))
    st = pl.BlockSpec((1, RET_H, RET_DK, RET_DV), lambda b_, t: (b_, 0, 0, 0))
    return pl.pallas_call(
        functools.partial(_ret_kernel, L=L),
        grid=(B, nT),
        in_specs=[rowblk(512, _OFF['ret_q']), rowblk(512, _OFF['ret_k']), rowblk(BW, _OFF['ret_v']),
                  rowblk(BW, _OFF['ret_z']), tab, tab, st],
        out_specs=[pl.BlockSpec((tt, BW), lambda b_, t: (b_ * nT + t, 0)), st],
        out_shape=[jax.ShapeDtypeStruct((B * T, BW), BF16), jax.ShapeDtypeStruct(s0.shape, F32)],
        scratch_shapes=[pltpu.VMEM((RET_H, RET_DK, RET_DV), F32)],
        compiler_params=_cparams("parallel", "arbitrary"),
        name="ret_scan",
    )(P, P, P, P, cos2, sin2, s0)


def _mlstm_pre_kernel(xm_ref, cp_ref, cw_ref, cb_ref, wq_ref, wk_ref, wv_ref, wif_ref, bif_ref,
                      xc_ref, q_ref, k_ref, v_ref, g_ref, carry_scr):
    t = pl.program_id(1)
    tm = xm_ref.shape[0]
    nprev = ML_CONV - 1

    @pl.when(t == 0)
    def _():
        carry_scr[...] = jnp.zeros_like(carry_scr)
        carry_scr[SUBLANE - nprev:SUBLANE, :] = cp_ref[0]

    xm = xm_ref[...]
    c8 = carry_scr[...]
    rid = _iota2((SUBLANE, 1), 0)
    conv = xm * cw_ref[nprev:nprev + 1, :] + cb_ref[...]
    for j in range(1, ML_CONV):
        rolled = pltpu.roll(xm, j, 0)
        head = jnp.where(rid < j, pltpu.roll(c8, j, 0), rolled[0:SUBLANE, :])
        prev = jnp.concatenate([head, rolled[SUBLANE:, :]], axis=0) if tm > SUBLANE else head
        conv = conv + prev * cw_ref[nprev - j:nprev - j + 1, :]
    carry_scr[...] = xm[tm - SUBLANE:tm, :]
    xc = _silu(conv)
    xc_ref[...] = xc
    xcb, xmb = _bf(xc), _bf(xm)
    q = _dot(xcb, wq_ref[...])
    k = _dot(xcb, wk_ref[...])
    v = _dot(xmb, wv_ref[...])
    g_ref[...] = (_dot(_bf(q), wif_ref[0]) + _dot(_bf(k), wif_ref[1]) + _dot(_bf(v), wif_ref[2])) + bif_ref[...]
    q_ref[...] = q
    k_ref[...] = k * (ML_D ** -0.5)
    v_ref[...] = v


def _mlstm_pre(P, B, T, conv_prev, p):
    tm = min(T, SCAN_TT)
    nT = T // tm
    row = pl.BlockSpec((tm, BW), lambda b_, t: (b_ * nT + t, 0))
    const = lambda shape: pl.BlockSpec(shape, lambda b_, t: (0,) * len(shape))
    M = B * T
    return pl.pallas_call(
        _mlstm_pre_kernel,
        grid=(B, nT),
        in_specs=[pl.BlockSpec((tm, BW), lambda b_, t: (b_ * nT + t, _OFF['ml_x'] // BW)),
                  pl.BlockSpec((1, ML_CONV - 1, BW), lambda b_, t: (b_, 0, 0)),
                  const((ML_CONV, BW)), const((1, BW)), const((BW, BW)), const((BW, BW)), const((BW, BW)),
                  const((3, BW, LANE)), const((1, LANE))],
        out_specs=[row, row, row, row, pl.BlockSpec((tm, LANE), lambda b_, t: (b_ * nT + t, 0))],
        out_shape=[jax.ShapeDtypeStruct((M, BW), F32)] * 4 + [jax.ShapeDtypeStruct((M, LANE), F32)],
        scratch_shapes=[pltpu.VMEM((SUBLANE, BW), F32)],
        compiler_params=_cparams("parallel", "arbitrary"),
        name="mlstm_pre",
    )(P, conv_prev, p['ml_conv_w'], p['ml_conv_b'], p['ml_wq'], p['ml_wk'], p['ml_wv'], p['ml_wif'], p['ml_bif'])


def _mlstm_kernel(q_ref, k_ref, v_ref, gc_ref, gr_ref, xc_ref, z_ref, ng_ref, sk_ref, c0_ref, n0_ref, m0_ref,
                  o_ref, c_ref, n_ref, m_ref, c_scr, n_scr, m_scr, *, L):
    t = pl.program_id(1)

    @pl.when(t == 0)
    def _():
        c_scr[...] = c0_ref[0]
        n_scr[...] = n0_ref[0]
        m_scr[...] = m0_ref[0]

    nchunk = q_ref.shape[0] // L
    row, col = _iota2((L, L), 0), _iota2((L, L), 1)
    tri = (row >= col).astype(BF16)
    triu = (row <= col).astype(BF16)
    causal = row >= col

    U = 2 if nchunk % 2 == 0 else 1
    hs, us = range(ML_H), range(U)
    sl = [slice(h * ML_D, (h + 1) * ML_D) for h in hs]
    uh = [(u, h) for u in us for h in hs]

    def chunks(i, carry):
        cidx = [i * U + u for u in us]
        rws = [pl.ds(pl.multiple_of(c * L, L), L) for c in cidx]
        gcb = [gc_ref[0, c] for c in cidx]
        grb = [gr_ref[0, c] for c in cidx]
        bc = [_cumsum_rows(tri, x) for x in gcb]
        br = [_cumsum_cols(x, triu) for x in grb]
        q = {(u, h): q_ref[rws[u], sl[h]] for u, h in uh}
        k = {(u, h): k_ref[rws[u], sl[h]] for u, h in uh}
        qb = {x: _bf(q[x]) for x in uh}
        vb = {(u, h): _bf(v_ref[rws[u], sl[h]]) for u, h in uh}
        qk = {x: _dot_nt(qb[x], _bf(k[x])) for x in uh}
        b_c = {(u, h): bc[u][:, ML_H + h:ML_H + h + 1] for u, h in uh}
        dlog = {(u, h): jnp.where(causal, b_c[u, h] - br[u][ML_H + h:ML_H + h + 1, :] + grb[u][h:h + 1, :], -jnp.inf)
                for u, h in uh}
        m_loc = {x: jnp.max(dlog[x], axis=-1, keepdims=True) for x in uh}
        s_loc = {x: qk[x] * jnp.exp(dlog[x] - m_loc[x]) for x in uh}
        rs = {x: jnp.sum(s_loc[x], axis=-1, keepdims=True) for x in uh}
        b_end = {x: b_c[x][L - 1:L, :] for x in uh}
        m_le = {x: m_loc[x][L - 1:L, :] for x in uh}
        kw = {(u, h): k[u, h] * jnp.exp(b_end[u, h] - b_c[u, h] + gcb[u][:, h:h + 1] - m_le[u, h]) for u, h in uh}
        sv = {x: _dot(_bf(s_loc[x]), vb[x]) for x in uh}
        kv = {x: _dot_tn(_bf(kw[x]), vb[x]) for x in uh}
        ksum = {x: jnp.sum(kw[x], axis=0, keepdims=True) for x in uh}
        C = [c_scr[h] for h in hs]
        nvec = [n_scr[h] for h in hs]
        m_prev = [m_scr[h][:, 0:1] for h in hs]
        hh = {}
        for u in us:
            qc = [_dot(qb[u, h], _bf(C[h])) for h in hs]
            qn = [jnp.sum(q[u, h] * nvec[h], axis=-1, keepdims=True) for h in hs]
            from_state = [b_c[u, h] + m_prev[h] for h in hs]
            m_t = [jnp.maximum(from_state[h], m_loc[u, h]) for h in hs]
            a_in = [jnp.exp(m_loc[u, h] - m_t[h]) for h in hs]
            w_state = [jnp.exp(from_state[h] - m_t[h]) for h in hs]
            den = [a_in[h] * rs[u, h] + w_state[h] * qn[h] for h in hs]
            for h in hs:
                hh[u, h] = ((a_in[h] * sv[u, h] + w_state[h] * qc[h])
                            / jnp.maximum(jnp.abs(den[h]), jnp.exp(-m_t[h])))
            m_new = [m_t[h][L - 1:L, :] for h in hs]
            cd = [jnp.exp(b_end[u, h] + m_prev[h] - m_new[h]) for h in hs]
            sc = [jnp.exp(m_le[u, h] - m_new[h]) for h in hs]
            C = [cd[h] * C[h] + sc[h] * kv[u, h] for h in hs]
            nvec = [cd[h] * nvec[h] + sc[h] * ksum[u, h] for h in hs]
            m_prev = m_new
        mean = {x: jnp.mean(hh[x], axis=-1, keepdims=True) for x in uh}
        cen = {x: hh[x] - mean[x] for x in uh}
        var = {x: jnp.mean(cen[x] * cen[x], axis=-1, keepdims=True) for x in uh}
        for u, h in uh:
            y = cen[u, h] * lax.rsqrt(var[u, h] + EPS)
            o_ref[rws[u], sl[h]] = _bf((y * ng_ref[:, sl[h]] + sk_ref[:, sl[h]] * xc_ref[rws[u], sl[h]])
                                       * _silu(z_ref[rws[u], sl[h]]))
        for h in hs:
            c_scr[h] = C[h]
            n_scr[h] = nvec[h]
            m_scr[h] = jnp.broadcast_to(m_prev[h], (1, LANE))
        return carry

    lax.fori_loop(0, nchunk // U, chunks, 0)

    @pl.when(t == pl.num_programs(1) - 1)
    def _():
        c_ref[0] = c_scr[...]
        n_ref[0] = n_scr[...]
        m_ref[0] = m_scr[...]


def _mlstm(q, k, v, gc, gr, xc, P, ng, sk, c0, n0, m0, B, T):
    L = min(T, CHUNK)
    tt = min(T, SCAN_TT)
    nT = T // tt
    nc = tt // L
    row = pl.BlockSpec((tt, BW), lambda b_, t: (b_ * nT + t, 0))
    const = pl.BlockSpec((1, BW), lambda b_, t: (0, 0))
    stc = pl.BlockSpec((1, ML_H, ML_D, ML_D), lambda b_, t: (b_, 0, 0, 0))
    stn = pl.BlockSpec((1, ML_H, 1, ML_D), lambda b_, t: (b_, 0, 0, 0))
    stm = pl.BlockSpec((1, ML_H, 1, LANE), lambda b_, t: (b_, 0, 0, 0))
    return pl.pallas_call(
        functools.partial(_mlstm_kernel, L=L),
        grid=(B, nT),
        in_specs=[row, row, row,
                  pl.BlockSpec((1, nc, L, 8), lambda b_, t: (b_, t, 0, 0)),
                  pl.BlockSpec((1, nc, 8, L), lambda b_, t: (b_, t, 0, 0)),
                  row, pl.BlockSpec((tt, BW), lambda b_, t: (b_ * nT + t, _OFF['ml_z'] // BW)), const, const, stc, stn, stm],
        out_specs=[row, stc, stn, stm],
        out_shape=[jax.ShapeDtypeStruct((B * T, BW), BF16), jax.ShapeDtypeStruct(c0.shape, F32),
                   jax.ShapeDtypeStruct(n0.shape, F32), jax.ShapeDtypeStruct(m0.shape, F32)],
        scratch_shapes=[pltpu.VMEM((ML_H, ML_D, ML_D), F32), pltpu.VMEM((ML_H, 1, ML_D), F32),
                        pltpu.VMEM((ML_H, 1, LANE), F32)],
        compiler_params=_cparams("parallel", "arbitrary"),
        name="mlstm_scan",
    )(q, k, v, gc, gr, xc, P, ng, sk, c0, n0, m0)


def _xattn_kernel(q_ref, z_ref, mk_ref, mv_ref, o_ref):
    hs = range(XA_H)
    sl = [slice(h * XA_D, (h + 1) * XA_D) for h in hs]
    s = [_dot_nt(_bf(q_ref[:, x]), _bf(mk_ref[0, :, x])) * (XA_D ** -0.5) for x in sl]
    e = [jnp.exp(x - jnp.max(x, axis=-1, keepdims=True)) for x in s]
    prob = [_bf(x / jnp.sum(x, axis=-1, keepdims=True)) for x in e]
    o = [_dot(prob[h], _bf(mv_ref[0, :, sl[h]])) for h in hs]
    for h in hs:
        o_ref[:, sl[h]] = _bf(o[h] * _silu(z_ref[:, sl[h]]))


def _xattn(P, B, T, mk, mv):
    tt = min(T, SCAN_TT)
    nT = T // tt
    n_mem = mk.shape[1]
    mem = pl.BlockSpec((1, n_mem, BW), lambda b_, t: (b_, 0, 0))
    return pl.pallas_call(
        _xattn_kernel,
        grid=(B, nT),
        in_specs=[pl.BlockSpec((tt, BW), lambda b_, t: (b_ * nT + t, _OFF['xa_q'] // BW)),
                  pl.BlockSpec((tt, BW), lambda b_, t: (b_ * nT + t, _OFF['xa_z'] // BW)), mem, mem],
        out_specs=pl.BlockSpec((tt, BW), lambda b_, t: (b_ * nT + t, 0)),
        out_shape=jax.ShapeDtypeStruct((B * T, BW), BF16),
        compiler_params=_cparams("parallel", "parallel"),
        name="xattn",
    )(P, P, mk, mv)


def _merge_kernel(o0, o1, o2, o3, o4, g_ref, w_ref, out_ref):
    ys = [_dot(o_ref[...], w_ref[i]) for i, o_ref in enumerate((o0, o1, o2, o3, o4))]
    acc = None
    for i, y in enumerate(ys):
        term = _sigmoid(g_ref[:, i * D_MODEL:(i + 1) * D_MODEL]) * y
        acc = term if acc is None else acc + term
    out_ref[...] = _bf(acc)


def _merge(ogs, P, wb, tm):
    M = ogs[0].shape[0]
    gw = N_BRANCH * D_MODEL
    og = pl.BlockSpec((tm, BW), lambda i: (i, 0))
    return pl.pallas_call(
        _merge_kernel,
        grid=(M // tm,),
        in_specs=[og] * N_BRANCH + [pl.BlockSpec((tm, gw), lambda i: (i, _OFF['gates'] // gw)),
                                    pl.BlockSpec((N_BRANCH, BW, D_MODEL), lambda i: (0, 0, 0),
                                                 pipeline_mode=pl.Buffered(1))],
        out_specs=pl.BlockSpec((tm, D_MODEL), lambda i: (i, 0)),
        out_shape=jax.ShapeDtypeStruct((M, D_MODEL), BF16),
        compiler_params=_cparams("parallel"),
        name="merge",
    )(*ogs, P, wb)


def _resid_kernel(x_ref, m_ref, w_ref, o_ref):
    o_ref[...] = x_ref[...] + _dot(m_ref[...], w_ref[...])


def _resid_norm_kernel(x_ref, m_ref, w_ref, g_ref, o_ref):
    y = x_ref[...] + _dot(m_ref[...], w_ref[...])
    ms = jnp.mean(y * y, axis=-1, keepdims=True)
    o_ref[...] = (y * lax.rsqrt(ms + EPS)) * g_ref[...]


def _resid_out(x, merged, w, tm, final_g=None):
    M = x.shape[0]
    row = pl.BlockSpec((tm, D_MODEL), lambda i: (i, 0))
    in_specs = [row, row, pl.BlockSpec((D_MODEL, D_MODEL), lambda i: (0, 0), pipeline_mode=pl.Buffered(1))]
    args = [x, merged, w]
    if final_g is not None:
        in_specs.append(pl.BlockSpec((1, D_MODEL), lambda i: (0, 0)))
        args.append(final_g)
    return pl.pallas_call(
        _resid_kernel if final_g is None else _resid_norm_kernel,
        grid=(M // tm,),
        in_specs=in_specs,
        out_specs=row,
        out_shape=jax.ShapeDtypeStruct((M, D_MODEL), F32),
        compiler_params=_cparams("parallel"),
        name="resid_out",
    )(*args)


def _blockdiag(w):
    depth, n, c, d = w.shape
    rows = w.reshape(depth, n * c, d)
    tile = (jnp.arange(d)[:, None] == jnp.arange(n * d)[None, :] % d).astype(w.dtype)
    dense = jnp.einsum('lrd,dj->lrj', rows, tile)
    mask = (jnp.arange(n * c)[:, None] // c) == (jnp.arange(n * d)[None, :] // d)
    return jnp.where(mask, dense, 0.0).astype(BF16)


def _rope_tables(start, T):
    half = RET_DK // 2
    inv = ROPE_BASE ** (-jnp.linspace(0.0, 1.0, half, dtype=F32))
    blk = ROPE_BLK
    if T % blk or T <= blk:
        ang = (start + jnp.arange(T, dtype=F32))[:, None] * inv[None, :]
        cos, sin = jnp.cos(ang), jnp.sin(ang)
    else:
        hi = (start + blk * jnp.arange(T // blk, dtype=F32))[:, None] * inv[None, :]
        lo = jnp.arange(blk, dtype=F32)[:, None] * inv[None, :]
        ch, sh, cl, sl_ = jnp.cos(hi)[:, None, :], jnp.sin(hi)[:, None, :], jnp.cos(lo)[None], jnp.sin(lo)[None]
        cos = (ch * cl - sh * sl_).reshape(T, half)
        sin = (sh * cl + ch * sl_).reshape(T, half)
    return jnp.concatenate([cos, cos], axis=-1), jnp.concatenate([-sin, sin], axis=-1)


def _layer(x, B, T, tabs, st, mem_k, mem_v, p, final_g=None):
    M = B * T
    tm = min(M, PROJ_TM)
    P = _normproj(x, p['norm_g'], p['w_pack'], tm, PROJ_TN)
    P3 = P.reshape(B, T, NP)

    og_a, s_rwkv = _rwkv(P, B, T, p, st['rwkv'], st['rwkv_shift'])
    shift_new = P3[:, T - 1:, _OFF['rwkv_shift']:_OFF['rwkv_shift'] + RWKV_SHIFT_W]

    og_b, s_gla_t = _gla(P, B, T, p['gla_g2p'], p['gla_gb'], p['gla_norm_g'], jnp.swapaxes(st['gla'], -1, -2))
    s_gla = jnp.swapaxes(s_gla_t, -1, -2)

    og_c, s_ret = _ret(P, B, T, tabs[0], tabs[1], st['ret'])

    xc, q_m, k_m, v_m, gates = _mlstm_pre(P, B, T, st['ml_conv'], p)
    L = min(T, CHUNK)
    gcol = jnp.concatenate([gates[:, :ML_H], _log_sigmoid(gates[:, ML_H:2 * ML_H])], axis=-1).reshape(B, T // L, L, 2 * ML_H)
    grow = jnp.swapaxes(gcol, -1, -2)
    og_d, c_new, n_new, m_new = _mlstm(
        q_m, k_m, v_m, gcol, grow, xc, P, p['ml_norm_g'], p['ml_skip'], st['ml_c'], st['ml_n'][:, :, None, :],
        jnp.broadcast_to(st['ml_m'][:, :, None, None], st['ml_m'].shape + (1, LANE)), B, T)
    xm_tail = P3[:, max(T - (ML_CONV - 1), 0):, _OFF['ml_x']:_OFF['ml_x'] + BW]
    conv_new = jnp.concatenate([st['ml_conv'], xm_tail], axis=1)[:, -(ML_CONV - 1):]

    og_x = _xattn(P, B, T, mem_k.reshape(B, -1, BW), mem_v.reshape(B, -1, BW))

    merged = _merge([og_a, og_b, og_c, og_d, og_x], P, p['w_branch'], min(M, MERGE_TM))
    x_new = _resid_out(x, merged, p['w_out'], min(M, RESID_TM), final_g)
    new = {'rwkv': s_rwkv, 'rwkv_shift': shift_new, 'gla': s_gla, 'ret': s_ret, 'ml_c': c_new,
           'ml_n': n_new[:, :, 0, :], 'ml_m': m_new[:, :, 0, 0], 'ml_conv': conv_new}
    return x_new, new


_STATE_KEYS = ('rwkv', 'rwkv_shift', 'gla', 'ret', 'ml_c', 'ml_n', 'ml_m', 'ml_conv')


def kernel(x_prompt, x_sample, mem_prompt, cache_mem_k, cache_mem_v, state_rwkv, state_rwkv_shift, state_gla, state_ret, state_mlstm_c, state_mlstm_n, state_mlstm_m, state_mlstm_conv, norm_g, mem_norm_g, w_in, w_mem_kv, rwkv_mu, rwkv_w0, rwkv_w2, rwkv_a0, rwkv_a2, rwkv_k_k, rwkv_k_a, rwkv_r_k, rwkv_ln_g, gla_g2, gla_gb, gla_norm_g, ml_conv_w, ml_conv_b, ml_wq, ml_wk, ml_wv, ml_w_if, ml_b_if, ml_skip, ml_norm_g, w_branch, w_out, final_norm_g):
    Bp, Tp, D = x_prompt.shape
    Bs, Ts, _ = x_sample.shape
    depth = w_in.shape[0]
    n_mem = mem_prompt.shape[1]

    params = {
        'norm_g': norm_g[:, None, :], 'mem_norm_g': mem_norm_g[:, None, :],
        'w_mem_kv': _bf(w_mem_kv),
        'rwkv_mu': rwkv_mu[:, None, :], 'rwkv_w0': rwkv_w0[:, None, :], 'rwkv_a0': rwkv_a0[:, None, :],
        'rwkv_w2p': _bf(jnp.pad(rwkv_w2, ((0, 0), (0, LANE - RWKV_LORA), (0, 0)))),
        'rwkv_a2p': _bf(jnp.pad(rwkv_a2, ((0, 0), (LANE - RWKV_LORA, 0), (0, 0)))),
        'rwkv_k_k': rwkv_k_k[:, None, :], 'rwkv_k_a': rwkv_k_a[:, None, :], 'rwkv_r_k': rwkv_r_k[:, None, :],
        'rwkv_ln_g': rwkv_ln_g[:, None, :],
        'gla_g2p': _bf(jnp.pad(gla_g2, ((0, 0), (0, LANE - GLA_LORA), (0, 0)))), 'gla_gb': gla_gb[:, None, :],
        'gla_norm_g': gla_norm_g[:, None, :],
        'ml_conv_w': ml_conv_w, 'ml_conv_b': ml_conv_b[:, None, :],
        'ml_wq': _blockdiag(ml_wq), 'ml_wk': _blockdiag(ml_wk), 'ml_wv': _blockdiag(ml_wv),
        'ml_wif': _bf(jnp.pad(ml_w_if.reshape(depth, 3, BW, 2 * ML_H), ((0, 0), (0, 0), (0, 0), (0, LANE - 2 * ML_H)))),
        'ml_bif': jnp.pad(ml_b_if, ((0, 0), (0, LANE - 2 * ML_H)))[:, None, :],
        'ml_skip': ml_skip[:, None, :], 'ml_norm_g': ml_norm_g[:, None, :],
        'w_branch': _bf(w_branch), 'w_out': _bf(w_out),
    }
    cache = {'rwkv': state_rwkv, 'rwkv_shift': state_rwkv_shift, 'gla': state_gla, 'ret': state_ret,
             'ml_c': state_mlstm_c, 'ml_n': state_mlstm_n, 'ml_m': state_mlstm_m, 'ml_conv': state_mlstm_conv,
             'mem_k': cache_mem_k, 'mem_v': cache_mem_v}
    tabs_p = _rope_tables(0.0, Tp)
    tabs_s = _rope_tables(float(PAST_LEN), Ts)
    zero_p = {
        'rwkv': jnp.zeros((Bp, RWKV_H, RWKV_D, RWKV_D), F32), 'rwkv_shift': jnp.zeros((Bp, 1, RWKV_SHIFT_W), F32),
        'gla': jnp.zeros((Bp, GLA_H, GLA_DK, GLA_DV), F32), 'ret': jnp.zeros((Bp, RET_H, RET_DK, RET_DV), F32),
        'ml_c': jnp.zeros((Bp, ML_H, ML_D, ML_D), F32), 'ml_n': jnp.zeros((Bp, ML_H, ML_D), F32),
        'ml_m': jnp.zeros((Bp, ML_H), F32), 'ml_conv': jnp.zeros((Bp, ML_CONV - 1, BW), F32),
    }
    mem2d = mem_prompt.reshape(Bp * n_mem, D)

    yp, ys = x_prompt.reshape(Bp * Tp, D), x_sample.reshape(Bs * Ts, D)
    outs = []
    for l in range(depth):
        p = {nm: arr[l] for nm, arr in params.items()}
        p['w_pack'] = _pack_w_in(w_in[l])
        kv = _normproj(mem2d, p['mem_norm_g'], p['w_mem_kv'], min(Bp * n_mem, MEM_TM), MEM_TN)
        mk_l = kv[:, :BW].reshape(Bp, n_mem, XA_H, XA_D)
        mv_l = kv[:, BW:].reshape(Bp, n_mem, XA_H, XA_D)
        fg = final_norm_g[None, :] if l == depth - 1 else None
        yp, stp_l = _layer(yp, Bp, Tp, tabs_p, zero_p, mk_l, mv_l, p, fg)
        ys, sts_l = _layer(ys, Bs, Ts, tabs_s, {nm: cache[nm][l] for nm in _STATE_KEYS},
                           cache['mem_k'][l], cache['mem_v'][l], p, fg)
        outs.append((stp_l, mk_l, mv_l, sts_l))
    stp = {nm: jnp.stack([o[0][nm] for o in outs]) for nm in _STATE_KEYS}
    sts = {nm: jnp.stack([o[3][nm] for o in outs]) for nm in _STATE_KEYS}
    mk = jnp.stack([o[1] for o in outs])
    mv = jnp.stack([o[2] for o in outs])
    y_prompt = yp.reshape(Bp, Tp, D)
    y_sample = ys.reshape(Bs, Ts, D)
    return (y_prompt, y_sample,
            stp['rwkv'], stp['rwkv_shift'], stp['gla'], stp['ret'], stp['ml_c'], stp['ml_n'], stp['ml_m'],
            stp['ml_conv'], mk, mv,
            sts['rwkv'], sts['rwkv_shift'], sts['gla'], sts['ret'], sts['ml_c'], sts['ml_n'], sts['ml_m'],
            sts['ml_conv'])
```

```python
import functools
import math

import jax
import jax.numpy as jnp
from jax import lax
from jax.experimental import pallas as pl
from jax.experimental.pallas import tpu as pltpu

F32 = jnp.float32
BF16 = jnp.bfloat16

D_MODEL = 2048
BW = 1024
EPS = 1e-6
CHUNK = 64
N_BRANCH = 5
PAST_LEN = 2048
RWKV_H, RWKV_D, RWKV_LORA = 16, 64, 64
RWKV_SHIFT_W = 3 * BW + 2 * RWKV_LORA
RWKV_DECAY_SCALE = 0.606531
GLA_H, GLA_DK, GLA_DV, GLA_LORA = 4, 128, 256, 16
GLA_GATE_NORM = 16.0
RET_H, RET_DK, RET_DV = 4, 128, 256
ROPE_BASE = 10000.0
ML_H, ML_D, ML_CONV = 4, 256, 4
XA_H, XA_D = 4, 256
LANE = 128
SUBLANE = 8
VMEM_LIMIT = 56 * 1024 * 1024

PROJ_TM = 1024
MEM_TM, MEM_TN = 512, 512
RWKV_TT = 256
SCAN_TT = 512
MERGE_TM = 256
RESID_TM = 512
ROPE_BLK = 128

_IN_LAYOUT = (
    ('rwkv_shift', RWKV_SHIFT_W), ('rwkv_z', BW), ('gla_q', 512), ('gla_k', 512), ('gla_v', BW),
    ('gla_gd', GLA_LORA), ('gla_z', BW), ('ret_q', 512), ('ret_k', 512), ('ret_v', BW), ('ret_z', BW),
    ('ml_x', BW), ('ml_z', BW), ('xa_q', BW), ('xa_z', BW), ('gates', N_BRANCH * D_MODEL),
)
_PACK_ORDER = ('gates', 'rwkv_z', 'gla_z', 'ret_z', 'ml_z', 'xa_z', 'ret_q', 'ret_k', 'ret_v',
               'gla_q', 'gla_k', 'gla_v', 'ml_x', 'xa_q', 'rwkv_shift', 'gla_gd')
PROJ_TN = 1792


def _src_cols(name):
    start = 0
    for nm, size in _IN_LAYOUT:
        if nm == name:
            return start, start + size
        start += size
    raise KeyError(name)


def _pack_offsets():
    off, cur = {}, 0
    for nm in _PACK_ORDER:
        a, b = _src_cols(nm)
        off[nm] = cur
        cur += -(-(b - a) // LANE) * LANE
    total = -(-cur // PROJ_TN) * PROJ_TN
    return off, total


_OFF, NP = _pack_offsets()


def _pack_w_in(w_in):
    parts, cur = [], 0
    for nm in _PACK_ORDER:
        a, b = _src_cols(nm)
        parts.append(w_in[..., a:b].astype(BF16))
        width = -(-(b - a) // LANE) * LANE
        if width != b - a:
            parts.append(jnp.zeros(w_in.shape[:-1] + (width - (b - a),), BF16))
        cur += width
    if NP != cur:
        parts.append(jnp.zeros(w_in.shape[:-1] + (NP - cur,), BF16))
    return jnp.concatenate(parts, axis=-1)


def _cparams(*sem):
    return pltpu.CompilerParams(dimension_semantics=sem, vmem_limit_bytes=VMEM_LIMIT)


def _dot(a, b):
    return jnp.dot(a, b, preferred_element_type=F32)


def _dot_nt(a, b):
    return lax.dot_general(a, b, (((1,), (1,)), ((), ())), preferred_element_type=F32)


def _dot_tn(a, b):
    return lax.dot_general(a, b, (((0,), (0,)), ((), ())), preferred_element_type=F32)


def _bf(x):
    return x.astype(BF16)


def _split3(x):
    hi = x.astype(BF16)
    r1 = x - hi.astype(F32)
    mid = r1.astype(BF16)
    lo = (r1 - mid.astype(F32)).astype(BF16)
    return hi, mid, lo


def _cumsum_rows(tri, x):
    hi, mid, lo = _split3(x)
    return _dot(tri, hi) + _dot(tri, mid) + _dot(tri, lo)


def _cumsum_cols(x, triu):
    hi, mid, lo = _split3(x)
    return _dot(hi, triu) + _dot(mid, triu) + _dot(lo, triu)


def _iota2(shape, dim):
    return lax.broadcasted_iota(jnp.int32, shape, dim)


def _log_sigmoid(x):
    return jnp.minimum(x, 0.0) - jnp.log1p(jnp.exp(-jnp.abs(x)))


def _sigmoid(x):
    return 1.0 / (1.0 + jnp.exp(-x))


def _silu(x):
    return x * _sigmoid(x)


def _normproj_kernel(x_ref, g_ref, w_ref, o_ref, h_ref):
    @pl.when(pl.program_id(1) == 0)
    def _():
        x = x_ref[...]
        ms = jnp.mean(x * x, axis=-1, keepdims=True)
        h_ref[...] = ((x * lax.rsqrt(ms + EPS)) * g_ref[...]).astype(BF16)

    o_ref[...] = _dot(h_ref[...], w_ref[...])


def _normproj(x, g, w, tm, tn):
    M, K = x.shape
    N = w.shape[1]
    return pl.pallas_call(
        _normproj_kernel,
        grid=(M // tm, N // tn),
        in_specs=[pl.BlockSpec((tm, K), lambda i, j: (i, 0)),
                  pl.BlockSpec((1, K), lambda i, j: (0, 0)),
                  pl.BlockSpec((K, tn), lambda i, j: (0, j))],
        out_specs=pl.BlockSpec((tm, tn), lambda i, j: (i, j)),
        out_shape=jax.ShapeDtypeStruct((M, N), F32),
        scratch_shapes=[pltpu.VMEM((tm, K), BF16)],
        compiler_params=_cparams("parallel", "arbitrary"),
        name="normproj",
    )(x, g, w)


def _rwkv_kernel(xr_ref, xk_ref, xv_ref, xwa_ref, z_ref, sp_ref, mu_ref, w2_ref, a2_ref, w0_ref, a0_ref,
                 kkw_ref, ka_ref, rk_ref, lng_ref, s0_ref, og_ref, s_ref,
                 s_scr, phi_scr, psi_scr, qt_scr, egl_scr, carry_scr,
                 r_scr, lw_scr, k_scr, v_scr, kk_scr, b_scr, y_scr, *, L):
    t = pl.program_id(1)
    HG, D = RWKV_H, RWKV_D
    hs = range(HG)
    sls = [slice(h * D, (h + 1) * D) for h in hs]

    @pl.when(t == 0)
    def _():
        s_scr[...] = s0_ref[0]
        carry_scr[...] = sp_ref[0]

    def cat(parts):
        return jnp.concatenate(parts, axis=1)

    tt = xr_ref.shape[0]
    first = _iota2((tt, 1), 0) == 0

    def mix(x_ref, lo, hi):
        x = x_ref[...]
        prev = jnp.where(first, carry_scr[:, lo:hi], pltpu.roll(x, 1, 0))
        carry_scr[:, lo:hi] = x[tt - 1:tt, :]
        return x + (prev - x) * mu_ref[:, lo:hi]

    r_ = mix(xr_ref, 0, BW)
    k_ = mix(xk_ref, BW, 2 * BW)
    v_ = mix(xv_ref, 2 * BW, 3 * BW)
    wa = mix(xwa_ref, 3 * BW, RWKV_SHIFT_W)
    lw_scr[...] = -RWKV_DECAY_SCALE * _sigmoid(w0_ref[...] + _dot(_bf(jnp.tanh(wa)), w2_ref[...]))
    a_ = _sigmoid(a0_ref[...] + _dot(_bf(wa), a2_ref[...]))
    ones_blk = (_iota2((LANE, LANE), 0) // D == _iota2((LANE, LANE), 1) // D).astype(BF16)

    def headsum(x):
        hi = _bf(x)
        lo = _bf(x - hi.astype(F32))
        return cat([_dot(hi[:, j:j + LANE], ones_blk) + _dot(lo[:, j:j + LANE], ones_blk)
                    for j in range(0, BW, LANE)])

    kkf = k_ * kkw_ref[...]
    kk = kkf * lax.rsqrt(jnp.maximum(headsum(kkf * kkf), 1e-24))
    r_scr[...] = r_
    k_scr[...] = k_ * (1.0 + (a_ - 1.0) * ka_ref[...])
    v_scr[...] = v_
    kk_scr[...] = kk
    b_scr[...] = kk * a_

    nchunk = tt // L
    row, col = _iota2((L, L), 0), _iota2((L, L), 1)
    tri = (row >= col).astype(BF16)
    strict = row > col
    incl2 = _iota2((L, 2 * L), 0) >= _iota2((L, 2 * L), 1) % L
    eye = (row == col).astype(F32)

    U1 = 4 if nchunk % 4 == 0 else 1
    it = range(U1 * HG)
    isl = [sls[i % HG] for i in it]

    def phase1(ci, carry):
        cidx = [ci * U1 + u for u in range(U1)]
        rws = [pl.ds(pl.multiple_of(c * L, L), L) for c in cidx]
        kkp, rg, kn, bn, kend, bend, v_c = [], [], [], [], [], [], []
        for u, c in enumerate(cidx):
            lw, k_c, b_c = lw_scr[rws[u], :], k_scr[rws[u], :], b_scr[rws[u], :]
            g = _cumsum_rows(tri, lw)
            gl = g[L - 1:L, :]
            kkp.append(kk_scr[rws[u], :] * jnp.exp(g - lw))
            rg.append(r_scr[rws[u], :] * jnp.exp(g))
            eng = jnp.exp(-g)
            kn.append(k_c * eng)
            bn.append(b_c * eng)
            ee = jnp.exp(gl - g)
            kend.append(k_c * ee)
            bend.append(b_c * ee)
            v_c.append(v_scr[rws[u], :])
            egl_scr[c] = jnp.exp(gl)
        kkp_h = [kkp[i // HG][:, isl[i]] for i in it]
        rg_h = [rg[i // HG][:, isl[i]] for i in it]
        bk = [_bf(jnp.concatenate([bn[i // HG][:, isl[i]], kn[i // HG][:, isl[i]]], axis=0)) for i in it]
        kr = [_bf(jnp.concatenate([kkp_h[i], rg_h[i]], axis=0)) for i in it]
        mnaq = [_dot_nt(kr[i], bk[i]) for i in it]
        mn = [x[:L, :] for x in mnaq]
        aq = [x[L:, :] for x in mnaq]
        N = [_bf(jnp.where(strict, m[:, L:], 0.0)) for m in mn]
        vb = [_bf(v_c[i // HG][:, isl[i]]) for i in it]
        nv = [_dot(N[i], vb[i]) for i in it]
        X = [jnp.where(strict, -m[:, :L], 0.0) for m in mn]
        Xb = [_bf(x) for x in X]
        P = [_dot(xb, xb) for xb in Xb]
        tinv = [eye + x for x in X]
        n = 2
        while n < L:
            Pb = [_bf(p) for p in P]
            if 2 * n >= L:
                tinv = [tinv[i] + _dot(_bf(tinv[i]), Pb[i]) for i in it]
            else:
                st = [_dot(_bf(jnp.concatenate([P[i], tinv[i]], axis=0)), Pb[i]) for i in it]
                P = [x[:L, :] for x in st]
                tinv = [tinv[i] + st[i][L:, :] for i in it]
            n *= 2
        gh = [_dot(_bf(tinv[i]), _bf(jnp.concatenate([kkp_h[i], nv[i]], axis=1))) for i in it]
        gm = [_bf(x[:, :D]) for x in gh]
        hm = [_bf(x[:, D:]) for x in gh]
        bend_b = [_bf(bend[i // HG][:, isl[i]]) for i in it]
        aqm = [_bf(jnp.where(incl2, a, 0.0)) for a in aq]
        hv = [jnp.concatenate([-hm[i], vb[i]], axis=0) for i in it]
        bke = [jnp.concatenate([bend_b[i], _bf(kend[i // HG][:, isl[i]])], axis=0) for i in it]
        phi = [_dot_tn(gm[i], bend_b[i]) for i in it]
        psi = [_dot_tn(hv[i], bke[i]) for i in it]
        qts = [rg_h[i] - _dot(aqm[i][:, :L], gm[i]) for i in it]
        y0s = [_dot(aqm[i], hv[i]) for i in it]
        for i in it:
            phi_scr[cidx[i // HG], i % HG] = _bf(phi[i])
            psi_scr[cidx[i // HG], i % HG] = psi[i]
        for u in range(U1):
            qt_scr[rws[u], :] = _bf(cat(qts[u * HG:(u + 1) * HG]))
            y_scr[rws[u], :] = cat(y0s[u * HG:(u + 1) * HG])
        return carry

    lax.fori_loop(0, nchunk // U1, phase1, 0)

    S = [s_scr[h] for h in hs]
    for c in range(nchunk):
        rows = slice(c * L, (c + 1) * L)
        Sb = [_bf(s_) for s_ in S]
        sphi = [_dot(Sb[h], phi_scr[c, h]) for h in hs]
        ys = [_dot_nt(qt_scr[rows, sls[h]], Sb[h]) for h in hs]
        egl = egl_scr[c]
        S = [S[h] * egl[:, sls[h]] - sphi[h] + psi_scr[c, h] for h in hs]
        y_scr[rows, :] += cat(ys)
    for h in hs:
        s_scr[h] = S[h]

    y = y_scr[...]
    rk = r_scr[...] * k_scr[...] * rk_ref[...]
    cen = y - headsum(y) * (1.0 / D)
    ln = cen * lax.rsqrt(headsum(cen * cen) * (1.0 / D) + EPS)
    out = ln * lng_ref[...] + headsum(rk) * v_scr[...]
    og_ref[...] = _bf(out * _silu(z_ref[...]))

    @pl.when(t == pl.num_programs(1) - 1)
    def _():
        s_ref[0] = s_scr[...]


def _rwkv(P, B, T, p, s0, shift_prev):
    L = min(T, CHUNK)
    tt = min(T, RWKV_TT)
    nT = T // tt
    nc = tt // L
    off = _OFF['rwkv_shift']
    rowblk = lambda w, o: pl.BlockSpec((tt, w), lambda b_, t: (b_ * nT + t, o // w))
    const = lambda shape: pl.BlockSpec(shape, lambda b_, t: (0,) * len(shape))
    st = pl.BlockSpec((1, RWKV_H, RWKV_D, RWKV_D), lambda b_, t: (b_, 0, 0, 0))
    vec = const((1, BW))
    return pl.pallas_call(
        functools.partial(_rwkv_kernel, L=L),
        grid=(B, nT),
        in_specs=[rowblk(BW, off), rowblk(BW, off + BW), rowblk(BW, off + 2 * BW), rowblk(LANE, off + 3 * BW),
                  rowblk(BW, _OFF['rwkv_z']),
                  pl.BlockSpec((1, 1, RWKV_SHIFT_W), lambda b_, t: (b_, 0, 0)), const((1, RWKV_SHIFT_W)),
                  const((LANE, BW)), const((LANE, BW)), vec, vec, vec, vec, vec, vec, st],
        out_specs=[pl.BlockSpec((tt, BW), lambda b_, t: (b_ * nT + t, 0)), st],
        out_shape=[jax.ShapeDtypeStruct((B * T, BW), BF16), jax.ShapeDtypeStruct(s0.shape, F32)],
        scratch_shapes=[pltpu.VMEM((RWKV_H, RWKV_D, RWKV_D), F32),
                        pltpu.VMEM((nc, RWKV_H, RWKV_D, RWKV_D), BF16),
                        pltpu.VMEM((nc, RWKV_H, RWKV_D, RWKV_D), F32),
                        pltpu.VMEM((tt, BW), BF16),
                        pltpu.VMEM((nc, 1, BW), F32),
                        pltpu.VMEM((1, RWKV_SHIFT_W), F32)] + [pltpu.VMEM((tt, BW), F32)] * 7,
        compiler_params=_cparams("parallel", "arbitrary"),
        name="rwkv_scan",
    )(P, P, P, P, P, shift_prev, p['rwkv_mu'], p['rwkv_w2p'], p['rwkv_a2p'], p['rwkv_w0'], p['rwkv_a0'],
      p['rwkv_k_k'], p['rwkv_k_a'], p['rwkv_r_k'], p['rwkv_ln_g'], s0)


def _gla_kernel(q_ref, k_ref, v_ref, gd_ref, z_ref, g2_ref, gb_ref, ng_ref, s0_ref, o_ref, s_ref, s_scr, g_scr, *, L):
    t = pl.program_id(1)

    @pl.when(t == 0)
    def _():
        s_scr[...] = s0_ref[0]

    gk = _dot(_bf(gd_ref[...]), g2_ref[...]) + gb_ref[...]
    g_scr[...] = _log_sigmoid(gk) * (1.0 / GLA_GATE_NORM)

    nchunk = q_ref.shape[0] // L
    row, col = _iota2((L, L), 0), _iota2((L, L), 1)
    tri = (row >= col).astype(BF16)
    causal = row >= col
    scale = GLA_DK ** -0.5

    U = 4 if nchunk % 4 == 0 else 1
    hs, us = range(GLA_H), range(U)
    ksl = [slice(h * GLA_DK, (h + 1) * GLA_DK) for h in hs]
    vsl = [slice(h * GLA_DV, (h + 1) * GLA_DV) for h in hs]
    uh = [(u, h) for u in us for h in hs]

    def chunks(i, carry):
        rws = [pl.ds(pl.multiple_of((i * U + u) * L, L), L) for u in us]
        q = [q_ref[r, :] * scale for r in rws]
        k = [k_ref[r, :] for r in rws]
        b = [_cumsum_rows(tri, g_scr[r, :]) for r in rws]
        b_end = [x[L - 1:L, :] for x in b]
        qi = [_bf(q[u] * jnp.exp(b[u])) for u in us]
        ki = [_bf(k[u] * jnp.exp(-b[u])) for u in us]
        kend = [_bf(k[u] * jnp.exp(b_end[u] - b[u])) for u in us]
        eb_end = [jnp.exp(x) for x in b_end]
        vb = {(u, h): _bf(v_ref[rws[u], vsl[h]]) for u, h in uh}
        A = {(u, h): _bf(jnp.where(causal, _dot_nt(qi[u][:, ksl[h]], ki[u][:, ksl[h]]), 0.0)) for u, h in uh}
        kv = {(u, h): _dot_tn(vb[u, h], kend[u][:, ksl[h]]) for u, h in uh}
        av = {(u, h): _dot(A[u, h], vb[u, h]) for u, h in uh}
        St = [s_scr[h] for h in hs]
        o = {}
        for u in us:
            for h in hs:
                o[u, h] = av[u, h] + _dot_nt(qi[u][:, ksl[h]], _bf(St[h]))
            St = [St[h] * eb_end[u][:, ksl[h]] + kv[u, h] for h in hs]
        ms = {x: jnp.mean(o[x] * o[x], axis=-1, keepdims=True) for x in uh}
        for u, h in uh:
            o_ref[rws[u], vsl[h]] = _bf((o[u, h] * lax.rsqrt(ms[u, h] + EPS)) * ng_ref[:, vsl[h]]
                                        * _silu(z_ref[rws[u], vsl[h]]))
        for h in hs:
            s_scr[h] = St[h]
        return carry

    lax.fori_loop(0, nchunk // U, chunks, 0)

    @pl.when(t == pl.num_programs(1) - 1)
    def _():
        s_ref[0] = s_scr[...]


def _gla(P, B, T, g2p, gb, ng, s0t):
    L = min(T, CHUNK)
    tt = min(T, SCAN_TT)
    nT = T // tt
    rowblk = lambda w, off: pl.BlockSpec((tt, w), lambda b_, t: (b_ * nT + t, off // w))
    const = lambda shape: pl.BlockSpec(shape, lambda b_, t: (0,) * len(shape))
    st = pl.BlockSpec((1, GLA_H, GLA_DV, GLA_DK), lambda b_, t: (b_, 0, 0, 0))
    return pl.pallas_call(
        functools.partial(_gla_kernel, L=L),
        grid=(B, nT),
        in_specs=[rowblk(512, _OFF['gla_q']), rowblk(512, _OFF['gla_k']), rowblk(BW, _OFF['gla_v']),
                  rowblk(LANE, _OFF['gla_gd']), rowblk(BW, _OFF['gla_z']), const((LANE, 512)), const((1, 512)),
                  const((1, BW)), st],
        out_specs=[pl.BlockSpec((tt, BW), lambda b_, t: (b_ * nT + t, 0)), st],
        out_shape=[jax.ShapeDtypeStruct((B * T, BW), BF16), jax.ShapeDtypeStruct(s0t.shape, F32)],
        scratch_shapes=[pltpu.VMEM((GLA_H, GLA_DV, GLA_DK), F32), pltpu.VMEM((tt, 512), F32)],
        compiler_params=_cparams("parallel", "arbitrary"),
        name="gla_scan",
    )(P, P, P, P, P, g2p, gb, ng, s0t)


def _ret_kernel(q_ref, k_ref, v_ref, z_ref, cos_ref, sin_ref, s0_ref, o_ref, s_ref, s_scr, *, L):
    t = pl.program_id(1)

    @pl.when(t == 0)
    def _():
        s_scr[...] = s0_ref[0]

    nchunk = q_ref.shape[0] // L
    rel = (_iota2((L, L), 0) - _iota2((L, L), 1)).astype(F32)
    tcol = _iota2((L, 1), 0).astype(F32)
    scale = RET_DK ** -0.5
    lgs = [math.log(1.0 - 2.0 ** (-5.0 - h)) for h in range(RET_H)]
    decay = [jnp.where(rel >= 0, jnp.exp(lg * jnp.maximum(rel, 0.0)), 0.0) for lg in lgs]
    rowd = [jnp.exp(lg * (tcol + 1.0)) for lg in lgs]
    cold = [jnp.exp(lg * (L - 1.0 - tcol)) for lg in lgs]

    def rope(x, cos2, sin2):
        return x * cos2 + pltpu.roll(x, RET_DK // 2, 1) * sin2

    U = 4 if nchunk % 4 == 0 else 1
    hs, us = range(RET_H), range(U)
    ksl = [slice(h * RET_DK, (h + 1) * RET_DK) for h in hs]
    vsl = [slice(h * RET_DV, (h + 1) * RET_DV) for h in hs]
    uh = [(u, h) for u in us for h in hs]

    def chunks(i, carry):
        rws = [pl.ds(pl.multiple_of((i * U + u) * L, L), L) for u in us]
        cs = [(cos_ref[r, :], sin_ref[r, :]) for r in rws]
        k = {(u, h): rope(k_ref[rws[u], ksl[h]], *cs[u]) for u, h in uh}
        qb = {(u, h): _bf(rope(q_ref[rws[u], ksl[h]], *cs[u]) * scale) for u, h in uh}
        kb = {x: _bf(k[x]) for x in uh}
        kcb = {(u, h): _bf(k[u, h] * cold[h]) for u, h in uh}
        vb = {(u, h): _bf(v_ref[rws[u], vsl[h]]) for u, h in uh}
        A = {(u, h): _bf(_dot_nt(qb[u, h], kb[u, h]) * decay[h]) for u, h in uh}
        kv = {x: _dot_tn(kcb[x], vb[x]) for x in uh}
        av = {x: _dot(A[x], vb[x]) for x in uh}
        S = [s_scr[h] for h in hs]
        o = {}
        for u in us:
            for h in hs:
                o[u, h] = av[u, h] + rowd[h] * _dot(qb[u, h], _bf(S[h]))
            S = [math.exp(lgs[h] * L) * S[h] + kv[u, h] for h in hs]
        ms = {x: jnp.mean(o[x] * o[x], axis=-1, keepdims=True) for x in uh}
        for u, h in uh:
            o_ref[rws[u], vsl[h]] = _bf(o[u, h] * lax.rsqrt(ms[u, h] + EPS) * _silu(z_ref[rws[u], vsl[h]]))
        for h in hs:
            s_scr[h] = S[h]
        return carry

    lax.fori_loop(0, nchunk // U, chunks, 0)

    @pl.when(t == pl.num_programs(1) - 1)
    def _():
        s_ref[0] = s_scr[...]


def _ret(P, B, T, cos2, sin2, s0):
    L = min(T, CHUNK)
    tt = min(T, SCAN_TT)
    nT = T // tt
    rowblk = lambda w, off: pl.BlockSpec((tt, w), lambda b_, t: (b_ * nT + t, off // w))
    tab = pl.BlockSpec((tt, RET_DK), lambda b_, t: (t, 0))
    st = pl.BlockSpec((1, RET_H, RET_DK, RET_DV), lambda b_, t: (b_, 0, 0, 0))
    return pl.pallas_call(
        functools.partial(_ret_kernel, L=L),
        grid=(B, nT),
        in_specs=[rowblk(512, _OFF['ret_q']), rowblk(512, _OFF['ret_k']), rowblk(BW, _OFF['ret_v']),
                  rowblk(BW, _OFF['ret_z']), tab, tab, st],
        out_specs=[pl.BlockSpec((tt, BW), lambda b_, t: (b_ * nT + t, 0)), st],
        out_shape=[jax.ShapeDtypeStruct((B * T, BW), BF16), jax.ShapeDtypeStruct(s0.shape, F32)],
        scratch_shapes=[pltpu.VMEM((RET_H, RET_DK, RET_DV), F32)],
        compiler_params=_cparams("parallel", "arbitrary"),
        name="ret_scan",
    )(P, P, P, P, cos2, sin2, s0)


def _mlstm_pre_kernel(xm_ref, cp_ref, cw_ref, cb_ref, wq_ref, wk_ref, wv_ref, wif_ref, bif_ref,
                      xc_ref, q_ref, k_ref, v_ref, g_ref, carry_scr):
    t = pl.program_id(1)
    tm = xm_ref.shape[0]
    nprev = ML_CONV - 1

    @pl.when(t == 0)
    def _():
        carry_scr[...] = jnp.zeros_like(carry_scr)
        carry_scr[SUBLANE - nprev:SUBLANE, :] = cp_ref[0]

    xm = xm_ref[...]
    c8 = carry_scr[...]
    rid = _iota2((SUBLANE, 1), 0)
    conv = xm * cw_ref[nprev:nprev + 1, :] + cb_ref[...]
    for j in range(1, ML_CONV):
        rolled = pltpu.roll(xm, j, 0)
        head = jnp.where(rid < j, pltpu.roll(c8, j, 0), rolled[0:SUBLANE, :])
        prev = jnp.concatenate([head, rolled[SUBLANE:, :]], axis=0) if tm > SUBLANE else head
        conv = conv + prev * cw_ref[nprev - j:nprev - j + 1, :]
    carry_scr[...] = xm[tm - SUBLANE:tm, :]
    xc = _silu(conv)
    xc_ref[...] = xc
    xcb, xmb = _bf(xc), _bf(xm)
    q = _dot(xcb, wq_ref[...])
    k = _dot(xcb, wk_ref[...])
    v = _dot(xmb, wv_ref[...])
    g_ref[...] = (_dot(_bf(q), wif_ref[0]) + _dot(_bf(k), wif_ref[1]) + _dot(_bf(v), wif_ref[2])) + bif_ref[...]
    q_ref[...] = q
    k_ref[...] = k * (ML_D ** -0.5)
    v_ref[...] = v


def _mlstm_pre(P, B, T, conv_prev, p):
    tm = min(T, SCAN_TT)
    nT = T // tm
    row = pl.BlockSpec((tm, BW), lambda b_, t: (b_ * nT + t, 0))
    const = lambda shape: pl.BlockSpec(shape, lambda b_, t: (0,) * len(shape))
    M = B * T
    return pl.pallas_call(
        _mlstm_pre_kernel,
        grid=(B, nT),
        in_specs=[pl.BlockSpec((tm, BW), lambda b_, t: (b_ * nT + t, _OFF['ml_x'] // BW)),
                  pl.BlockSpec((1, ML_CONV - 1, BW), lambda b_, t: (b_, 0, 0)),
                  const((ML_CONV, BW)), const((1, BW)), const((BW, BW)), const((BW, BW)), const((BW, BW)),
                  const((3, BW, LANE)), const((1, LANE))],
        out_specs=[row, row, row, row, pl.BlockSpec((tm, LANE), lambda b_, t: (b_ * nT + t, 0))],
        out_shape=[jax.ShapeDtypeStruct((M, BW), F32)] * 4 + [jax.ShapeDtypeStruct((M, LANE), F32)],
        scratch_shapes=[pltpu.VMEM((SUBLANE, BW), F32)],
        compiler_params=_cparams("parallel", "arbitrary"),
        name="mlstm_pre",
    )(P, conv_prev, p['ml_conv_w'], p['ml_conv_b'], p['ml_wq'], p['ml_wk'], p['ml_wv'], p['ml_wif'], p['ml_bif'])


def _mlstm_kernel(q_ref, k_ref, v_ref, gc_ref, gr_ref, xc_ref, z_ref, ng_ref, sk_ref, c0_ref, n0_ref, m0_ref,
                  o_ref, c_ref, n_ref, m_ref, c_scr, n_scr, m_scr, *, L):
    t = pl.program_id(1)

    @pl.when(t == 0)
    def _():
        c_scr[...] = c0_ref[0]
        n_scr[...] = n0_ref[0]
        m_scr[...] = m0_ref[0]

    nchunk = q_ref.shape[0] // L
    row, col = _iota2((L, L), 0), _iota2((L, L), 1)
    tri = (row >= col).astype(BF16)
    triu = (row <= col).astype(BF16)
    causal = row >= col

    U = 2 if nchunk % 2 == 0 else 1
    hs, us = range(ML_H), range(U)
    sl = [slice(h * ML_D, (h + 1) * ML_D) for h in hs]
    uh = [(u, h) for u in us for h in hs]

    def chunks(i, carry):
        cidx = [i * U + u for u in us]
        rws = [pl.ds(pl.multiple_of(c * L, L), L) for c in cidx]
        gcb = [gc_ref[0, c] for c in cidx]
        grb = [gr_ref[0, c] for c in cidx]
        bc = [_cumsum_rows(tri, x) for x in gcb]
        br = [_cumsum_cols(x, triu) for x in grb]
        q = {(u, h): q_ref[rws[u], sl[h]] for u, h in uh}
        k = {(u, h): k_ref[rws[u], sl[h]] for u, h in uh}
        qb = {x: _bf(q[x]) for x in uh}
        vb = {(u, h): _bf(v_ref[rws[u], sl[h]]) for u, h in uh}
        qk = {x: _dot_nt(qb[x], _bf(k[x])) for x in uh}
        b_c = {(u, h): bc[u][:, ML_H + h:ML_H + h + 1] for u, h in uh}
        dlog = {(u, h): jnp.where(causal, b_c[u, h] - br[u][ML_H + h:ML_H + h + 1, :] + grb[u][h:h + 1, :], -jnp.inf)
                for u, h in uh}
        m_loc = {x: jnp.max(dlog[x], axis=-1, keepdims=True) for x in uh}
        s_loc = {x: qk[x] * jnp.exp(dlog[x] - m_loc[x]) for x in uh}
        rs = {x: jnp.sum(s_loc[x], axis=-1, keepdims=True) for x in uh}
        b_end = {x: b_c[x][L - 1:L, :] for x in uh}
        m_le = {x: m_loc[x][L - 1:L, :] for x in uh}
        kw = {(u, h): k[u, h] * jnp.exp(b_end[u, h] - b_c[u, h] + gcb[u][:, h:h + 1] - m_le[u, h]) for u, h in uh}
        sv = {x: _dot(_bf(s_loc[x]), vb[x]) for x in uh}
        kv = {x: _dot_tn(_bf(kw[x]), vb[x]) for x in uh}
        ksum = {x: jnp.sum(kw[x], axis=0, keepdims=True) for x in uh}
        C = [c_scr[h] for h in hs]
        nvec = [n_scr[h] for h in hs]
        m_prev = [m_scr[h][:, 0:1] for h in hs]
        hh = {}
        for u in us:
            qc = [_dot(qb[u, h], _bf(C[h])) for h in hs]
            qn = [jnp.sum(q[u, h] * nvec[h], axis=-1, keepdims=True) for h in hs]
            from_state = [b_c[u, h] + m_prev[h] for h in hs]
            m_t = [jnp.maximum(from_state[h], m_loc[u, h]) for h in hs]
            a_in = [jnp.exp(m_loc[u, h] - m_t[h]) for h in hs]
            w_state = [jnp.exp(from_state[h] - m_t[h]) for h in hs]
            den = [a_in[h] * rs[u, h] + w_state[h] * qn[h] for h in hs]
            for h in hs:
                hh[u, h] = ((a_in[h] * sv[u, h] + w_state[h] * qc[h])
                            / jnp.maximum(jnp.abs(den[h]), jnp.exp(-m_t[h])))
            m_new = [m_t[h][L - 1:L, :] for h in hs]
            cd = [jnp.exp(b_end[u, h] + m_prev[h] - m_new[h]) for h in hs]
            sc = [jnp.exp(m_le[u, h] - m_new[h]) for h in hs]
            C = [cd[h] * C[h] + sc[h] * kv[u, h] for h in hs]
            nvec = [cd[h] * nvec[h] + sc[h] * ksum[u, h] for h in hs]
            m_prev = m_new
        mean = {x: jnp.mean(hh[x], axis=-1, keepdims=True) for x in uh}
        cen = {x: hh[x] - mean[x] for x in uh}
        var = {x: jnp.mean(cen[x] * cen[x], axis=-1, keepdims=True) for x in uh}
        for u, h in uh:
            y = cen[u, h] * lax.rsqrt(var[u, h] + EPS)
            o_ref[rws[u], sl[h]] = _bf((y * ng_ref[:, sl[h]] + sk_ref[:, sl[h]] * xc_ref[rws[u], sl[h]])
                                       * _silu(z_ref[rws[u], sl[h]]))
        for h in hs:
            c_scr[h] = C[h]
            n_scr[h] = nvec[h]
            m_scr[h] = jnp.broadcast_to(m_prev[h], (1, LANE))
        return carry

    lax.fori_loop(0, nchunk // U, chunks, 0)

    @pl.when(t == pl.num_programs(1) - 1)
    def _():
        c_ref[0] = c_scr[...]
        n_ref[0] = n_scr[...]
        m_ref[0] = m_scr[...]


def _mlstm(q, k, v, gc, gr, xc, P, ng, sk, c0, n0, m0, B, T):
    L = min(T, CHUNK)
    tt = min(T, SCAN_TT)
    nT = T // tt
    nc = tt // L
    row = pl.BlockSpec((tt, BW), lambda b_, t: (b_ * nT + t, 0))
    const = pl.BlockSpec((1, BW), lambda b_, t: (0, 0))
    stc = pl.BlockSpec((1, ML_H, ML_D, ML_D), lambda b_, t: (b_, 0, 0, 0))
    stn = pl.BlockSpec((1, ML_H, 1, ML_D), lambda b_, t: (b_, 0, 0, 0))
    stm = pl.BlockSpec((1, ML_H, 1, LANE), lambda b_, t: (b_, 0, 0, 0))
    return pl.pallas_call(
        functools.partial(_mlstm_kernel, L=L),
        grid=(B, nT),
        in_specs=[row, row, row,
                  pl.BlockSpec((1, nc, L, 8), lambda b_, t: (b_, t, 0, 0)),
                  pl.BlockSpec((1, nc, 8, L), lambda b_, t: (b_, t, 0, 0)),
                  row, pl.BlockSpec((tt, BW), lambda b_, t: (b_ * nT + t, _OFF['ml_z'] // BW)), const, const, stc, stn, stm],
        out_specs=[row, stc, stn, stm],
        out_shape=[jax.ShapeDtypeStruct((B * T, BW), BF16), jax.ShapeDtypeStruct(c0.shape, F32),
                   jax.ShapeDtypeStruct(n0.shape, F32), jax.ShapeDtypeStruct(m0.shape, F32)],
        scratch_shapes=[pltpu.VMEM((ML_H, ML_D, ML_D), F32), pltpu.VMEM((ML_H, 1, ML_D), F32),
                        pltpu.VMEM((ML_H, 1, LANE), F32)],
        compiler_params=_cparams("parallel", "arbitrary"),
        name="mlstm_scan",
    )(q, k, v, gc, gr, xc, P, ng, sk, c0, n0, m0)


def _xattn_kernel(q_ref, z_ref, mk_ref, mv_ref, o_ref):
    hs = range(XA_H)
    sl = [slice(h * XA_D, (h + 1) * XA_D) for h in hs]
    s = [_dot_nt(_bf(q_ref[:, x]), _bf(mk_ref[0, :, x])) * (XA_D ** -0.5) for x in sl]
    e = [jnp.exp(x - jnp.max(x, axis=-1, keepdims=True)) for x in s]
    prob = [_bf(x / jnp.sum(x, axis=-1, keepdims=True)) for x in e]
    o = [_dot(prob[h], _bf(mv_ref[0, :, sl[h]])) for h in hs]
    for h in hs:
        o_ref[:, sl[h]] = _bf(o[h] * _silu(z_ref[:, sl[h]]))


def _xattn(P, B, T, mk, mv):
    tt = min(T, SCAN_TT)
    nT = T // tt
    n_mem = mk.shape[1]
    mem = pl.BlockSpec((1, n_mem, BW), lambda b_, t: (b_, 0, 0))
    return pl.pallas_call(
        _xattn_kernel,
        grid=(B, nT),
        in_specs=[pl.BlockSpec((tt, BW), lambda b_, t: (b_ * nT + t, _OFF['xa_q'] // BW)),
                  pl.BlockSpec((tt, BW), lambda b_, t: (b_ * nT + t, _OFF['xa_z'] // BW)), mem, mem],
        out_specs=pl.BlockSpec((tt, BW), lambda b_, t: (b_ * nT + t, 0)),
        out_shape=jax.ShapeDtypeStruct((B * T, BW), BF16),
        compiler_params=_cparams("parallel", "parallel"),
        name="xattn",
    )(P, P, mk, mv)


def _merge_kernel(o0, o1, o2, o3, o4, g_ref, w_ref, out_ref):
    ys = [_dot(o_ref[...], w_ref[i]) for i, o_ref in enumerate((o0, o1, o2, o3, o4))]
    acc = None
    for i, y in enumerate(ys):
        term = _sigmoid(g_ref[:, i * D_MODEL:(i + 1) * D_MODEL]) * y
        acc = term if acc is None else acc + term
    out_ref[...] = _bf(acc)


def _merge(ogs, P, wb, tm):
    M = ogs[0].shape[0]
    gw = N_BRANCH * D_MODEL
    og = pl.BlockSpec((tm, BW), lambda i: (i, 0))
    return pl.pallas_call(
        _merge_kernel,
        grid=(M // tm,),
        in_specs=[og] * N_BRANCH + [pl.BlockSpec((tm, gw), lambda i: (i, _OFF['gates'] // gw)),
                                    pl.BlockSpec((N_BRANCH, BW, D_MODEL), lambda i: (0, 0, 0),
                                                 pipeline_mode=pl.Buffered(1))],
        out_specs=pl.BlockSpec((tm, D_MODEL), lambda i: (i, 0)),
        out_shape=jax.ShapeDtypeStruct((M, D_MODEL), BF16),
        compiler_params=_cparams("parallel"),
        name="merge",
    )(*ogs, P, wb)


def _resid_kernel(x_ref, m_ref, w_ref, o_ref):
    o_ref[...] = x_ref[...] + _dot(m_ref[...], w_ref[...])


def _resid_norm_kernel(x_ref, m_ref, w_ref, g_ref, o_ref):
    y = x_ref[...] + _dot(m_ref[...], w_ref[...])
    ms = jnp.mean(y * y, axis=-1, keepdims=True)
    o_ref[...] = (y * lax.rsqrt(ms + EPS)) * g_ref[...]


def _resid_out(x, merged, w, tm, final_g=None):
    M = x.shape[0]
    row = pl.BlockSpec((tm, D_MODEL), lambda i: (i, 0))
    in_specs = [row, row, pl.BlockSpec((D_MODEL, D_MODEL), lambda i: (0, 0), pipeline_mode=pl.Buffered(1))]
    args = [x, merged, w]
    if final_g is not None:
        in_specs.append(pl.BlockSpec((1, D_MODEL), lambda i: (0, 0)))
        args.append(final_g)
    return pl.pallas_call(
        _resid_kernel if final_g is None else _resid_norm_kernel,
        grid=(M // tm,),
        in_specs=in_specs,
        out_specs=row,
        out_shape=jax.ShapeDtypeStruct((M, D_MODEL), F32),
        compiler_params=_cparams("parallel"),
        name="resid_out",
    )(*args)


def _blockdiag(w):
    depth, n, c, d = w.shape
    rows = w.reshape(depth, n * c, d)
    tile = (jnp.arange(d)[:, None] == jnp.arange(n * d)[None, :] % d).astype(w.dtype)
    dense = jnp.einsum('lrd,dj->lrj', rows, tile)
    mask = (jnp.arange(n * c)[:, None] // c) == (jnp.arange(n * d)[None, :] // d)
    return jnp.where(mask, dense, 0.0).astype(BF16)


def _rope_tables(start, T):
    half = RET_DK // 2
    inv = ROPE_BASE ** (-jnp.linspace(0.0, 1.0, half, dtype=F32))
    blk = ROPE_BLK
    if T % blk or T <= blk:
        ang = (start + jnp.arange(T, dtype=F32))[:, None] * inv[None, :]
        cos, sin = jnp.cos(ang), jnp.sin(ang)
    else:
        hi = (start + blk * jnp.arange(T // blk, dtype=F32))[:, None] * inv[None, :]
        lo = jnp.arange(blk, dtype=F32)[:, None] * inv[None, :]
        ch, sh, cl, sl_ = jnp.cos(hi)[:, None, :], jnp.sin(hi)[:, None, :], jnp.cos(lo)[None], jnp.sin(lo)[None]
        cos = (ch * cl - sh * sl_).reshape(T, half)
        sin = (sh * cl + ch * sl_).reshape(T, half)
    return jnp.concatenate([cos, cos], axis=-1), jnp.concatenate([-sin, sin], axis=-1)


def _layer(x, B, T, tabs, st, mem_k, mem_v, p, final_g=None):
    M = B * T
    tm = min(M, PROJ_TM)
    P = _normproj(x, p['norm_g'], p['w_pack'], tm, PROJ_TN)
    P3 = P.reshape(B, T, NP)

    og_a, s_rwkv = _rwkv(P, B, T, p, st['rwkv'], st['rwkv_shift'])
    shift_new = P3[:, T - 1:, _OFF['rwkv_shift']:_OFF['rwkv_shift'] + RWKV_SHIFT_W]

    og_b, s_gla_t = _gla(P, B, T, p['gla_g2p'], p['gla_gb'], p['gla_norm_g'], jnp.swapaxes(st['gla'], -1, -2))
    s_gla = jnp.swapaxes(s_gla_t, -1, -2)

    og_c, s_ret = _ret(P, B, T, tabs[0], tabs[1], st['ret'])

    xc, q_m, k_m, v_m, gates = _mlstm_pre(P, B, T, st['ml_conv'], p)
    L = min(T, CHUNK)
    gcol = jnp.concatenate([gates[:, :ML_H], _log_sigmoid(gates[:, ML_H:2 * ML_H])], axis=-1).reshape(B, T // L, L, 2 * ML_H)
    grow = jnp.swapaxes(gcol, -1, -2)
    og_d, c_new, n_new, m_new = _mlstm(
        q_m, k_m, v_m, gcol, grow, xc, P, p['ml_norm_g'], p['ml_skip'], st['ml_c'], st['ml_n'][:, :, None, :],
        jnp.broadcast_to(st['ml_m'][:, :, None, None], st['ml_m'].shape + (1, LANE)), B, T)
    xm_tail = P3[:, max(T - (ML_CONV - 1), 0):, _OFF['ml_x']:_OFF['ml_x'] + BW]
    conv_new = jnp.concatenate([st['ml_conv'], xm_tail], axis=1)[:, -(ML_CONV - 1):]

    og_x = _xattn(P, B, T, mem_k.reshape(B, -1, BW), mem_v.reshape(B, -1, BW))

    merged = _merge([og_a, og_b, og_c, og_d, og_x], P, p['w_branch'], min(M, MERGE_TM))
    x_new = _resid_out(x, merged, p['w_out'], min(M, RESID_TM), final_g)
    new = {'rwkv': s_rwkv, 'rwkv_shift': shift_new, 'gla': s_gla, 'ret': s_ret, 'ml_c': c_new,
           'ml_n': n_new[:, :, 0, :], 'ml_m': m_new[:, :, 0, 0], 'ml_conv': conv_new}
    return x_new, new


_STATE_KEYS = ('rwkv', 'rwkv_shift', 'gla', 'ret', 'ml_c', 'ml_n', 'ml_m', 'ml_conv')


def kernel(x_prompt, x_sample, mem_prompt, cache_mem_k, cache_mem_v, state_rwkv, state_rwkv_shift, state_gla, state_ret, state_mlstm_c, state_mlstm_n, state_mlstm_m, state_mlstm_conv, norm_g, mem_norm_g, w_in, w_mem_kv, rwkv_mu, rwkv_w0, rwkv_w2, rwkv_a0, rwkv_a2, rwkv_k_k, rwkv_k_a, rwkv_r_k, rwkv_ln_g, gla_g2, gla_gb, gla_norm_g, ml_conv_w, ml_conv_b, ml_wq, ml_wk, ml_wv, ml_w_if, ml_b_if, ml_skip, ml_norm_g, w_branch, w_out, final_norm_g):
    Bp, Tp, D = x_prompt.shape
    Bs, Ts, _ = x_sample.shape
    depth = w_in.shape[0]
    n_mem = mem_prompt.shape[1]

    params = {
        'norm_g': norm_g[:, None, :], 'mem_norm_g': mem_norm_g[:, None, :],
        'w_mem_kv': _bf(w_mem_kv),
        'rwkv_mu': rwkv_mu[:, None, :], 'rwkv_w0': rwkv_w0[:, None, :], 'rwkv_a0': rwkv_a0[:, None, :],
        'rwkv_w2p': _bf(jnp.pad(rwkv_w2, ((0, 0), (0, LANE - RWKV_LORA), (0, 0)))),
        'rwkv_a2p': _bf(jnp.pad(rwkv_a2, ((0, 0), (LANE - RWKV_LORA, 0), (0, 0)))),
        'rwkv_k_k': rwkv_k_k[:, None, :], 'rwkv_k_a': rwkv_k_a[:, None, :], 'rwkv_r_k': rwkv_r_k[:, None, :],
        'rwkv_ln_g': rwkv_ln_g[:, None, :],
        'gla_g2p': _bf(jnp.pad(gla_g2, ((0, 0), (0, LANE - GLA_LORA), (0, 0)))), 'gla_gb': gla_gb[:, None, :],
        'gla_norm_g': gla_norm_g[:, None, :],
        'ml_conv_w': ml_conv_w, 'ml_conv_b': ml_conv_b[:, None, :],
        'ml_wq': _blockdiag(ml_wq), 'ml_wk': _blockdiag(ml_wk), 'ml_wv': _blockdiag(ml_wv),
        'ml_wif': _bf(jnp.pad(ml_w_if.reshape(depth, 3, BW, 2 * ML_H), ((0, 0), (0, 0), (0, 0), (0, LANE - 2 * ML_H)))),
        'ml_bif': jnp.pad(ml_b_if, ((0, 0), (0, LANE - 2 * ML_H)))[:, None, :],
        'ml_skip': ml_skip[:, None, :], 'ml_norm_g': ml_norm_g[:, None, :],
        'w_branch': _bf(w_branch), 'w_out': _bf(w_out),
    }
    cache = {'rwkv': state_rwkv, 'rwkv_shift': state_rwkv_shift, 'gla': state_gla, 'ret': state_ret,
             'ml_c': state_mlstm_c, 'ml_n': state_mlstm_n, 'ml_m': state_mlstm_m, 'ml_conv': state_mlstm_conv,
             'mem_k': cache_mem_k, 'mem_v': cache_mem_v}
    tabs_p = _rope_tables(0.0, Tp)
    tabs_s = _rope_tables(float(PAST_LEN), Ts)
    zero_p = {
        'rwkv': jnp.zeros((Bp, RWKV_H, RWKV_D, RWKV_D), F32), 'rwkv_shift': jnp.zeros((Bp, 1, RWKV_SHIFT_W), F32),
        'gla': jnp.zeros((Bp, GLA_H, GLA_DK, GLA_DV), F32), 'ret': jnp.zeros((Bp, RET_H, RET_DK, RET_DV), F32),
        'ml_c': jnp.zeros((Bp, ML_H, ML_D, ML_D), F32), 'ml_n': jnp.zeros((Bp, ML_H, ML_D), F32),
        'ml_m': jnp.zeros((Bp, ML_H), F32), 'ml_conv': jnp.zeros((Bp, ML_CONV - 1, BW), F32),
    }
    mem2d = mem_prompt.reshape(Bp * n_mem, D)

    yp, ys = x_prompt.reshape(Bp * Tp, D), x_sample.reshape(Bs * Ts, D)
    outs = []
    for l in range(depth):
        p = {nm: arr[l] for nm, arr in params.items()}
        p['w_pack'] = _pack_w_in(w_in[l])
        kv = _normproj(mem2d, p['mem_norm_g'], p['w_mem_kv'], min(Bp * n_mem, MEM_TM), MEM_TN)
        mk_l = kv[:, :BW].reshape(Bp, n_mem, XA_H, XA_D)
        mv_l = kv[:, BW:].reshape(Bp, n_mem, XA_H, XA_D)
        fg = final_norm_g[None, :] if l == depth - 1 else None
        yp, stp_l = _layer(yp, Bp, Tp, tabs_p, zero_p, mk_l, mv_l, p, fg)
        ys, sts_l = _layer(ys, Bs, Ts, tabs_s, {nm: cache[nm][l] for nm in _STATE_KEYS},
                           cache['mem_k'][l], cache['mem_v'][l], p, fg)
        outs.append((stp_l, mk_l, mv_l, sts_l))
    stp = {nm: jnp.stack([o[0][nm] for o in outs]) for nm in _STATE_KEYS}
    sts = {nm: jnp.stack([o[3][nm] for o in outs]) for nm in _STATE_KEYS}
    mk = jnp.stack([o[1] for o in outs])
    mv = jnp.stack([o[2] for o in outs])
    y_prompt = yp.reshape(Bp, Tp, D)
    y_sample = ys.reshape(Bs, Ts, D)
    return (y_prompt, y_sample,
            stp['rwkv'], stp['rwkv_shift'], stp['gla'], stp['ret'], stp['ml_c'], stp['ml_n'], stp['ml_m'],
            stp['ml_conv'], mk, mv,
            sts['rwkv'], sts['rwkv_shift'], sts['gla'], sts['ret'], sts['ml_c'], sts['ml_n'], sts['ml_m'],
            sts['ml_conv'])
```

```python
import functools
import math

import jax
import jax.numpy as jnp
from jax import lax
from jax.experimental import pallas as pl
from jax.experimental.pallas import tpu as pltpu

F32 = jnp.float32
BF16 = jnp.bfloat16

D_MODEL = 2048
BW = 1024
EPS = 1e-6
CHUNK = 64
N_BRANCH = 5
PAST_LEN = 2048
RWKV_H, RWKV_D, RWKV_LORA = 16, 64, 64
RWKV_SHIFT_W = 3 * BW + 2 * RWKV_LORA
RWKV_DECAY_SCALE = 0.606531
GLA_H, GLA_DK, GLA_DV, GLA_LORA = 4, 128, 256, 16
GLA_GATE_NORM = 16.0
RET_H, RET_DK, RET_DV = 4, 128, 256
ROPE_BASE = 10000.0
ML_H, ML_D, ML_CONV = 4, 256, 4
XA_H, XA_D = 4, 256
LANE = 128
SUBLANE = 8
VMEM_LIMIT = 56 * 1024 * 1024

PROJ_TM = 1024
MEM_TM, MEM_TN = 512, 512
RWKV_TT = 256
SCAN_TT = 512
MERGE_TM = 256
RESID_TM = 512
ROPE_BLK = 128
PACK_TR = 128

_IN_LAYOUT = (
    ('rwkv_shift', RWKV_SHIFT_W), ('rwkv_z', BW), ('gla_q', 512), ('gla_k', 512), ('gla_v', BW),
    ('gla_gd', GLA_LORA), ('gla_z', BW), ('ret_q', 512), ('ret_k', 512), ('ret_v', BW), ('ret_z', BW),
    ('ml_x', BW), ('ml_z', BW), ('xa_q', BW), ('xa_z', BW), ('gates', N_BRANCH * D_MODEL),
)
_PACK_ORDER = ('gates', 'rwkv_z', 'gla_z', 'ret_z', 'ml_z', 'xa_z', 'ret_q', 'ret_k', 'ret_v',
               'gla_q', 'gla_k', 'gla_v', 'ml_x', 'xa_q', 'rwkv_shift', 'gla_gd')
PROJ_TN = 1792


def _src_cols(name):
    start = 0
    for nm, size in _IN_LAYOUT:
        if nm == name:
            return start, start + size
        start += size
    raise KeyError(name)


def _pack_offsets():
    off, cur = {}, 0
    for nm in _PACK_ORDER:
        a, b = _src_cols(nm)
        off[nm] = cur
        cur += -(-(b - a) // LANE) * LANE
    total = -(-cur // PROJ_TN) * PROJ_TN
    return off, total


_OFF, NP = _pack_offsets()


def _pack_kernel(w_ref, o_ref):
    cur = 0
    for nm in _PACK_ORDER:
        a, b = _src_cols(nm)
        o_ref[:, cur:cur + (b - a)] = w_ref[0, :, a:b].astype(BF16)
        width = -(-(b - a) // LANE) * LANE
        if width != b - a:
            o_ref[:, cur + (b - a):cur + width] = jnp.zeros((o_ref.shape[0], width - (b - a)), BF16)
        cur += width
    if NP != cur:
        o_ref[:, cur:NP] = jnp.zeros((o_ref.shape[0], NP - cur), BF16)


def _pack_w_in(w_in, layer):
    _, K, n_in = w_in.shape
    return pl.pallas_call(
        _pack_kernel,
        grid=(K // PACK_TR,),
        in_specs=[pl.BlockSpec((1, PACK_TR, n_in), lambda i: (layer, i, 0))],
        out_specs=pl.BlockSpec((PACK_TR, NP), lambda i: (i, 0)),
        out_shape=jax.ShapeDtypeStruct((K, NP), BF16),
        compiler_params=_cparams("parallel"),
        name="pack_w_in",
    )(w_in)


def _cparams(*sem):
    return pltpu.CompilerParams(dimension_semantics=sem, vmem_limit_bytes=VMEM_LIMIT)


def _dot(a, b):
    return jnp.dot(a, b, preferred_element_type=F32)


def _dot_nt(a, b):
    return lax.dot_general(a, b, (((1,), (1,)), ((), ())), preferred_element_type=F32)


def _dot_tn(a, b):
    return lax.dot_general(a, b, (((0,), (0,)), ((), ())), preferred_element_type=F32)


def _bf(x):
    return x.astype(BF16)


def _split3(x):
    hi = x.astype(BF16)
    r1 = x - hi.astype(F32)
    mid = r1.astype(BF16)
    lo = (r1 - mid.astype(F32)).astype(BF16)
    return hi, mid, lo


def _cumsum_rows(tri, x):
    hi, mid, lo = _split3(x)
    return _dot(tri, hi) + _dot(tri, mid) + _dot(tri, lo)


def _cumsum_cols(x, triu):
    hi, mid, lo = _split3(x)
    return _dot(hi, triu) + _dot(mid, triu) + _dot(lo, triu)


def _iota2(shape, dim):
    return lax.broadcasted_iota(jnp.int32, shape, dim)


def _log_sigmoid(x):
    return jnp.minimum(x, 0.0) - jnp.log1p(jnp.exp(-jnp.abs(x)))


def _sigmoid(x):
    return 1.0 / (1.0 + jnp.exp(-x))


def _silu(x):
    return x * _sigmoid(x)


def _normproj_kernel(x_ref, g_ref, w_ref, o_ref, h_ref):
    @pl.when(pl.program_id(1) == 0)
    def _():
        x = x_ref[...]
        ms = jnp.mean(x * x, axis=-1, keepdims=True)
        h_ref[...] = ((x * lax.rsqrt(ms + EPS)) * g_ref[...]).astype(BF16)

    o_ref[...] = _dot(h_ref[...], w_ref[...])


def _normproj(x, g, w, tm, tn):
    M, K = x.shape
    N = w.shape[1]
    return pl.pallas_call(
        _normproj_kernel,
        grid=(M // tm, N // tn),
        in_specs=[pl.BlockSpec((tm, K), lambda i, j: (i, 0)),
                  pl.BlockSpec((1, K), lambda i, j: (0, 0)),
                  pl.BlockSpec((K, tn), lambda i, j: (0, j))],
        out_specs=pl.BlockSpec((tm, tn), lambda i, j: (i, j)),
        out_shape=jax.ShapeDtypeStruct((M, N), F32),
        scratch_shapes=[pltpu.VMEM((tm, K), BF16)],
        compiler_params=_cparams("parallel", "arbitrary"),
        name="normproj",
    )(x, g, w)


def _rwkv_kernel(xr_ref, xk_ref, xv_ref, xwa_ref, z_ref, sp_ref, mu_ref, w2_ref, a2_ref, w0_ref, a0_ref,
                 kkw_ref, ka_ref, rk_ref, lng_ref, s0_ref, og_ref, s_ref,
                 s_scr, phi_scr, psi_scr, qt_scr, egl_scr, carry_scr,
                 r_scr, lw_scr, k_scr, v_scr, kk_scr, b_scr, y_scr, *, L):
    t = pl.program_id(1)
    HG, D = RWKV_H, RWKV_D
    hs = range(HG)
    sls = [slice(h * D, (h + 1) * D) for h in hs]

    @pl.when(t == 0)
    def _():
        s_scr[...] = s0_ref[0]
        carry_scr[...] = sp_ref[0]

    def cat(parts):
        return jnp.concatenate(parts, axis=1)

    tt = xr_ref.shape[0]
    first = _iota2((tt, 1), 0) == 0

    def mix(x_ref, lo, hi):
        x = x_ref[...]
        prev = jnp.where(first, carry_scr[:, lo:hi], pltpu.roll(x, 1, 0))
        carry_scr[:, lo:hi] = x[tt - 1:tt, :]
        return x + (prev - x) * mu_ref[:, lo:hi]

    r_ = mix(xr_ref, 0, BW)
    k_ = mix(xk_ref, BW, 2 * BW)
    v_ = mix(xv_ref, 2 * BW, 3 * BW)
    wa = mix(xwa_ref, 3 * BW, RWKV_SHIFT_W)
    lw_scr[...] = -RWKV_DECAY_SCALE * _sigmoid(w0_ref[...] + _dot(_bf(jnp.tanh(wa)), w2_ref[...]))
    a_ = _sigmoid(a0_ref[...] + _dot(_bf(wa), a2_ref[...]))
    ones_blk = (_iota2((LANE, LANE), 0) // D == _iota2((LANE, LANE), 1) // D).astype(BF16)

    def headsum(x):
        hi = _bf(x)
        lo = _bf(x - hi.astype(F32))
        return cat([_dot(hi[:, j:j + LANE], ones_blk) + _dot(lo[:, j:j + LANE], ones_blk)
                    for j in range(0, BW, LANE)])

    kkf = k_ * kkw_ref[...]
    kk = kkf * lax.rsqrt(jnp.maximum(headsum(kkf * kkf), 1e-24))
    r_scr[...] = r_
    k_scr[...] = k_ * (1.0 + (a_ - 1.0) * ka_ref[...])
    v_scr[...] = v_
    kk_scr[...] = kk
    b_scr[...] = kk * a_

    nchunk = tt // L
    row, col = _iota2((L, L), 0), _iota2((L, L), 1)
    tri = (row >= col).astype(BF16)
    strict = row > col
    incl2 = _iota2((L, 2 * L), 0) >= _iota2((L, 2 * L), 1) % L
    eye = (row == col).astype(F32)

    U1 = 4 if nchunk % 4 == 0 else 1
    it = range(U1 * HG)
    isl = [sls[i % HG] for i in it]

    def phase1(ci, carry):
        cidx = [ci * U1 + u for u in range(U1)]
        rws = [pl.ds(pl.multiple_of(c * L, L), L) for c in cidx]
        kkp, rg, kn, bn, kend, bend, v_c = [], [], [], [], [], [], []
        for u, c in enumerate(cidx):
            lw, k_c, b_c = lw_scr[rws[u], :], k_scr[rws[u], :], b_scr[rws[u], :]
            g = _cumsum_rows(tri, lw)
            gl = g[L - 1:L, :]
            kkp.append(kk_scr[rws[u], :] * jnp.exp(g - lw))
            rg.append(r_scr[rws[u], :] * jnp.exp(g))
            eng = jnp.exp(-g)
            kn.append(k_c * eng)
            bn.append(b_c * eng)
            ee = jnp.exp(gl - g)
            kend.append(k_c * ee)
            bend.append(b_c * ee)
            v_c.append(v_scr[rws[u], :])
            egl_scr[c] = jnp.exp(gl)
        kkp_h = [kkp[i // HG][:, isl[i]] for i in it]
        rg_h = [rg[i // HG][:, isl[i]] for i in it]
        bk = [_bf(jnp.concatenate([bn[i // HG][:, isl[i]], kn[i // HG][:, isl[i]]], axis=0)) for i in it]
        kr = [_bf(jnp.concatenate([kkp_h[i], rg_h[i]], axis=0)) for i in it]
        mnaq = [_dot_nt(kr[i], bk[i]) for i in it]
        mn = [x[:L, :] for x in mnaq]
        aq = [x[L:, :] for x in mnaq]
        N = [_bf(jnp.where(strict, m[:, L:], 0.0)) for m in mn]
        vb = [_bf(v_c[i // HG][:, isl[i]]) for i in it]
        nv = [_dot(N[i], vb[i]) for i in it]
        X = [jnp.where(strict, -m[:, :L], 0.0) for m in mn]
        Xb = [_bf(x) for x in X]
        P = [_dot(xb, xb) for xb in Xb]
        tinv = [eye + x for x in X]
        n = 2
        while n < L:
            Pb = [_bf(p) for p in P]
            if 2 * n >= L:
                tinv = [tinv[i] + _dot(_bf(tinv[i]), Pb[i]) for i in it]
            else:
                st = [_dot(_bf(jnp.concatenate([P[i], tinv[i]], axis=0)), Pb[i]) for i in it]
                P = [x[:L, :] for x in st]
                tinv = [tinv[i] + st[i][L:, :] for i in it]
            n *= 2
        gh = [_dot(_bf(tinv[i]), _bf(jnp.concatenate([kkp_h[i], nv[i]], axis=1))) for i in it]
        gm = [_bf(x[:, :D]) for x in gh]
        hm = [_bf(x[:, D:]) for x in gh]
        bend_b = [_bf(bend[i // HG][:, isl[i]]) for i in it]
        aqm = [_bf(jnp.where(incl2, a, 0.0)) for a in aq]
        hv = [jnp.concatenate([-hm[i], vb[i]], axis=0) for i in it]
        bke = [jnp.concatenate([bend_b[i], _bf(kend[i // HG][:, isl[i]])], axis=0) for i in it]
        phi = [_dot_tn(gm[i], bend_b[i]) for i in it]
        psi = [_dot_tn(hv[i], bke[i]) for i in it]
        qts = [rg_h[i] - _dot(aqm[i][:, :L], gm[i]) for i in it]
        y0s = [_dot(aqm[i], hv[i]) for i in it]
        for i in it:
            phi_scr[cidx[i // HG], i % HG] = _bf(phi[i])
            psi_scr[cidx[i // HG], i % HG] = psi[i]
        for u in range(U1):
            qt_scr[rws[u], :] = _bf(cat(qts[u * HG:(u + 1) * HG]))
            y_scr[rws[u], :] = cat(y0s[u * HG:(u + 1) * HG])
        return carry

    lax.fori_loop(0, nchunk // U1, phase1, 0)

    S = [s_scr[h] for h in hs]
    for c in range(nchunk):
        rows = slice(c * L, (c + 1) * L)
        Sb = [_bf(s_) for s_ in S]
        sphi = [_dot(Sb[h], phi_scr[c, h]) for h in hs]
        ys = [_dot_nt(qt_scr[rows, sls[h]], Sb[h]) for h in hs]
        egl = egl_scr[c]
        S = [S[h] * egl[:, sls[h]] - sphi[h] + psi_scr[c, h] for h in hs]
        y_scr[rows, :] += cat(ys)
    for h in hs:
        s_scr[h] = S[h]

    y = y_scr[...]
    rk = r_scr[...] * k_scr[...] * rk_ref[...]
    cen = y - headsum(y) * (1.0 / D)
    ln = cen * lax.rsqrt(headsum(cen * cen) * (1.0 / D) + EPS)
    out = ln * lng_ref[...] + headsum(rk) * v_scr[...]
    og_ref[...] = _bf(out * _silu(z_ref[...]))

    @pl.when(t == pl.num_programs(1) - 1)
    def _():
        s_ref[0] = s_scr[...]


def _rwkv(P, B, T, p, s0, shift_prev):
    L = min(T, CHUNK)
    tt = min(T, RWKV_TT)
    nT = T // tt
    nc = tt // L
    off = _OFF['rwkv_shift']
    rowblk = lambda w, o: pl.BlockSpec((tt, w), lambda b_, t: (b_ * nT + t, o // w))
    const = lambda shape: pl.BlockSpec(shape, lambda b_, t: (0,) * len(shape))
    st = pl.BlockSpec((1, RWKV_H, RWKV_D, RWKV_D), lambda b_, t: (b_, 0, 0, 0))
    vec = const((1, BW))
    return pl.pallas_call(
        functools.partial(_rwkv_kernel, L=L),
        grid=(B, nT),
        in_specs=[rowblk(BW, off), rowblk(BW, off + BW), rowblk(BW, off + 2 * BW), rowblk(LANE, off + 3 * BW),
                  rowblk(BW, _OFF['rwkv_z']),
                  pl.BlockSpec((1, 1, RWKV_SHIFT_W), lambda b_, t: (b_, 0, 0)), const((1, RWKV_SHIFT_W)),
                  const((LANE, BW)), const((LANE, BW)), vec, vec, vec, vec, vec, vec, st],
        out_specs=[pl.BlockSpec((tt, BW), lambda b_, t: (b_ * nT + t, 0)), st],
        out_shape=[jax.ShapeDtypeStruct((B * T, BW), BF16), jax.ShapeDtypeStruct(s0.shape, F32)],
        scratch_shapes=[pltpu.VMEM((RWKV_H, RWKV_D, RWKV_D), F32),
                        pltpu.VMEM((nc, RWKV_H, RWKV_D, RWKV_D), BF16),
                        pltpu.VMEM((nc, RWKV_H, RWKV_D, RWKV_D), F32),
                        pltpu.VMEM((tt, BW), BF16),
                        pltpu.VMEM((nc, 1, BW), F32),
                        pltpu.VMEM((1, RWKV_SHIFT_W), F32)] + [pltpu.VMEM((tt, BW), F32)] * 7,
        compiler_params=_cparams("parallel", "arbitrary"),
        name="rwkv_scan",
    )(P, P, P, P, P, shift_prev, p['rwkv_mu'], p['rwkv_w2p'], p['rwkv_a2p'], p['rwkv_w0'], p['rwkv_a0'],
      p['rwkv_k_k'], p['rwkv_k_a'], p['rwkv_r_k'], p['rwkv_ln_g'], s0)


def _gla_kernel(q_ref, k_ref, v_ref, gd_ref, z_ref, g2_ref, gb_ref, ng_ref, s0_ref, o_ref, s_ref, s_scr, g_scr, *, L):
    t = pl.program_id(1)

    @pl.when(t == 0)
    def _():
        s_scr[...] = s0_ref[0]

    gk = _dot(_bf(gd_ref[...]), g2_ref[...]) + gb_ref[...]
    g_scr[...] = _log_sigmoid(gk) * (1.0 / GLA_GATE_NORM)

    nchunk = q_ref.shape[0] // L
    row, col = _iota2((L, L), 0), _iota2((L, L), 1)
    tri = (row >= col).astype(BF16)
    causal = row >= col
    scale = GLA_DK ** -0.5

    U = 4 if nchunk % 4 == 0 else 1
    hs, us = range(GLA_H), range(U)
    ksl = [slice(h * GLA_DK, (h + 1) * GLA_DK) for h in hs]
    vsl = [slice(h * GLA_DV, (h + 1) * GLA_DV) for h in hs]
    uh = [(u, h) for u in us for h in hs]

    def chunks(i, carry):
        rws = [pl.ds(pl.multiple_of((i * U + u) * L, L), L) for u in us]
        q = [q_ref[r, :] * scale for r in rws]
        k = [k_ref[r, :] for r in rws]
        b = [_cumsum_rows(tri, g_scr[r, :]) for r in rws]
        b_end = [x[L - 1:L, :] for x in b]
        qi = [_bf(q[u] * jnp.exp(b[u])) for u in us]
        ki = [_bf(k[u] * jnp.exp(-b[u])) for u in us]
        kend = [_bf(k[u] * jnp.exp(b_end[u] - b[u])) for u in us]
        eb_end = [jnp.exp(x) for x in b_end]
        vb = {(u, h): _bf(v_ref[rws[u], vsl[h]]) for u, h in uh}
        A = {(u, h): _bf(jnp.where(causal, _dot_nt(qi[u][:, ksl[h]], ki[u][:, ksl[h]]), 0.0)) for u, h in uh}
        kv = {(u, h): _dot_tn(vb[u, h], kend[u][:, ksl[h]]) for u, h in uh}
        av = {(u, h): _dot(A[u, h], vb[u, h]) for u, h in uh}
        St = [s_scr[h] for h in hs]
        o = {}
        for u in us:
            for h in hs:
                o[u, h] = av[u, h] + _dot_nt(qi[u][:, ksl[h]], _bf(St[h]))
            St = [St[h] * eb_end[u][:, ksl[h]] + kv[u, h] for h in hs]
        ms = {x: jnp.mean(o[x] * o[x], axis=-1, keepdims=True) for x in uh}
        for u, h in uh:
            o_ref[rws[u], vsl[h]] = _bf((o[u, h] * lax.rsqrt(ms[u, h] + EPS)) * ng_ref[:, vsl[h]]
                                        * _silu(z_ref[rws[u], vsl[h]]))
        for h in hs:
            s_scr[h] = St[h]
        return carry

    lax.fori_loop(0, nchunk // U, chunks, 0)

    @pl.when(t == pl.num_programs(1) - 1)
    def _():
        s_ref[0] = s_scr[...]


def _gla(P, B, T, g2p, gb, ng, s0t):
    L = min(T, CHUNK)
    tt = min(T, SCAN_TT)
    nT = T // tt
    rowblk = lambda w, off: pl.BlockSpec((tt, w), lambda b_, t: (b_ * nT + t, off // w))
    const = lambda shape: pl.BlockSpec(shape, lambda b_, t: (0,) * len(shape))
    st = pl.BlockSpec((1, GLA_H, GLA_DV, GLA_DK), lambda b_, t: (b_, 0, 0, 0))
    return pl.pallas_call(
        functools.partial(_gla_kernel, L=L),
        grid=(B, nT),
        in_specs=[rowblk(512, _OFF['gla_q']), rowblk(512, _OFF['gla_k']), rowblk(BW, _OFF['gla_v']),
                  rowblk(LANE, _OFF['gla_gd']), rowblk(BW, _OFF['gla_z']), const((LANE, 512)), const((1, 512)),
                  const((1, BW)), st],
        out_specs=[pl.BlockSpec((tt, BW), lambda b_, t: (b_ * nT + t, 0)), st],
        out_shape=[jax.ShapeDtypeStruct((B * T, BW), BF16), jax.ShapeDtypeStruct(s0t.shape, F32)],
        scratch_shapes=[pltpu.VMEM((GLA_H, GLA_DV, GLA_DK), F32), pltpu.VMEM((tt, 512), F32)],
        compiler_params=_cparams("parallel", "arbitrary"),
        name="gla_scan",
    )(P, P, P, P, P, g2p, gb, ng, s0t)


def _ret_kernel(q_ref, k_ref, v_ref, z_ref, cos_ref, sin_ref, s0_ref, o_ref, s_ref, s_scr, *, L):
    t = pl.program_id(1)

    @pl.when(t == 0)
    def _():
        s_scr[...] = s0_ref[0]

    nchunk = q_ref.shape[0] // L
    rel = (_iota2((L, L), 0) - _iota2((L, L), 1)).astype(F32)
    tcol = _iota2((L, 1), 0).astype(F32)
    scale = RET_DK ** -0.5
    lgs = [math.log(1.0 - 2.0 ** (-5.0 - h)) for h in range(RET_H)]
    decay = [jnp.where(rel >= 0, jnp.exp(lg * jnp.maximum(rel, 0.0)), 0.0) for lg in lgs]
    rowd = [jnp.exp(lg * (tcol + 1.0)) for lg in lgs]
    cold = [jnp.exp(lg * (L - 1.0 - tcol)) for lg in lgs]

    def rope(x, cos2, sin2):
        return x * cos2 + pltpu.roll(x, RET_DK // 2, 1) * sin2

    U = 4 if nchunk % 4 == 0 else 1
    hs, us = range(RET_H), range(U)
    ksl = [slice(h * RET_DK, (h + 1) * RET_DK) for h in hs]
    vsl = [slice(h * RET_DV, (h + 1) * RET_DV) for h in hs]
    uh = [(u, h) for u in us for h in hs]

    def chunks(i, carry):
        rws = [pl.ds(pl.multiple_of((i * U + u) * L, L), L) for u in us]
        cs = [(cos_ref[r, :], sin_ref[r, :]) for r in rws]
        k = {(u, h): rope(k_ref[rws[u], ksl[h]], *cs[u]) for u, h in uh}
        qb = {(u, h): _bf(rope(q_ref[rws[u], ksl[h]], *cs[u]) * scale) for u, h in uh}
        kb = {x: _bf(k[x]) for x in uh}
        kcb = {(u, h): _bf(k[u, h] * cold[h]) for u, h in uh}
        vb = {(u, h): _bf(v_ref[rws[u], vsl[h]]) for u, h in uh}
        A = {(u, h): _bf(_dot_nt(qb[u, h], kb[u, h]) * decay[h]) for u, h in uh}
        kv = {x: _dot_tn(kcb[x], vb[x]) for x in uh}
        av = {x: _dot(A[x], vb[x]) for x in uh}
        S = [s_scr[h] for h in hs]
        o = {}
        for u in us:
            for h in hs:
                o[u, h] = av[u, h] + rowd[h] * _dot(qb[u, h], _bf(S[h]))
            S = [math.exp(lgs[h] * L) * S[h] + kv[u, h] for h in hs]
        ms = {x: jnp.mean(o[x] * o[x], axis=-1, keepdims=True) for x in uh}
        for u, h in uh:
            o_ref[rws[u], vsl[h]] = _bf(o[u, h] * lax.rsqrt(ms[u, h] + EPS) * _silu(z_ref[rws[u], vsl[h]]))
        for h in hs:
            s_scr[h] = S[h]
        return carry

    lax.fori_loop(0, nchunk // U, chunks, 0)

    @pl.when(t == pl.num_programs(1) - 1)
    def _():
        s_ref[0] = s_scr[...]


def _ret(P, B, T, cos2, sin2, s0):
    L = min(T, CHUNK)
    tt = min(T, SCAN_TT)
    nT = T // tt
    rowblk = lambda w, off: pl.BlockSpec((tt, w), lambda b_, t: (b_ * nT + t, off // w))
    tab = pl.BlockSpec((tt, RET_DK), lambda b_, t: (t, 0))
    st = pl.BlockSpec((1, RET_H, RET_DK, RET_DV), lambda b_, t: (b_, 0, 0, 0))
    return pl.pallas_call(
        functools.partial(_ret_kernel, L=L),
        grid=(B, nT),
        in_specs=[rowblk(512, _OFF['ret_q']), rowblk(512, _OFF['ret_k']), rowblk(BW, _OFF['ret_v']),
                  rowblk(BW, _OFF['ret_z']), tab, tab, st],
        out_specs=[pl.BlockSpec((tt, BW), lambda b_, t: (b_ * nT + t, 0)), st],
        out_shape=[jax.ShapeDtypeStruct((B * T, BW), BF16), jax.ShapeDtypeStruct(s0.shape, F32)],
        scratch_shapes=[pltpu.VMEM((RET_H, RET_DK, RET_DV), F32)],
        compiler_params=_cparams("parallel", "arbitrary"),
        name="ret_scan",
    )(P, P, P, P, cos2, sin2, s0)


def _mlstm_pre_kernel(xm_ref, cp_ref, cw_ref, cb_ref, wq_ref, wk_ref, wv_ref, wif_ref, bif_ref,
                      xc_ref, q_ref, k_ref, v_ref, g_ref, carry_scr):
    t = pl.program_id(1)
    tm = xm_ref.shape[0]
    nprev = ML_CONV - 1

    @pl.when(t == 0)
    def _():
        carry_scr[...] = jnp.zeros_like(carry_scr)
        carry_scr[SUBLANE - nprev:SUBLANE, :] = cp_ref[0]

    xm = xm_ref[...]
    c8 = carry_scr[...]
    rid = _iota2((SUBLANE, 1), 0)
    conv = xm * cw_ref[nprev:nprev + 1, :] + cb_ref[...]
    for j in range(1, ML_CONV):
        rolled = pltpu.roll(xm, j, 0)
        head = jnp.where(rid < j, pltpu.roll(c8, j, 0), rolled[0:SUBLANE, :])
        prev = jnp.concatenate([head, rolled[SUBLANE:, :]], axis=0) if tm > SUBLANE else head
        conv = conv + prev * cw_ref[nprev - j:nprev - j + 1, :]
    carry_scr[...] = xm[tm - SUBLANE:tm, :]
    xc = _silu(conv)
    xc_ref[...] = xc
    xcb, xmb = _bf(xc), _bf(xm)
    q = _dot(xcb, wq_ref[...])
    k = _dot(xcb, wk_ref[...])
    v = _dot(xmb, wv_ref[...])
    g_ref[...] = (_dot(_bf(q), wif_ref[0]) + _dot(_bf(k), wif_ref[1]) + _dot(_bf(v), wif_ref[2])) + bif_ref[...]
    q_ref[...] = q
    k_ref[...] = k * (ML_D ** -0.5)
    v_ref[...] = v


def _mlstm_pre(P, B, T, conv_prev, p):
    tm = min(T, SCAN_TT)
    nT = T // tm
    row = pl.BlockSpec((tm, BW), lambda b_, t: (b_ * nT + t, 0))
    const = lambda shape: pl.BlockSpec(shape, lambda b_, t: (0,) * len(shape))
    M = B * T
    return pl.pallas_call(
        _mlstm_pre_kernel,
        grid=(B, nT),
        in_specs=[pl.BlockSpec((tm, BW), lambda b_, t: (b_ * nT + t, _OFF['ml_x'] // BW)),
                  pl.BlockSpec((1, ML_CONV - 1, BW), lambda b_, t: (b_, 0, 0)),
                  const((ML_CONV, BW)), const((1, BW)), const((BW, BW)), const((BW, BW)), const((BW, BW)),
                  const((3, BW, LANE)), const((1, LANE))],
        out_specs=[row, row, row, row, pl.BlockSpec((tm, LANE), lambda b_, t: (b_ * nT + t, 0))],
        out_shape=[jax.ShapeDtypeStruct((M, BW), F32)] * 4 + [jax.ShapeDtypeStruct((M, LANE), F32)],
        scratch_shapes=[pltpu.VMEM((SUBLANE, BW), F32)],
        compiler_params=_cparams("parallel", "arbitrary"),
        name="mlstm_pre",
    )(P, conv_prev, p['ml_conv_w'], p['ml_conv_b'], p['ml_wq'], p['ml_wk'], p['ml_wv'], p['ml_wif'], p['ml_bif'])


def _mlstm_kernel(q_ref, k_ref, v_ref, gc_ref, gr_ref, xc_ref, z_ref, ng_ref, sk_ref, c0_ref, n0_ref, m0_ref,
                  o_ref, c_ref, n_ref, m_ref, c_scr, n_scr, m_scr, *, L):
    t = pl.program_id(1)

    @pl.when(t == 0)
    def _():
        c_scr[...] = c0_ref[0]
        n_scr[...] = n0_ref[0]
        m_scr[...] = m0_ref[0]

    nchunk = q_ref.shape[0] // L
    row, col = _iota2((L, L), 0), _iota2((L, L), 1)
    tri = (row >= col).astype(BF16)
    triu = (row <= col).astype(BF16)
    causal = row >= col

    U = 2 if nchunk % 2 == 0 else 1
    hs, us = range(ML_H), range(U)
    sl = [slice(h * ML_D, (h + 1) * ML_D) for h in hs]
    uh = [(u, h) for u in us for h in hs]

    def chunks(i, carry):
        cidx = [i * U + u for u in us]
        rws = [pl.ds(pl.multiple_of(c * L, L), L) for c in cidx]
        gcb = [gc_ref[0, c] for c in cidx]
        grb = [gr_ref[0, c] for c in cidx]
        bc = [_cumsum_rows(tri, x) for x in gcb]
        br = [_cumsum_cols(x, triu) for x in grb]
        q = {(u, h): q_ref[rws[u], sl[h]] for u, h in uh}
        k = {(u, h): k_ref[rws[u], sl[h]] for u, h in uh}
        qb = {x: _bf(q[x]) for x in uh}
        vb = {(u, h): _bf(v_ref[rws[u], sl[h]]) for u, h in uh}
        qk = {x: _dot_nt(qb[x], _bf(k[x])) for x in uh}
        b_c = {(u, h): bc[u][:, ML_H + h:ML_H + h + 1] for u, h in uh}
        dlog = {(u, h): jnp.where(causal, b_c[u, h] - br[u][ML_H + h:ML_H + h + 1, :] + grb[u][h:h + 1, :], -jnp.inf)
                for u, h in uh}
        m_loc = {x: jnp.max(dlog[x], axis=-1, keepdims=True) for x in uh}
        s_loc = {x: qk[x] * jnp.exp(dlog[x] - m_loc[x]) for x in uh}
        rs = {x: jnp.sum(s_loc[x], axis=-1, keepdims=True) for x in uh}
        b_end = {x: b_c[x][L - 1:L, :] for x in uh}
        m_le = {x: m_loc[x][L - 1:L, :] for x in uh}
        kw = {(u, h): k[u, h] * jnp.exp(b_end[u, h] - b_c[u, h] + gcb[u][:, h:h + 1] - m_le[u, h]) for u, h in uh}
        sv = {x: _dot(_bf(s_loc[x]), vb[x]) for x in uh}
        kv = {x: _dot_tn(_bf(kw[x]), vb[x]) for x in uh}
        ksum = {x: jnp.sum(kw[x], axis=0, keepdims=True) for x in uh}
        C = [c_scr[h] for h in hs]
        nvec = [n_scr[h] for h in hs]
        m_prev = [m_scr[h][:, 0:1] for h in hs]
        hh = {}
        for u in us:
            qc = [_dot(qb[u, h], _bf(C[h])) for h in hs]
            qn = [jnp.sum(q[u, h] * nvec[h], axis=-1, keepdims=True) for h in hs]
            from_state = [b_c[u, h] + m_prev[h] for h in hs]
            m_t = [jnp.maximum(from_state[h], m_loc[u, h]) for h in hs]
            a_in = [jnp.exp(m_loc[u, h] - m_t[h]) for h in hs]
            w_state = [jnp.exp(from_state[h] - m_t[h]) for h in hs]
            den = [a_in[h] * rs[u, h] + w_state[h] * qn[h] for h in hs]
            for h in hs:
                hh[u, h] = ((a_in[h] * sv[u, h] + w_state[h] * qc[h])
                            / jnp.maximum(jnp.abs(den[h]), jnp.exp(-m_t[h])))
            m_new = [m_t[h][L - 1:L, :] for h in hs]
            cd = [jnp.exp(b_end[u, h] + m_prev[h] - m_new[h]) for h in hs]
            sc = [jnp.exp(m_le[u, h] - m_new[h]) for h in hs]
            C = [cd[h] * C[h] + sc[h] * kv[u, h] for h in hs]
            nvec = [cd[h] * nvec[h] + sc[h] * ksum[u, h] for h in hs]
            m_prev = m_new
        mean = {x: jnp.mean(hh[x], axis=-1, keepdims=True) for x in uh}
        cen = {x: hh[x] - mean[x] for x in uh}
        var = {x: jnp.mean(cen[x] * cen[x], axis=-1, keepdims=True) for x in uh}
        for u, h in uh:
            y = cen[u, h] * lax.rsqrt(var[u, h] + EPS)
            o_ref[rws[u], sl[h]] = _bf((y * ng_ref[:, sl[h]] + sk_ref[:, sl[h]] * xc_ref[rws[u], sl[h]])
                                       * _silu(z_ref[rws[u], sl[h]]))
        for h in hs:
            c_scr[h] = C[h]
            n_scr[h] = nvec[h]
            m_scr[h] = jnp.broadcast_to(m_prev[h], (1, LANE))
        return carry

    lax.fori_loop(0, nchunk // U, chunks, 0)

    @pl.when(t == pl.num_programs(1) - 1)
    def _():
        c_ref[0] = c_scr[...]
        n_ref[0] = n_scr[...]
        m_ref[0] = m_scr[...]


def _mlstm(q, k, v, gc, gr, xc, P, ng, sk, c0, n0, m0, B, T):
    L = min(T, CHUNK)
    tt = min(T, SCAN_TT)
    nT = T // tt
    nc = tt // L
    row = pl.BlockSpec((tt, BW), lambda b_, t: (b_ * nT + t, 0))
    const = pl.BlockSpec((1, BW), lambda b_, t: (0, 0))
    stc = pl.BlockSpec((1, ML_H, ML_D, ML_D), lambda b_, t: (b_, 0, 0, 0))
    stn = pl.BlockSpec((1, ML_H, 1, ML_D), lambda b_, t: (b_, 0, 0, 0))
    stm = pl.BlockSpec((1, ML_H, 1, LANE), lambda b_, t: (b_, 0, 0, 0))
    return pl.pallas_call(
        functools.partial(_mlstm_kernel, L=L),
        grid=(B, nT),
        in_specs=[row, row, row,
                  pl.BlockSpec((1, nc, L, 8), lambda b_, t: (b_, t, 0, 0)),
                  pl.BlockSpec((1, nc, 8, L), lambda b_, t: (b_, t, 0, 0)),
                  row, pl.BlockSpec((tt, BW), lambda b_, t: (b_ * nT + t, _OFF['ml_z'] // BW)), const, const, stc, stn, stm],
        out_specs=[row, stc, stn, stm],
        out_shape=[jax.ShapeDtypeStruct((B * T, BW), BF16), jax.ShapeDtypeStruct(c0.shape, F32),
                   jax.ShapeDtypeStruct(n0.shape, F32), jax.ShapeDtypeStruct(m0.shape, F32)],
        scratch_shapes=[pltpu.VMEM((ML_H, ML_D, ML_D), F32), pltpu.VMEM((ML_H, 1, ML_D), F32),
                        pltpu.VMEM((ML_H, 1, LANE), F32)],
        compiler_params=_cparams("parallel", "arbitrary"),
        name="mlstm_scan",
    )(q, k, v, gc, gr, xc, P, ng, sk, c0, n0, m0)


def _xattn_kernel(q_ref, z_ref, mk_ref, mv_ref, o_ref):
    hs = range(XA_H)
    sl = [slice(h * XA_D, (h + 1) * XA_D) for h in hs]
    s = [_dot_nt(_bf(q_ref[:, x]), _bf(mk_ref[0, :, x])) * (XA_D ** -0.5) for x in sl]
    e = [jnp.exp(x - jnp.max(x, axis=-1, keepdims=True)) for x in s]
    prob = [_bf(x / jnp.sum(x, axis=-1, keepdims=True)) for x in e]
    o = [_dot(prob[h], _bf(mv_ref[0, :, sl[h]])) for h in hs]
    for h in hs:
        o_ref[:, sl[h]] = _bf(o[h] * _silu(z_ref[:, sl[h]]))


def _xattn(P, B, T, mk, mv):
    tt = min(T, SCAN_TT)
    nT = T // tt
    n_mem = mk.shape[1]
    mem = pl.BlockSpec((1, n_mem, BW), lambda b_, t: (b_, 0, 0))
    return pl.pallas_call(
        _xattn_kernel,
        grid=(B, nT),
        in_specs=[pl.BlockSpec((tt, BW), lambda b_, t: (b_ * nT + t, _OFF['xa_q'] // BW)),
                  pl.BlockSpec((tt, BW), lambda b_, t: (b_ * nT + t, _OFF['xa_z'] // BW)), mem, mem],
        out_specs=pl.BlockSpec((tt, BW), lambda b_, t: (b_ * nT + t, 0)),
        out_shape=jax.ShapeDtypeStruct((B * T, BW), BF16),
        compiler_params=_cparams("parallel", "parallel"),
        name="xattn",
    )(P, P, mk, mv)


def _merge_kernel(o0, o1, o2, o3, o4, g_ref, w_ref, out_ref):
    ys = [_dot(o_ref[...], w_ref[i]) for i, o_ref in enumerate((o0, o1, o2, o3, o4))]
    acc = None
    for i, y in enumerate(ys):
        term = _sigmoid(g_ref[:, i * D_MODEL:(i + 1) * D_MODEL]) * y
        acc = term if acc is None else acc + term
    out_ref[...] = _bf(acc)


def _merge(ogs, P, wb, tm):
    M = ogs[0].shape[0]
    gw = N_BRANCH * D_MODEL
    og = pl.BlockSpec((tm, BW), lambda i: (i, 0))
    return pl.pallas_call(
        _merge_kernel,
        grid=(M // tm,),
        in_specs=[og] * N_BRANCH + [pl.BlockSpec((tm, gw), lambda i: (i, _OFF['gates'] // gw)),
                                    pl.BlockSpec((N_BRANCH, BW, D_MODEL), lambda i: (0, 0, 0),
                                                 pipeline_mode=pl.Buffered(1))],
        out_specs=pl.BlockSpec((tm, D_MODEL), lambda i: (i, 0)),
        out_shape=jax.ShapeDtypeStruct((M, D_MODEL), BF16),
        compiler_params=_cparams("parallel"),
        name="merge",
    )(*ogs, P, wb)


def _resid_kernel(x_ref, m_ref, w_ref, o_ref):
    o_ref[...] = x_ref[...] + _dot(m_ref[...], w_ref[...])


def _resid_norm_kernel(x_ref, m_ref, w_ref, g_ref, o_ref):
    y = x_ref[...] + _dot(m_ref[...], w_ref[...])
    ms = jnp.mean(y * y, axis=-1, keepdims=True)
    o_ref[...] = (y * lax.rsqrt(ms + EPS)) * g_ref[...]


def _resid_out(x, merged, w, tm, final_g=None):
    M = x.shape[0]
    row = pl.BlockSpec((tm, D_MODEL), lambda i: (i, 0))
    in_specs = [row, row, pl.BlockSpec((D_MODEL, D_MODEL), lambda i: (0, 0), pipeline_mode=pl.Buffered(1))]
    args = [x, merged, w]
    if final_g is not None:
        in_specs.append(pl.BlockSpec((1, D_MODEL), lambda i: (0, 0)))
        args.append(final_g)
    return pl.pallas_call(
        _resid_kernel if final_g is None else _resid_norm_kernel,
        grid=(M // tm,),
        in_specs=in_specs,
        out_specs=row,
        out_shape=jax.ShapeDtypeStruct((M, D_MODEL), F32),
        compiler_params=_cparams("parallel"),
        name="resid_out",
    )(*args)


def _blockdiag(w):
    depth, n, c, d = w.shape
    rows = w.reshape(depth, n * c, d)
    tile = (jnp.arange(d)[:, None] == jnp.arange(n * d)[None, :] % d).astype(w.dtype)
    dense = jnp.einsum('lrd,dj->lrj', rows, tile)
    mask = (jnp.arange(n * c)[:, None] // c) == (jnp.arange(n * d)[None, :] // d)
    return jnp.where(mask, dense, 0.0).astype(BF16)


def _rope_tables(start, T):
    half = RET_DK // 2
    inv = ROPE_BASE ** (-jnp.linspace(0.0, 1.0, half, dtype=F32))
    blk = ROPE_BLK
    if T % blk or T <= blk:
        ang = (start + jnp.arange(T, dtype=F32))[:, None] * inv[None, :]
        cos, sin = jnp.cos(ang), jnp.sin(ang)
    else:
        hi = (start + blk * jnp.arange(T // blk, dtype=F32))[:, None] * inv[None, :]
        lo = jnp.arange(blk, dtype=F32)[:, None] * inv[None, :]
        ch, sh, cl, sl_ = jnp.cos(hi)[:, None, :], jnp.sin(hi)[:, None, :], jnp.cos(lo)[None], jnp.sin(lo)[None]
        cos = (ch * cl - sh * sl_).reshape(T, half)
        sin = (sh * cl + ch * sl_).reshape(T, half)
    return jnp.concatenate([cos, cos], axis=-1), jnp.concatenate([-sin, sin], axis=-1)


def _layer(x, B, T, tabs, st, mem_k, mem_v, p, final_g=None):
    M = B * T
    tm = min(M, PROJ_TM)
    P = _normproj(x, p['norm_g'], p['w_pack'], tm, PROJ_TN)
    P3 = P.reshape(B, T, NP)

    og_a, s_rwkv = _rwkv(P, B, T, p, st['rwkv'], st['rwkv_shift'])
    shift_new = P3[:, T - 1:, _OFF['rwkv_shift']:_OFF['rwkv_shift'] + RWKV_SHIFT_W]

    og_b, s_gla_t = _gla(P, B, T, p['gla_g2p'], p['gla_gb'], p['gla_norm_g'], jnp.swapaxes(st['gla'], -1, -2))
    s_gla = jnp.swapaxes(s_gla_t, -1, -2)

    og_c, s_ret = _ret(P, B, T, tabs[0], tabs[1], st['ret'])

    xc, q_m, k_m, v_m, gates = _mlstm_pre(P, B, T, st['ml_conv'], p)
    L = min(T, CHUNK)
    gcol = jnp.concatenate([gates[:, :ML_H], _log_sigmoid(gates[:, ML_H:2 * ML_H])], axis=-1).reshape(B, T // L, L, 2 * ML_H)
    grow = jnp.swapaxes(gcol, -1, -2)
    og_d, c_new, n_new, m_new = _mlstm(
        q_m, k_m, v_m, gcol, grow, xc, P, p['ml_norm_g'], p['ml_skip'], st['ml_c'], st['ml_n'][:, :, None, :],
        jnp.broadcast_to(st['ml_m'][:, :, None, None], st['ml_m'].shape + (1, LANE)), B, T)
    xm_tail = P3[:, max(T - (ML_CONV - 1), 0):, _OFF['ml_x']:_OFF['ml_x'] + BW]
    conv_new = jnp.concatenate([st['ml_conv'], xm_tail], axis=1)[:, -(ML_CONV - 1):]

    og_x = _xattn(P, B, T, mem_k.reshape(B, -1, BW), mem_v.reshape(B, -1, BW))

    merged = _merge([og_a, og_b, og_c, og_d, og_x], P, p['w_branch'], min(M, MERGE_TM))
    x_new = _resid_out(x, merged, p['w_out'], min(M, RESID_TM), final_g)
    new = {'rwkv': s_rwkv, 'rwkv_shift': shift_new, 'gla': s_gla, 'ret': s_ret, 'ml_c': c_new,
           'ml_n': n_new[:, :, 0, :], 'ml_m': m_new[:, :, 0, 0], 'ml_conv': conv_new}
    return x_new, new


_STATE_KEYS = ('rwkv', 'rwkv_shift', 'gla', 'ret', 'ml_c', 'ml_n', 'ml_m', 'ml_conv')


def kernel(x_prompt, x_sample, mem_prompt, cache_mem_k, cache_mem_v, state_rwkv, state_rwkv_shift, state_gla, state_ret, state_mlstm_c, state_mlstm_n, state_mlstm_m, state_mlstm_conv, norm_g, mem_norm_g, w_in, w_mem_kv, rwkv_mu, rwkv_w0, rwkv_w2, rwkv_a0, rwkv_a2, rwkv_k_k, rwkv_k_a, rwkv_r_k, rwkv_ln_g, gla_g2, gla_gb, gla_norm_g, ml_conv_w, ml_conv_b, ml_wq, ml_wk, ml_wv, ml_w_if, ml_b_if, ml_skip, ml_norm_g, w_branch, w_out, final_norm_g):
    Bp, Tp, D = x_prompt.shape
    Bs, Ts, _ = x_sample.shape
    depth = w_in.shape[0]
    n_mem = mem_prompt.shape[1]

    params = {
        'norm_g': norm_g[:, None, :], 'mem_norm_g': mem_norm_g[:, None, :],
        'w_mem_kv': _bf(w_mem_kv),
        'rwkv_mu': rwkv_mu[:, None, :], 'rwkv_w0': rwkv_w0[:, None, :], 'rwkv_a0': rwkv_a0[:, None, :],
        'rwkv_w2p': _bf(jnp.pad(rwkv_w2, ((0, 0), (0, LANE - RWKV_LORA), (0, 0)))),
        'rwkv_a2p': _bf(jnp.pad(rwkv_a2, ((0, 0), (LANE - RWKV_LORA, 0), (0, 0)))),
        'rwkv_k_k': rwkv_k_k[:, None, :], 'rwkv_k_a': rwkv_k_a[:, None, :], 'rwkv_r_k': rwkv_r_k[:, None, :],
        'rwkv_ln_g': rwkv_ln_g[:, None, :],
        'gla_g2p': _bf(jnp.pad(gla_g2, ((0, 0), (0, LANE - GLA_LORA), (0, 0)))), 'gla_gb': gla_gb[:, None, :],
        'gla_norm_g': gla_norm_g[:, None, :],
        'ml_conv_w': ml_conv_w, 'ml_conv_b': ml_conv_b[:, None, :],
        'ml_wq': _blockdiag(ml_wq), 'ml_wk': _blockdiag(ml_wk), 'ml_wv': _blockdiag(ml_wv),
        'ml_wif': _bf(jnp.pad(ml_w_if.reshape(depth, 3, BW, 2 * ML_H), ((0, 0), (0, 0), (0, 0), (0, LANE - 2 * ML_H)))),
        'ml_bif': jnp.pad(ml_b_if, ((0, 0), (0, LANE - 2 * ML_H)))[:, None, :],
        'ml_skip': ml_skip[:, None, :], 'ml_norm_g': ml_norm_g[:, None, :],
        'w_branch': _bf(w_branch), 'w_out': _bf(w_out),
    }
    cache = {'rwkv': state_rwkv, 'rwkv_shift': state_rwkv_shift, 'gla': state_gla, 'ret': state_ret,
             'ml_c': state_mlstm_c, 'ml_n': state_mlstm_n, 'ml_m': state_mlstm_m, 'ml_conv': state_mlstm_conv,
             'mem_k': cache_mem_k, 'mem_v': cache_mem_v}
    tabs_p = _rope_tables(0.0, Tp)
    tabs_s = _rope_tables(float(PAST_LEN), Ts)
    zero_p = {
        'rwkv': jnp.zeros((Bp, RWKV_H, RWKV_D, RWKV_D), F32), 'rwkv_shift': jnp.zeros((Bp, 1, RWKV_SHIFT_W), F32),
        'gla': jnp.zeros((Bp, GLA_H, GLA_DK, GLA_DV), F32), 'ret': jnp.zeros((Bp, RET_H, RET_DK, RET_DV), F32),
        'ml_c': jnp.zeros((Bp, ML_H, ML_D, ML_D), F32), 'ml_n': jnp.zeros((Bp, ML_H, ML_D), F32),
        'ml_m': jnp.zeros((Bp, ML_H), F32), 'ml_conv': jnp.zeros((Bp, ML_CONV - 1, BW), F32),
    }
    mem2d = mem_prompt.reshape(Bp * n_mem, D)

    yp, ys = x_prompt.reshape(Bp * Tp, D), x_sample.reshape(Bs * Ts, D)
    outs = []
    for l in range(depth):
        p = {nm: arr[l] for nm, arr in params.items()}
        p['w_pack'] = _pack_w_in(w_in, l)
        kv = _normproj(mem2d, p['mem_norm_g'], p['w_mem_kv'], min(Bp * n_mem, MEM_TM), MEM_TN)
        mk_l = kv[:, :BW].reshape(Bp, n_mem, XA_H, XA_D)
        mv_l = kv[:, BW:].reshape(Bp, n_mem, XA_H, XA_D)
        fg = final_norm_g[None, :] if l == depth - 1 else None
        yp, stp_l = _layer(yp, Bp, Tp, tabs_p, zero_p, mk_l, mv_l, p, fg)
        ys, sts_l = _layer(ys, Bs, Ts, tabs_s, {nm: cache[nm][l] for nm in _STATE_KEYS},
                           cache['mem_k'][l], cache['mem_v'][l], p, fg)
        outs.append((stp_l, mk_l, mv_l, sts_l))
    stp = {nm: jnp.stack([o[0][nm] for o in outs]) for nm in _STATE_KEYS}
    sts = {nm: jnp.stack([o[3][nm] for o in outs]) for nm in _STATE_KEYS}
    mk = jnp.stack([o[1] for o in outs])
    mv = jnp.stack([o[2] for o in outs])
    y_prompt = yp.reshape(Bp, Tp, D)
    y_sample = ys.reshape(Bs, Ts, D)
    return (y_prompt, y_sample,
            stp['rwkv'], stp['rwkv_shift'], stp['gla'], stp['ret'], stp['ml_c'], stp['ml_n'], stp['ml_m'],
            stp['ml_conv'], mk, mv,
            sts['rwkv'], sts['rwkv_shift'], sts['gla'], sts['ret'], sts['ml_c'], sts['ml_n'], sts['ml_m'],
            sts['ml_conv'])
```

```python
import functools
import math

import jax
import jax.numpy as jnp
from jax import lax
from jax.experimental import pallas as pl
from jax.experimental.pallas import tpu as pltpu

F32 = jnp.float32
BF16 = jnp.bfloat16

D_MODEL = 2048
BW = 1024
EPS = 1e-6
CHUNK = 64
N_BRANCH = 5
PAST_LEN = 2048
RWKV_H, RWKV_D, RWKV_LORA = 16, 64, 64
RWKV_SHIFT_W = 3 * BW + 2 * RWKV_LORA
RWKV_DECAY_SCALE = 0.606531
GLA_H, GLA_DK, GLA_DV, GLA_LORA = 4, 128, 256, 16
GLA_GATE_NORM = 16.0
RET_H, RET_DK, RET_DV = 4, 128, 256
ROPE_BASE = 10000.0
ML_H, ML_D, ML_CONV = 4, 256, 4
XA_H, XA_D = 4, 256
LANE = 128
SUBLANE = 8
VMEM_LIMIT = 56 * 1024 * 1024

PROJ_TM = 1024
MEM_TM, MEM_TN = 512, 512
RWKV_TT = 256
SCAN_TT = 512
MERGE_TM = 256
RESID_TM = 512
ROPE_BLK = 128
PACK_TR = 128

_IN_LAYOUT = (
    ('rwkv_shift', RWKV_SHIFT_W), ('rwkv_z', BW), ('gla_q', 512), ('gla_k', 512), ('gla_v', BW),
    ('gla_gd', GLA_LORA), ('gla_z', BW), ('ret_q', 512), ('ret_k', 512), ('ret_v', BW), ('ret_z', BW),
    ('ml_x', BW), ('ml_z', BW), ('xa_q', BW), ('xa_z', BW), ('gates', N_BRANCH * D_MODEL),
)
_PACK_ORDER = ('gates', 'rwkv_z', 'gla_z', 'ret_z', 'ml_z', 'xa_z', 'ret_q', 'ret_k', 'ret_v',
               'gla_q', 'gla_k', 'gla_v', 'ml_x', 'xa_q', 'rwkv_shift', 'gla_gd')
PROJ_TN = 1792


def _src_cols(name):
    start = 0
    for nm, size in _IN_LAYOUT:
        if nm == name:
            return start, start + size
        start += size
    raise KeyError(name)


def _pack_offsets():
    off, cur = {}, 0
    for nm in _PACK_ORDER:
        a, b = _src_cols(nm)
        off[nm] = cur
        cur += -(-(b - a) // LANE) * LANE
    total = -(-cur // PROJ_TN) * PROJ_TN
    return off, total


_OFF, NP = _pack_offsets()


def _pack_kernel(w_ref, o_ref):
    cur = 0
    for nm in _PACK_ORDER:
        a, b = _src_cols(nm)
        o_ref[:, cur:cur + (b - a)] = w_ref[0, :, a:b].astype(BF16)
        width = -(-(b - a) // LANE) * LANE
        if width != b - a:
            o_ref[:, cur + (b - a):cur + width] = jnp.zeros((o_ref.shape[0], width - (b - a)), BF16)
        cur += width
    if NP != cur:
        o_ref[:, cur:NP] = jnp.zeros((o_ref.shape[0], NP - cur), BF16)


def _pack_w_in(w_in, layer):
    _, K, n_in = w_in.shape
    return pl.pallas_call(
        _pack_kernel,
        grid=(K // PACK_TR,),
        in_specs=[pl.BlockSpec((1, PACK_TR, n_in), lambda i: (layer, i, 0))],
        out_specs=pl.BlockSpec((PACK_TR, NP), lambda i: (i, 0)),
        out_shape=jax.ShapeDtypeStruct((K, NP), BF16),
        compiler_params=_cparams("parallel"),
        name="pack_w_in",
    )(w_in)


def _cparams(*sem):
    return pltpu.CompilerParams(dimension_semantics=sem, vmem_limit_bytes=VMEM_LIMIT)


def _dot(a, b):
    return jnp.dot(a, b, preferred_element_type=F32)


def _dot_nt(a, b):
    return lax.dot_general(a, b, (((1,), (1,)), ((), ())), preferred_element_type=F32)


def _dot_tn(a, b):
    return lax.dot_general(a, b, (((0,), (0,)), ((), ())), preferred_element_type=F32)


def _bf(x):
    return x.astype(BF16)


def _split3(x):
    hi = x.astype(BF16)
    r1 = x - hi.astype(F32)
    mid = r1.astype(BF16)
    lo = (r1 - mid.astype(F32)).astype(BF16)
    return hi, mid, lo


def _cumsum_rows(tri, x):
    hi, mid, lo = _split3(x)
    return _dot(tri, hi) + _dot(tri, mid) + _dot(tri, lo)


def _cumsum_cols(x, triu):
    hi, mid, lo = _split3(x)
    return _dot(hi, triu) + _dot(mid, triu) + _dot(lo, triu)


def _iota2(shape, dim):
    return lax.broadcasted_iota(jnp.int32, shape, dim)


def _log_sigmoid(x):
    return jnp.minimum(x, 0.0) - jnp.log1p(jnp.exp(-jnp.abs(x)))


def _sigmoid(x):
    return 1.0 / (1.0 + jnp.exp(-x))


def _silu(x):
    return x * _sigmoid(x)


def _normproj_kernel(x_ref, g_ref, w_ref, o_ref, h_ref):
    @pl.when(pl.program_id(1) == 0)
    def _():
        x = x_ref[...]
        ms = jnp.mean(x * x, axis=-1, keepdims=True)
        h_ref[...] = ((x * lax.rsqrt(ms + EPS)) * g_ref[...]).astype(BF16)

    o_ref[...] = _dot(h_ref[...], w_ref[...])


def _normproj(x, g, w, tm, tn):
    M, K = x.shape
    N = w.shape[1]
    return pl.pallas_call(
        _normproj_kernel,
        grid=(M // tm, N // tn),
        in_specs=[pl.BlockSpec((tm, K), lambda i, j: (i, 0)),
                  pl.BlockSpec((1, K), lambda i, j: (0, 0)),
                  pl.BlockSpec((K, tn), lambda i, j: (0, j))],
        out_specs=pl.BlockSpec((tm, tn), lambda i, j: (i, j)),
        out_shape=jax.ShapeDtypeStruct((M, N), F32),
        scratch_shapes=[pltpu.VMEM((tm, K), BF16)],
        compiler_params=_cparams("parallel", "arbitrary"),
        name="normproj",
    )(x, g, w)


def _rwkv_kernel(xr_ref, xk_ref, xv_ref, xwa_ref, z_ref, sp_ref, mu_ref, w2_ref, a2_ref, w0_ref, a0_ref,
                 kkw_ref, ka_ref, rk_ref, lng_ref, s0_ref, og_ref, s_ref,
                 s_scr, phi_scr, psi_scr, qt_scr, egl_scr, carry_scr,
                 r_scr, lw_scr, k_scr, v_scr, kk_scr, b_scr, y_scr, *, L):
    t = pl.program_id(1)
    HG, D = RWKV_H, RWKV_D
    hs = range(HG)
    sls = [slice(h * D, (h + 1) * D) for h in hs]

    @pl.when(t == 0)
    def _():
        s_scr[...] = s0_ref[0]
        carry_scr[...] = sp_ref[0]

    def cat(parts):
        return jnp.concatenate(parts, axis=1)

    tt = xr_ref.shape[0]
    first = _iota2((tt, 1), 0) == 0

    def mix(x_ref, lo, hi):
        x = x_ref[...]
        prev = jnp.where(first, carry_scr[:, lo:hi], pltpu.roll(x, 1, 0))
        carry_scr[:, lo:hi] = x[tt - 1:tt, :]
        return x + (prev - x) * mu_ref[:, lo:hi]

    r_ = mix(xr_ref, 0, BW)
    k_ = mix(xk_ref, BW, 2 * BW)
    v_ = mix(xv_ref, 2 * BW, 3 * BW)
    wa = mix(xwa_ref, 3 * BW, RWKV_SHIFT_W)
    lw_scr[...] = -RWKV_DECAY_SCALE * _sigmoid(w0_ref[...] + _dot(_bf(jnp.tanh(wa)), w2_ref[...]))
    a_ = _sigmoid(a0_ref[...] + _dot(_bf(wa), a2_ref[...]))
    ones_blk = (_iota2((2 * LANE, LANE), 0) % LANE // D == _iota2((2 * LANE, LANE), 1) // D).astype(BF16)

    def headsum(x):
        hi = _bf(x)
        lo = _bf(x - hi.astype(F32))
        return cat([_dot(jnp.concatenate([hi[:, j:j + LANE], lo[:, j:j + LANE]], axis=1), ones_blk)
                    for j in range(0, BW, LANE)])

    kkf = k_ * kkw_ref[...]
    kk = kkf * lax.rsqrt(jnp.maximum(headsum(kkf * kkf), 1e-24))
    r_scr[...] = r_
    k_scr[...] = k_ * (1.0 + (a_ - 1.0) * ka_ref[...])
    v_scr[...] = v_
    kk_scr[...] = kk
    b_scr[...] = kk * a_

    nchunk = tt // L
    row, col = _iota2((L, L), 0), _iota2((L, L), 1)
    tri = (row >= col).astype(BF16)
    strict = row > col
    incl2 = _iota2((L, 2 * L), 0) >= _iota2((L, 2 * L), 1) % L
    eye = (row == col).astype(F32)

    U1 = 4 if nchunk % 4 == 0 else 1
    it = range(U1 * HG)
    isl = [sls[i % HG] for i in it]

    def phase1(ci, carry):
        cidx = [ci * U1 + u for u in range(U1)]
        rws = [pl.ds(pl.multiple_of(c * L, L), L) for c in cidx]
        kkp, rg, kn, bn, kend, bend, v_c = [], [], [], [], [], [], []
        for u, c in enumerate(cidx):
            lw, k_c, b_c = lw_scr[rws[u], :], k_scr[rws[u], :], b_scr[rws[u], :]
            g = _cumsum_rows(tri, lw)
            gl = g[L - 1:L, :]
            kkp.append(kk_scr[rws[u], :] * jnp.exp(g - lw))
            rg.append(r_scr[rws[u], :] * jnp.exp(g))
            eng = jnp.exp(-g)
            kn.append(k_c * eng)
            bn.append(b_c * eng)
            ee = jnp.exp(gl - g)
            kend.append(k_c * ee)
            bend.append(b_c * ee)
            v_c.append(v_scr[rws[u], :])
            egl_scr[c] = jnp.exp(gl)
        kkp_h = [kkp[i // HG][:, isl[i]] for i in it]
        rg_h = [rg[i // HG][:, isl[i]] for i in it]
        bk = [_bf(jnp.concatenate([bn[i // HG][:, isl[i]], kn[i // HG][:, isl[i]]], axis=0)) for i in it]
        kr = [_bf(jnp.concatenate([kkp_h[i], rg_h[i]], axis=0)) for i in it]
        mnaq = [_dot_nt(kr[i], bk[i]) for i in it]
        mn = [x[:L, :] for x in mnaq]
        aq = [x[L:, :] for x in mnaq]
        N = [_bf(jnp.where(strict, m[:, L:], 0.0)) for m in mn]
        vb = [_bf(v_c[i // HG][:, isl[i]]) for i in it]
        nv = [_dot(N[i], vb[i]) for i in it]
        X = [jnp.where(strict, -m[:, :L], 0.0) for m in mn]
        Xb = [_bf(x) for x in X]
        P = [_dot(xb, xb) for xb in Xb]
        tinv = [eye + x for x in X]
        n = 2
        while n < L:
            Pb = [_bf(p) for p in P]
            if 2 * n >= L:
                tinv = [tinv[i] + _dot(_bf(tinv[i]), Pb[i]) for i in it]
            else:
                st = [_dot(_bf(jnp.concatenate([P[i], tinv[i]], axis=0)), Pb[i]) for i in it]
                P = [x[:L, :] for x in st]
                tinv = [tinv[i] + st[i][L:, :] for i in it]
            n *= 2
        gh = [_dot(_bf(tinv[i]), _bf(jnp.concatenate([kkp_h[i], nv[i]], axis=1))) for i in it]
        gm = [_bf(x[:, :D]) for x in gh]
        hm = [_bf(x[:, D:]) for x in gh]
        bend_b = [_bf(bend[i // HG][:, isl[i]]) for i in it]
        aqm = [_bf(jnp.where(incl2, a, 0.0)) for a in aq]
        hv = [jnp.concatenate([-hm[i], vb[i]], axis=0) for i in it]
        bke = [jnp.concatenate([bend_b[i], _bf(kend[i // HG][:, isl[i]])], axis=0) for i in it]
        phi = [_dot_tn(gm[i], bend_b[i]) for i in it]
        psi = [_dot_tn(hv[i], bke[i]) for i in it]
        qts = [rg_h[i] - _dot(aqm[i][:, :L], gm[i]) for i in it]
        y0s = [_dot(aqm[i], hv[i]) for i in it]
        for i in it:
            phi_scr[cidx[i // HG], i % HG] = _bf(phi[i])
            psi_scr[cidx[i // HG], i % HG] = psi[i]
        for u in range(U1):
            qt_scr[rws[u], :] = _bf(cat(qts[u * HG:(u + 1) * HG]))
            y_scr[rws[u], :] = cat(y0s[u * HG:(u + 1) * HG])
        return carry

    lax.fori_loop(0, nchunk // U1, phase1, 0)

    S = [s_scr[h] for h in hs]
    for c in range(nchunk):
        rows = slice(c * L, (c + 1) * L)
        Sb = [_bf(s_) for s_ in S]
        sphi = [_dot(Sb[h], phi_scr[c, h]) for h in hs]
        ys = [_dot_nt(qt_scr[rows, sls[h]], Sb[h]) for h in hs]
        egl = egl_scr[c]
        S = [S[h] * egl[:, sls[h]] - sphi[h] + psi_scr[c, h] for h in hs]
        y_scr[rows, :] += cat(ys)
    for h in hs:
        s_scr[h] = S[h]

    y = y_scr[...]
    rk = r_scr[...] * k_scr[...] * rk_ref[...]
    cen = y - headsum(y) * (1.0 / D)
    ln = cen * lax.rsqrt(headsum(cen * cen) * (1.0 / D) + EPS)
    out = ln * lng_ref[...] + headsum(rk) * v_scr[...]
    og_ref[...] = _bf(out * _silu(z_ref[...]))

    @pl.when(t == pl.num_programs(1) - 1)
    def _():
        s_ref[0] = s_scr[...]


def _rwkv(P, B, T, p, s0, shift_prev):
    L = min(T, CHUNK)
    tt = min(T, RWKV_TT)
    nT = T // tt
    nc = tt // L
    off = _OFF['rwkv_shift']
    rowblk = lambda w, o: pl.BlockSpec((tt, w), lambda b_, t: (b_ * nT + t, o // w))
    const = lambda shape: pl.BlockSpec(shape, lambda b_, t: (0,) * len(shape))
    st = pl.BlockSpec((1, RWKV_H, RWKV_D, RWKV_D), lambda b_, t: (b_, 0, 0, 0))
    vec = const((1, BW))
    return pl.pallas_call(
        functools.partial(_rwkv_kernel, L=L),
        grid=(B, nT),
        in_specs=[rowblk(BW, off), rowblk(BW, off + BW), rowblk(BW, off + 2 * BW), rowblk(LANE, off + 3 * BW),
                  rowblk(BW, _OFF['rwkv_z']),
                  pl.BlockSpec((1, 1, RWKV_SHIFT_W), lambda b_, t: (b_, 0, 0)), const((1, RWKV_SHIFT_W)),
                  const((LANE, BW)), const((LANE, BW)), vec, vec, vec, vec, vec, vec, st],
        out_specs=[pl.BlockSpec((tt, BW), lambda b_, t: (b_ * nT + t, 0)), st],
        out_shape=[jax.ShapeDtypeStruct((B * T, BW), BF16), jax.ShapeDtypeStruct(s0.shape, F32)],
        scratch_shapes=[pltpu.VMEM((RWKV_H, RWKV_D, RWKV_D), F32),
                        pltpu.VMEM((nc, RWKV_H, RWKV_D, RWKV_D), BF16),
                        pltpu.VMEM((nc, RWKV_H, RWKV_D, RWKV_D), F32),
                        pltpu.VMEM((tt, BW), BF16),
                        pltpu.VMEM((nc, 1, BW), F32),
                        pltpu.VMEM((1, RWKV_SHIFT_W), F32)] + [pltpu.VMEM((tt, BW), F32)] * 7,
        compiler_params=_cparams("parallel", "arbitrary"),
        name="rwkv_scan",
    )(P, P, P, P, P, shift_prev, p['rwkv_mu'], p['rwkv_w2p'], p['rwkv_a2p'], p['rwkv_w0'], p['rwkv_a0'],
      p['rwkv_k_k'], p['rwkv_k_a'], p['rwkv_r_k'], p['rwkv_ln_g'], s0)


def _gla_kernel(q_ref, k_ref, v_ref, gd_ref, z_ref, g2_ref, gb_ref, ng_ref, s0_ref, o_ref, s_ref, s_scr, g_scr, *, L):
    t = pl.program_id(1)

    @pl.when(t == 0)
    def _():
        s_scr[...] = s0_ref[0]

    gk = _dot(_bf(gd_ref[...]), g2_ref[...]) + gb_ref[...]
    g_scr[...] = _log_sigmoid(gk) * (1.0 / GLA_GATE_NORM)

    nchunk = q_ref.shape[0] // L
    row, col = _iota2((L, L), 0), _iota2((L, L), 1)
    tri = (row >= col).astype(BF16)
    causal = row >= col
    scale = GLA_DK ** -0.5

    U = 8 if nchunk % 8 == 0 else 1
    hs, us = range(GLA_H), range(U)
    ksl = [slice(h * GLA_DK, (h + 1) * GLA_DK) for h in hs]
    vsl = [slice(h * GLA_DV, (h + 1) * GLA_DV) for h in hs]
    uh = [(u, h) for u in us for h in hs]

    def chunks(i, carry):
        rws = [pl.ds(pl.multiple_of((i * U + u) * L, L), L) for u in us]
        q = [q_ref[r, :] * scale for r in rws]
        k = [k_ref[r, :] for r in rws]
        b = [_cumsum_rows(tri, g_scr[r, :]) for r in rws]
        b_end = [x[L - 1:L, :] for x in b]
        qi = [_bf(q[u] * jnp.exp(b[u])) for u in us]
        ki = [_bf(k[u] * jnp.exp(-b[u])) for u in us]
        kend = [_bf(k[u] * jnp.exp(b_end[u] - b[u])) for u in us]
        eb_end = [jnp.exp(x) for x in b_end]
        vb = {(u, h): _bf(v_ref[rws[u], vsl[h]]) for u, h in uh}
        A = {(u, h): _bf(jnp.where(causal, _dot_nt(qi[u][:, ksl[h]], ki[u][:, ksl[h]]), 0.0)) for u, h in uh}
        kv = {(u, h): _dot_tn(vb[u, h], kend[u][:, ksl[h]]) for u, h in uh}
        av = {(u, h): _dot(A[u, h], vb[u, h]) for u, h in uh}
        St = [s_scr[h] for h in hs]
        o = {}
        for u in us:
            for h in hs:
                o[u, h] = av[u, h] + _dot_nt(qi[u][:, ksl[h]], _bf(St[h]))
            St = [St[h] * eb_end[u][:, ksl[h]] + kv[u, h] for h in hs]
        ms = {x: jnp.mean(o[x] * o[x], axis=-1, keepdims=True) for x in uh}
        for u, h in uh:
            o_ref[rws[u], vsl[h]] = _bf((o[u, h] * lax.rsqrt(ms[u, h] + EPS)) * ng_ref[:, vsl[h]]
                                        * _silu(z_ref[rws[u], vsl[h]]))
        for h in hs:
            s_scr[h] = St[h]
        return carry

    lax.fori_loop(0, nchunk // U, chunks, 0)

    @pl.when(t == pl.num_programs(1) - 1)
    def _():
        s_ref[0] = s_scr[...]


def _gla(P, B, T, g2p, gb, ng, s0t):
    L = min(T, CHUNK)
    tt = min(T, SCAN_TT)
    nT = T // tt
    rowblk = lambda w, off: pl.BlockSpec((tt, w), lambda b_, t: (b_ * nT + t, off // w))
    const = lambda shape: pl.BlockSpec(shape, lambda b_, t: (0,) * len(shape))
    st = pl.BlockSpec((1, GLA_H, GLA_DV, GLA_DK), lambda b_, t: (b_, 0, 0, 0))
    return pl.pallas_call(
        functools.partial(_gla_kernel, L=L),
        grid=(B, nT),
        in_specs=[rowblk(512, _OFF['gla_q']), rowblk(512, _OFF['gla_k']), rowblk(BW, _OFF['gla_v']),
                  rowblk(LANE, _OFF['gla_gd']), rowblk(BW, _OFF['gla_z']), const((LANE, 512)), const((1, 512)),
                  const((1, BW)), st],
        out_specs=[pl.BlockSpec((tt, BW), lambda b_, t: (b_ * nT + t, 0)), st],
        out_shape=[jax.ShapeDtypeStruct((B * T, BW), BF16), jax.ShapeDtypeStruct(s0t.shape, F32)],
        scratch_shapes=[pltpu.VMEM((GLA_H, GLA_DV, GLA_DK), F32), pltpu.VMEM((tt, 512), F32)],
        compiler_params=_cparams("parallel", "arbitrary"),
        name="gla_scan",
    )(P, P, P, P, P, g2p, gb, ng, s0t)


def _ret_kernel(q_ref, k_ref, v_ref, z_ref, cos_ref, sin_ref, s0_ref, o_ref, s_ref, s_scr, *, L):
    t = pl.program_id(1)

    @pl.when(t == 0)
    def _():
        s_scr[...] = s0_ref[0]

    nchunk = q_ref.shape[0] // L
    rel = (_iota2((L, L), 0) - _iota2((L, L), 1)).astype(F32)
    tcol = _iota2((L, 1), 0).astype(F32)
    scale = RET_DK ** -0.5
    lgs = [math.log(1.0 - 2.0 ** (-5.0 - h)) for h in range(RET_H)]
    decay = [jnp.where(rel >= 0, jnp.exp(lg * jnp.maximum(rel, 0.0)), 0.0) for lg in lgs]
    rowd = [jnp.exp(lg * (tcol + 1.0)) for lg in lgs]
    cold = [jnp.exp(lg * (L - 1.0 - tcol)) for lg in lgs]

    def rope(x, cos2, sin2):
        return x * cos2 + pltpu.roll(x, RET_DK // 2, 1) * sin2

    U = 8 if nchunk % 8 == 0 else 1
    hs, us = range(RET_H), range(U)
    ksl = [slice(h * RET_DK, (h + 1) * RET_DK) for h in hs]
    vsl = [slice(h * RET_DV, (h + 1) * RET_DV) for h in hs]
    uh = [(u, h) for u in us for h in hs]

    def chunks(i, carry):
        rws = [pl.ds(pl.multiple_of((i * U + u) * L, L), L) for u in us]
        cs = [(cos_ref[r, :], sin_ref[r, :]) for r in rws]
        k = {(u, h): rope(k_ref[rws[u], ksl[h]], *cs[u]) for u, h in uh}
        qb = {(u, h): _bf(rope(q_ref[rws[u], ksl[h]], *cs[u]) * scale) for u, h in uh}
        kb = {x: _bf(k[x]) for x in uh}
        kcb = {(u, h): _bf(k[u, h] * cold[h]) for u, h in uh}
        vb = {(u, h): _bf(v_ref[rws[u], vsl[h]]) for u, h in uh}
        A = {(u, h): _bf(_dot_nt(qb[u, h], kb[u, h]) * decay[h]) for u, h in uh}
        kv = {x: _dot_tn(kcb[x], vb[x]) for x in uh}
        av = {x: _dot(A[x], vb[x]) for x in uh}
        S = [s_scr[h] for h in hs]
        o = {}
        for u in us:
            for h in hs:
                o[u, h] = av[u, h] + rowd[h] * _dot(qb[u, h], _bf(S[h]))
            S = [math.exp(lgs[h] * L) * S[h] + kv[u, h] for h in hs]
        ms = {x: jnp.mean(o[x] * o[x], axis=-1, keepdims=True) for x in uh}
        for u, h in uh:
            o_ref[rws[u], vsl[h]] = _bf(o[u, h] * lax.rsqrt(ms[u, h] + EPS) * _silu(z_ref[rws[u], vsl[h]]))
        for h in hs:
            s_scr[h] = S[h]
        return carry

    lax.fori_loop(0, nchunk // U, chunks, 0)

    @pl.when(t == pl.num_programs(1) - 1)
    def _():
        s_ref[0] = s_scr[...]


def _ret(P, B, T, cos2, sin2, s0):
    L = min(T, CHUNK)
    tt = min(T, SCAN_TT)
    nT = T // tt
    rowblk = lambda w, off: pl.BlockSpec((tt, w), lambda b_, t: (b_ * nT + t, off // w))
    tab = pl.BlockSpec((tt, RET_DK), lambda b_, t: (t, 0))
    st = pl.BlockSpec((1, RET_H, RET_DK, RET_DV), lambda b_, t: (b_, 0, 0, 0))
    return pl.pallas_call(
        functools.partial(_ret_kernel, L=L),
        grid=(B, nT),
        in_specs=[rowblk(512, _OFF['ret_q']), rowblk(512, _OFF['ret_k']), rowblk(BW, _OFF['ret_v']),
                  rowblk(BW, _OFF['ret_z']), tab, tab, st],
        out_specs=[pl.BlockSpec((tt, BW), lambda b_, t: (b_ * nT + t, 0)), st],
        out_shape=[jax.ShapeDtypeStruct((B * T, BW), BF16), jax.ShapeDtypeStruct(s0.shape, F32)],
        scratch_shapes=[pltpu.VMEM((RET_H, RET_DK, RET_DV), F32)],
        compiler_params=_cparams("parallel", "arbitrary"),
        name="ret_scan",
    )(P, P, P, P, cos2, sin2, s0)


def _mlstm_pre_kernel(xm_ref, cp_ref, cw_ref, cb_ref, wq_ref, wk_ref, wv_ref, wif_ref, bif_ref,
                      xc_ref, q_ref, k_ref, v_ref, g_ref, carry_scr):
    t = pl.program_id(1)
    tm = xm_ref.shape[0]
    nprev = ML_CONV - 1

    @pl.when(t == 0)
    def _():
        carry_scr[...] = jnp.zeros_like(carry_scr)
        carry_scr[SUBLANE - nprev:SUBLANE, :] = cp_ref[0]

    xm = xm_ref[...]
    c8 = carry_scr[...]
    rid = _iota2((SUBLANE, 1), 0)
    conv = xm * cw_ref[nprev:nprev + 1, :] + cb_ref[...]
    for j in range(1, ML_CONV):
        rolled = pltpu.roll(xm, j, 0)
        head = jnp.where(rid < j, pltpu.roll(c8, j, 0), rolled[0:SUBLANE, :])
        prev = jnp.concatenate([head, rolled[SUBLANE:, :]], axis=0) if tm > SUBLANE else head
        conv = conv + prev * cw_ref[nprev - j:nprev - j + 1, :]
    carry_scr[...] = xm[tm - SUBLANE:tm, :]
    xc = _silu(conv)
    xc_ref[...] = xc
    xcb, xmb = _bf(xc), _bf(xm)
    q = _dot(xcb, wq_ref[...])
    k = _dot(xcb, wk_ref[...])
    v = _dot(xmb, wv_ref[...])
    g_ref[...] = (_dot(_bf(q), wif_ref[0]) + _dot(_bf(k), wif_ref[1]) + _dot(_bf(v), wif_ref[2])) + bif_ref[...]
    q_ref[...] = q
    k_ref[...] = k * (ML_D ** -0.5)
    v_ref[...] = v


def _mlstm_pre(P, B, T, conv_prev, p):
    tm = min(T, SCAN_TT)
    nT = T // tm
    row = pl.BlockSpec((tm, BW), lambda b_, t: (b_ * nT + t, 0))
    const = lambda shape: pl.BlockSpec(shape, lambda b_, t: (0,) * len(shape))
    M = B * T
    return pl.pallas_call(
        _mlstm_pre_kernel,
        grid=(B, nT),
        in_specs=[pl.BlockSpec((tm, BW), lambda b_, t: (b_ * nT + t, _OFF['ml_x'] // BW)),
                  pl.BlockSpec((1, ML_CONV - 1, BW), lambda b_, t: (b_, 0, 0)),
                  const((ML_CONV, BW)), const((1, BW)), const((BW, BW)), const((BW, BW)), const((BW, BW)),
                  const((3, BW, LANE)), const((1, LANE))],
        out_specs=[row, row, row, row, pl.BlockSpec((tm, LANE), lambda b_, t: (b_ * nT + t, 0))],
        out_shape=[jax.ShapeDtypeStruct((M, BW), F32)] * 4 + [jax.ShapeDtypeStruct((M, LANE), F32)],
        scratch_shapes=[pltpu.VMEM((SUBLANE, BW), F32)],
        compiler_params=_cparams("parallel", "arbitrary"),
        name="mlstm_pre",
    )(P, conv_prev, p['ml_conv_w'], p['ml_conv_b'], p['ml_wq'], p['ml_wk'], p['ml_wv'], p['ml_wif'], p['ml_bif'])


def _mlstm_kernel(q_ref, k_ref, v_ref, gc_ref, gr_ref, xc_ref, z_ref, ng_ref, sk_ref, c0_ref, n0_ref, m0_ref,
                  o_ref, c_ref, n_ref, m_ref, c_scr, n_scr, m_scr, *, L):
    t = pl.program_id(1)

    @pl.when(t == 0)
    def _():
        c_scr[...] = c0_ref[0]
        n_scr[...] = n0_ref[0]
        m_scr[...] = m0_ref[0]

    nchunk = q_ref.shape[0] // L
    row, col = _iota2((L, L), 0), _iota2((L, L), 1)
    tri = (row >= col).astype(BF16)
    triu = (row <= col).astype(BF16)
    causal = row >= col

    U = 2 if nchunk % 2 == 0 else 1
    hs, us = range(ML_H), range(U)
    sl = [slice(h * ML_D, (h + 1) * ML_D) for h in hs]
    uh = [(u, h) for u in us for h in hs]

    def chunks(i, carry):
        cidx = [i * U + u for u in us]
        rws = [pl.ds(pl.multiple_of(c * L, L), L) for c in cidx]
        gcb = [gc_ref[0, c] for c in cidx]
        grb = [gr_ref[0, c] for c in cidx]
        bc = [_cumsum_rows(tri, x) for x in gcb]
        br = [_cumsum_cols(x, triu) for x in grb]
        q = {(u, h): q_ref[rws[u], sl[h]] for u, h in uh}
        k = {(u, h): k_ref[rws[u], sl[h]] for u, h in uh}
        qb = {x: _bf(q[x]) for x in uh}
        vb = {(u, h): _bf(v_ref[rws[u], sl[h]]) for u, h in uh}
        qk = {x: _dot_nt(qb[x], _bf(k[x])) for x in uh}
        b_c = {(u, h): bc[u][:, ML_H + h:ML_H + h + 1] for u, h in uh}
        dlog = {(u, h): jnp.where(causal, b_c[u, h] - br[u][ML_H + h:ML_H + h + 1, :] + grb[u][h:h + 1, :], -jnp.inf)
                for u, h in uh}
        m_loc = {x: jnp.max(dlog[x], axis=-1, keepdims=True) for x in uh}
        s_loc = {x: qk[x] * jnp.exp(dlog[x] - m_loc[x]) for x in uh}
        rs = {x: jnp.sum(s_loc[x], axis=-1, keepdims=True) for x in uh}
        b_end = {x: b_c[x][L - 1:L, :] for x in uh}
        m_le = {x: m_loc[x][L - 1:L, :] for x in uh}
        kw = {(u, h): k[u, h] * jnp.exp(b_end[u, h] - b_c[u, h] + gcb[u][:, h:h + 1] - m_le[u, h]) for u, h in uh}
        sv = {x: _dot(_bf(s_loc[x]), vb[x]) for x in uh}
        kv = {x: _dot_tn(_bf(kw[x]), vb[x]) for x in uh}
        ksum = {x: jnp.sum(kw[x], axis=0, keepdims=True) for x in uh}
        C = [c_scr[h] for h in hs]
        nvec = [n_scr[h] for h in hs]
        m_prev = [m_scr[h][:, 0:1] for h in hs]
        hh = {}
        for u in us:
            qc = [_dot(qb[u, h], _bf(C[h])) for h in hs]
            qn = [jnp.sum(q[u, h] * nvec[h], axis=-1, keepdims=True) for h in hs]
            from_state = [b_c[u, h] + m_prev[h] for h in hs]
            m_t = [jnp.maximum(from_state[h], m_loc[u, h]) for h in hs]
            a_in = [jnp.exp(m_loc[u, h] - m_t[h]) for h in hs]
            w_state = [jnp.exp(from_state[h] - m_t[h]) for h in hs]
            den = [a_in[h] * rs[u, h] + w_state[h] * qn[h] for h in hs]
            for h in hs:
                hh[u, h] = ((a_in[h] * sv[u, h] + w_state[h] * qc[h])
                            / jnp.maximum(jnp.abs(den[h]), jnp.exp(-m_t[h])))
            m_new = [m_t[h][L - 1:L, :] for h in hs]
            cd = [jnp.exp(b_end[u, h] + m_prev[h] - m_new[h]) for h in hs]
            sc = [jnp.exp(m_le[u, h] - m_new[h]) for h in hs]
            C = [cd[h] * C[h] + sc[h] * kv[u, h] for h in hs]
            nvec = [cd[h] * nvec[h] + sc[h] * ksum[u, h] for h in hs]
            m_prev = m_new
        mean = {x: jnp.mean(hh[x], axis=-1, keepdims=True) for x in uh}
        cen = {x: hh[x] - mean[x] for x in uh}
        var = {x: jnp.mean(cen[x] * cen[x], axis=-1, keepdims=True) for x in uh}
        for u, h in uh:
            y = cen[u, h] * lax.rsqrt(var[u, h] + EPS)
            o_ref[rws[u], sl[h]] = _bf((y * ng_ref[:, sl[h]] + sk_ref[:, sl[h]] * xc_ref[rws[u], sl[h]])
                                       * _silu(z_ref[rws[u], sl[h]]))
        for h in hs:
            c_scr[h] = C[h]
            n_scr[h] = nvec[h]
            m_scr[h] = jnp.broadcast_to(m_prev[h], (1, LANE))
        return carry

    for i in range(nchunk // U):
        chunks(i, 0)

    @pl.when(t == pl.num_programs(1) - 1)
    def _():
        c_ref[0] = c_scr[...]
        n_ref[0] = n_scr[...]
        m_ref[0] = m_scr[...]


def _mlstm(q, k, v, gc, gr, xc, P, ng, sk, c0, n0, m0, B, T):
    L = min(T, CHUNK)
    tt = min(T, SCAN_TT)
    nT = T // tt
    nc = tt // L
    row = pl.BlockSpec((tt, BW), lambda b_, t: (b_ * nT + t, 0))
    const = pl.BlockSpec((1, BW), lambda b_, t: (0, 0))
    stc = pl.BlockSpec((1, ML_H, ML_D, ML_D), lambda b_, t: (b_, 0, 0, 0))
    stn = pl.BlockSpec((1, ML_H, 1, ML_D), lambda b_, t: (b_, 0, 0, 0))
    stm = pl.BlockSpec((1, ML_H, 1, LANE), lambda b_, t: (b_, 0, 0, 0))
    return pl.pallas_call(
        functools.partial(_mlstm_kernel, L=L),
        grid=(B, nT),
        in_specs=[row, row, row,
                  pl.BlockSpec((1, nc, L, 8), lambda b_, t: (b_, t, 0, 0)),
                  pl.BlockSpec((1, nc, 8, L), lambda b_, t: (b_, t, 0, 0)),
                  row, pl.BlockSpec((tt, BW), lambda b_, t: (b_ * nT + t, _OFF['ml_z'] // BW)), const, const, stc, stn, stm],
        out_specs=[row, stc, stn, stm],
        out_shape=[jax.ShapeDtypeStruct((B * T, BW), BF16), jax.ShapeDtypeStruct(c0.shape, F32),
                   jax.ShapeDtypeStruct(n0.shape, F32), jax.ShapeDtypeStruct(m0.shape, F32)],
        scratch_shapes=[pltpu.VMEM((ML_H, ML_D, ML_D), F32), pltpu.VMEM((ML_H, 1, ML_D), F32),
                        pltpu.VMEM((ML_H, 1, LANE), F32)],
        compiler_params=_cparams("parallel", "arbitrary"),
        name="mlstm_scan",
    )(q, k, v, gc, gr, xc, P, ng, sk, c0, n0, m0)


def _xattn_kernel(q_ref, z_ref, mk_ref, mv_ref, o_ref):
    hs = range(XA_H)
    sl = [slice(h * XA_D, (h + 1) * XA_D) for h in hs]
    s = [_dot_nt(_bf(q_ref[:, x]), _bf(mk_ref[0, :, x])) * (XA_D ** -0.5) for x in sl]
    e = [jnp.exp(x - jnp.max(x, axis=-1, keepdims=True)) for x in s]
    prob = [_bf(x / jnp.sum(x, axis=-1, keepdims=True)) for x in e]
    o = [_dot(prob[h], _bf(mv_ref[0, :, sl[h]])) for h in hs]
    for h in hs:
        o_ref[:, sl[h]] = _bf(o[h] * _silu(z_ref[:, sl[h]]))


def _xattn(P, B, T, mk, mv):
    tt = min(T, SCAN_TT)
    nT = T // tt
    n_mem = mk.shape[1]
    mem = pl.BlockSpec((1, n_mem, BW), lambda b_, t: (b_, 0, 0))
    return pl.pallas_call(
        _xattn_kernel,
        grid=(B, nT),
        in_specs=[pl.BlockSpec((tt, BW), lambda b_, t: (b_ * nT + t, _OFF['xa_q'] // BW)),
                  pl.BlockSpec((tt, BW), lambda b_, t: (b_ * nT + t, _OFF['xa_z'] // BW)), mem, mem],
        out_specs=pl.BlockSpec((tt, BW), lambda b_, t: (b_ * nT + t, 0)),
        out_shape=jax.ShapeDtypeStruct((B * T, BW), BF16),
        compiler_params=_cparams("parallel", "parallel"),
        name="xattn",
    )(P, P, mk, mv)


def _merge_kernel(o0, o1, o2, o3, o4, g_ref, w_ref, out_ref):
    ys = [_dot(o_ref[...], w_ref[i]) for i, o_ref in enumerate((o0, o1, o2, o3, o4))]
    acc = None
    for i, y in enumerate(ys):
        term = _sigmoid(g_ref[:, i * D_MODEL:(i + 1) * D_MODEL]) * y
        acc = term if acc is None else acc + term
    out_ref[...] = _bf(acc)


def _merge(ogs, P, wb, tm):
    M = ogs[0].shape[0]
    gw = N_BRANCH * D_MODEL
    og = pl.BlockSpec((tm, BW), lambda i: (i, 0))
    return pl.pallas_call(
        _merge_kernel,
        grid=(M // tm,),
        in_specs=[og] * N_BRANCH + [pl.BlockSpec((tm, gw), lambda i: (i, _OFF['gates'] // gw)),
                                    pl.BlockSpec((N_BRANCH, BW, D_MODEL), lambda i: (0, 0, 0),
                                                 pipeline_mode=pl.Buffered(1))],
        out_specs=pl.BlockSpec((tm, D_MODEL), lambda i: (i, 0)),
        out_shape=jax.ShapeDtypeStruct((M, D_MODEL), BF16),
        compiler_params=_cparams("parallel"),
        name="merge",
    )(*ogs, P, wb)


def _resid_kernel(x_ref, m_ref, w_ref, o_ref):
    o_ref[...] = x_ref[...] + _dot(m_ref[...], w_ref[...])


def _resid_norm_kernel(x_ref, m_ref, w_ref, g_ref, o_ref):
    y = x_ref[...] + _dot(m_ref[...], w_ref[...])
    ms = jnp.mean(y * y, axis=-1, keepdims=True)
    o_ref[...] = (y * lax.rsqrt(ms + EPS)) * g_ref[...]


def _resid_out(x, merged, w, tm, final_g=None):
    M = x.shape[0]
    row = pl.BlockSpec((tm, D_MODEL), lambda i: (i, 0))
    in_specs = [row, row, pl.BlockSpec((D_MODEL, D_MODEL), lambda i: (0, 0), pipeline_mode=pl.Buffered(1))]
    args = [x, merged, w]
    if final_g is not None:
        in_specs.append(pl.BlockSpec((1, D_MODEL), lambda i: (0, 0)))
        args.append(final_g)
    return pl.pallas_call(
        _resid_kernel if final_g is None else _resid_norm_kernel,
        grid=(M // tm,),
        in_specs=in_specs,
        out_specs=row,
        out_shape=jax.ShapeDtypeStruct((M, D_MODEL), F32),
        compiler_params=_cparams("parallel"),
        name="resid_out",
    )(*args)


def _blockdiag(w):
    depth, n, c, d = w.shape
    rows = w.reshape(depth, n * c, d)
    tile = (jnp.arange(d)[:, None] == jnp.arange(n * d)[None, :] % d).astype(w.dtype)
    dense = jnp.einsum('lrd,dj->lrj', rows, tile)
    mask = (jnp.arange(n * c)[:, None] // c) == (jnp.arange(n * d)[None, :] // d)
    return jnp.where(mask, dense, 0.0).astype(BF16)


def _rope_tables(start, T):
    half = RET_DK // 2
    inv = ROPE_BASE ** (-jnp.linspace(0.0, 1.0, half, dtype=F32))
    blk = ROPE_BLK
    if T % blk or T <= blk:
        ang = (start + jnp.arange(T, dtype=F32))[:, None] * inv[None, :]
        cos, sin = jnp.cos(ang), jnp.sin(ang)
    else:
        hi = (start + blk * jnp.arange(T // blk, dtype=F32))[:, None] * inv[None, :]
        lo = jnp.arange(blk, dtype=F32)[:, None] * inv[None, :]
        ch, sh, cl, sl_ = jnp.cos(hi)[:, None, :], jnp.sin(hi)[:, None, :], jnp.cos(lo)[None], jnp.sin(lo)[None]
        cos = (ch * cl - sh * sl_).reshape(T, half)
        sin = (sh * cl + ch * sl_).reshape(T, half)
    return jnp.concatenate([cos, cos], axis=-1), jnp.concatenate([-sin, sin], axis=-1)


def _layer(x, B, T, tabs, st, mem_k, mem_v, p, final_g=None):
    M = B * T
    tm = min(M, PROJ_TM)
    P = _normproj(x, p['norm_g'], p['w_pack'], tm, PROJ_TN)
    P3 = P.reshape(B, T, NP)

    og_a, s_rwkv = _rwkv(P, B, T, p, st['rwkv'], st['rwkv_shift'])
    shift_new = P3[:, T - 1:, _OFF['rwkv_shift']:_OFF['rwkv_shift'] + RWKV_SHIFT_W]

    og_b, s_gla_t = _gla(P, B, T, p['gla_g2p'], p['gla_gb'], p['gla_norm_g'], jnp.swapaxes(st['gla'], -1, -2))
    s_gla = jnp.swapaxes(s_gla_t, -1, -2)

    og_c, s_ret = _ret(P, B, T, tabs[0], tabs[1], st['ret'])

    xc, q_m, k_m, v_m, gates = _mlstm_pre(P, B, T, st['ml_conv'], p)
    L = min(T, CHUNK)
    gcol = jnp.concatenate([gates[:, :ML_H], _log_sigmoid(gates[:, ML_H:2 * ML_H])], axis=-1).reshape(B, T // L, L, 2 * ML_H)
    grow = jnp.swapaxes(gcol, -1, -2)
    og_d, c_new, n_new, m_new = _mlstm(
        q_m, k_m, v_m, gcol, grow, xc, P, p['ml_norm_g'], p['ml_skip'], st['ml_c'], st['ml_n'][:, :, None, :],
        jnp.broadcast_to(st['ml_m'][:, :, None, None], st['ml_m'].shape + (1, LANE)), B, T)
    xm_tail = P3[:, max(T - (ML_CONV - 1), 0):, _OFF['ml_x']:_OFF['ml_x'] + BW]
    conv_new = jnp.concatenate([st['ml_conv'], xm_tail], axis=1)[:, -(ML_CONV - 1):]

    og_x = _xattn(P, B, T, mem_k.reshape(B, -1, BW), mem_v.reshape(B, -1, BW))

    merged = _merge([og_a, og_b, og_c, og_d, og_x], P, p['w_branch'], min(M, MERGE_TM))
    x_new = _resid_out(x, merged, p['w_out'], min(M, RESID_TM), final_g)
    new = {'rwkv': s_rwkv, 'rwkv_shift': shift_new, 'gla': s_gla, 'ret': s_ret, 'ml_c': c_new,
           'ml_n': n_new[:, :, 0, :], 'ml_m': m_new[:, :, 0, 0], 'ml_conv': conv_new}
    return x_new, new


_STATE_KEYS = ('rwkv', 'rwkv_shift', 'gla', 'ret', 'ml_c', 'ml_n', 'ml_m', 'ml_conv')


def kernel(x_prompt, x_sample, mem_prompt, cache_mem_k, cache_mem_v, state_rwkv, state_rwkv_shift, state_gla, state_ret, state_mlstm_c, state_mlstm_n, state_mlstm_m, state_mlstm_conv, norm_g, mem_norm_g, w_in, w_mem_kv, rwkv_mu, rwkv_w0, rwkv_w2, rwkv_a0, rwkv_a2, rwkv_k_k, rwkv_k_a, rwkv_r_k, rwkv_ln_g, gla_g2, gla_gb, gla_norm_g, ml_conv_w, ml_conv_b, ml_wq, ml_wk, ml_wv, ml_w_if, ml_b_if, ml_skip, ml_norm_g, w_branch, w_out, final_norm_g):
    Bp, Tp, D = x_prompt.shape
    Bs, Ts, _ = x_sample.shape
    depth = w_in.shape[0]
    n_mem = mem_prompt.shape[1]

    params = {
        'norm_g': norm_g[:, None, :], 'mem_norm_g': mem_norm_g[:, None, :],
        'w_mem_kv': _bf(w_mem_kv),
        'rwkv_mu': rwkv_mu[:, None, :], 'rwkv_w0': rwkv_w0[:, None, :], 'rwkv_a0': rwkv_a0[:, None, :],
        'rwkv_w2p': _bf(jnp.pad(rwkv_w2, ((0, 0), (0, LANE - RWKV_LORA), (0, 0)))),
        'rwkv_a2p': _bf(jnp.pad(rwkv_a2, ((0, 0), (LANE - RWKV_LORA, 0), (0, 0)))),
        'rwkv_k_k': rwkv_k_k[:, None, :], 'rwkv_k_a': rwkv_k_a[:, None, :], 'rwkv_r_k': rwkv_r_k[:, None, :],
        'rwkv_ln_g': rwkv_ln_g[:, None, :],
        'gla_g2p': _bf(jnp.pad(gla_g2, ((0, 0), (0, LANE - GLA_LORA), (0, 0)))), 'gla_gb': gla_gb[:, None, :],
        'gla_norm_g': gla_norm_g[:, None, :],
        'ml_conv_w': ml_conv_w, 'ml_conv_b': ml_conv_b[:, None, :],
        'ml_wq': _blockdiag(ml_wq), 'ml_wk': _blockdiag(ml_wk), 'ml_wv': _blockdiag(ml_wv),
        'ml_wif': _bf(jnp.pad(ml_w_if.reshape(depth, 3, BW, 2 * ML_H), ((0, 0), (0, 0), (0, 0), (0, LANE - 2 * ML_H)))),
        'ml_bif': jnp.pad(ml_b_if, ((0, 0), (0, LANE - 2 * ML_H)))[:, None, :],
        'ml_skip': ml_skip[:, None, :], 'ml_norm_g': ml_norm_g[:, None, :],
        'w_branch': _bf(w_branch), 'w_out': _bf(w_out),
    }
    cache = {'rwkv': state_rwkv, 'rwkv_shift': state_rwkv_shift, 'gla': state_gla, 'ret': state_ret,
             'ml_c': state_mlstm_c, 'ml_n': state_mlstm_n, 'ml_m': state_mlstm_m, 'ml_conv': state_mlstm_conv,
             'mem_k': cache_mem_k, 'mem_v': cache_mem_v}
    tabs_p = _rope_tables(0.0, Tp)
    tabs_s = _rope_tables(float(PAST_LEN), Ts)
    zero_p = {
        'rwkv': jnp.zeros((Bp, RWKV_H, RWKV_D, RWKV_D), F32), 'rwkv_shift': jnp.zeros((Bp, 1, RWKV_SHIFT_W), F32),
        'gla': jnp.zeros((Bp, GLA_H, GLA_DK, GLA_DV), F32), 'ret': jnp.zeros((Bp, RET_H, RET_DK, RET_DV), F32),
        'ml_c': jnp.zeros((Bp, ML_H, ML_D, ML_D), F32), 'ml_n': jnp.zeros((Bp, ML_H, ML_D), F32),
        'ml_m': jnp.zeros((Bp, ML_H), F32), 'ml_conv': jnp.zeros((Bp, ML_CONV - 1, BW), F32),
    }
    mem2d = mem_prompt.reshape(Bp * n_mem, D)

    yp, ys = x_prompt.reshape(Bp * Tp, D), x_sample.reshape(Bs * Ts, D)
    outs = []
    for l in range(depth):
        p = {nm: arr[l] for nm, arr in params.items()}
        p['w_pack'] = _pack_w_in(w_in, l)
        kv = _normproj(mem2d, p['mem_norm_g'], p['w_mem_kv'], min(Bp * n_mem, MEM_TM), MEM_TN)
        mk_l = kv[:, :BW].reshape(Bp, n_mem, XA_H, XA_D)
        mv_l = kv[:, BW:].reshape(Bp, n_mem, XA_H, XA_D)
        fg = final_norm_g[None, :] if l == depth - 1 else None
        yp, stp_l = _layer(yp, Bp, Tp, tabs_p, zero_p, mk_l, mv_l, p, fg)
        ys, sts_l = _layer(ys, Bs, Ts, tabs_s, {nm: cache[nm][l] for nm in _STATE_KEYS},
                           cache['mem_k'][l], cache['mem_v'][l], p, fg)
        outs.append((stp_l, mk_l, mv_l, sts_l))
    stp = {nm: jnp.stack([o[0][nm] for o in outs]) for nm in _STATE_KEYS}
    sts = {nm: jnp.stack([o[3][nm] for o in outs]) for nm in _STATE_KEYS}
    mk = jnp.stack([o[1] for o in outs])
    mv = jnp.stack([o[2] for o in outs])
    y_prompt = yp.reshape(Bp, Tp, D)
    y_sample = ys.reshape(Bs, Ts, D)
    return (y_prompt, y_sample,
            stp['rwkv'], stp['rwkv_shift'], stp['gla'], stp['ret'], stp['ml_c'], stp['ml_n'], stp['ml_m'],
            stp['ml_conv'], mk, mv,
            sts['rwkv'], sts['rwkv_shift'], sts['gla'], sts['ret'], sts['ml_c'], sts['ml_n'], sts['ml_m'],
            sts['ml_conv'])
```

```python
import functools
import math

import jax
import jax.numpy as jnp
from jax import lax
from jax.experimental import pallas as pl
from jax.experimental.pallas import tpu as pltpu

F32 = jnp.float32
BF16 = jnp.bfloat16

D_MODEL = 2048
BW = 1024
EPS = 1e-6
CHUNK = 64
N_BRANCH = 5
PAST_LEN = 2048
RWKV_H, RWKV_D, RWKV_LORA = 16, 64, 64
RWKV_SHIFT_W = 3 * BW + 2 * RWKV_LORA
RWKV_DECAY_SCALE = 0.606531
GLA_H, GLA_DK, GLA_DV, GLA_LORA = 4, 128, 256, 16
GLA_GATE_NORM = 16.0
RET_H, RET_DK, RET_DV = 4, 128, 256
ROPE_BASE = 10000.0
ML_H, ML_D, ML_CONV = 4, 256, 4
XA_H, XA_D = 4, 256
LANE = 128
SUBLANE = 8
VMEM_LIMIT = 56 * 1024 * 1024

PROJ_TM = 1024
MEM_TM, MEM_TN = 512, 512
RWKV_TT = 256
SCAN_TT = 512
MERGE_TM = 256
RESID_TM = 512
ROPE_BLK = 128
PACK_TR = 128

_IN_LAYOUT = (
    ('rwkv_shift', RWKV_SHIFT_W), ('rwkv_z', BW), ('gla_q', 512), ('gla_k', 512), ('gla_v', BW),
    ('gla_gd', GLA_LORA), ('gla_z', BW), ('ret_q', 512), ('ret_k', 512), ('ret_v', BW), ('ret_z', BW),
    ('ml_x', BW), ('ml_z', BW), ('xa_q', BW), ('xa_z', BW), ('gates', N_BRANCH * D_MODEL),
)
_PACK_ORDER = ('gates', 'rwkv_z', 'gla_z', 'ret_z', 'ml_z', 'xa_z', 'ret_q', 'ret_k', 'ret_v',
               'gla_q', 'gla_k', 'gla_v', 'ml_x', 'xa_q', 'rwkv_shift', 'gla_gd')
PROJ_TN = 1792


def _src_cols(name):
    start = 0
    for nm, size in _IN_LAYOUT:
        if nm == name:
            return start, start + size
        start += size
    raise KeyError(name)


def _pack_offsets():
    off, cur = {}, 0
    for nm in _PACK_ORDER:
        a, b = _src_cols(nm)
        off[nm] = cur
        cur += -(-(b - a) // LANE) * LANE
    total = -(-cur // PROJ_TN) * PROJ_TN
    return off, total


_OFF, NP = _pack_offsets()


def _pack_kernel(w_ref, o_ref):
    cur = 0
    for nm in _PACK_ORDER:
        a, b = _src_cols(nm)
        o_ref[:, cur:cur + (b - a)] = w_ref[0, :, a:b].astype(BF16)
        width = -(-(b - a) // LANE) * LANE
        if width != b - a:
            o_ref[:, cur + (b - a):cur + width] = jnp.zeros((o_ref.shape[0], width - (b - a)), BF16)
        cur += width
    if NP != cur:
        o_ref[:, cur:NP] = jnp.zeros((o_ref.shape[0], NP - cur), BF16)


def _pack_w_in(w_in, layer):
    _, K, n_in = w_in.shape
    return pl.pallas_call(
        _pack_kernel,
        grid=(K // PACK_TR,),
        in_specs=[pl.BlockSpec((1, PACK_TR, n_in), lambda i: (layer, i, 0))],
        out_specs=pl.BlockSpec((PACK_TR, NP), lambda i: (i, 0)),
        out_shape=jax.ShapeDtypeStruct((K, NP), BF16),
        compiler_params=_cparams("parallel"),
        name="pack_w_in",
    )(w_in)


def _cparams(*sem):
    return pltpu.CompilerParams(dimension_semantics=sem, vmem_limit_bytes=VMEM_LIMIT)


def _dot(a, b):
    return jnp.dot(a, b, preferred_element_type=F32)


def _dot_nt(a, b):
    return lax.dot_general(a, b, (((1,), (1,)), ((), ())), preferred_element_type=F32)


def _dot_tn(a, b):
    return lax.dot_general(a, b, (((0,), (0,)), ((), ())), preferred_element_type=F32)


def _bf(x):
    return x.astype(BF16)


def _split3(x):
    hi = x.astype(BF16)
    r1 = x - hi.astype(F32)
    mid = r1.astype(BF16)
    lo = (r1 - mid.astype(F32)).astype(BF16)
    return hi, mid, lo


def _cumsum_rows(tri, x):
    hi, mid, lo = _split3(x)
    return _dot(tri, hi) + _dot(tri, mid) + _dot(tri, lo)


def _cumsum_cols(x, triu):
    hi, mid, lo = _split3(x)
    return _dot(hi, triu) + _dot(mid, triu) + _dot(lo, triu)


def _iota2(shape, dim):
    return lax.broadcasted_iota(jnp.int32, shape, dim)


def _log_sigmoid(x):
    return jnp.minimum(x, 0.0) - jnp.log1p(jnp.exp(-jnp.abs(x)))


def _sigmoid(x):
    return 1.0 / (1.0 + jnp.exp(-x))


def _silu(x):
    return x * _sigmoid(x)


def _normproj_kernel(x_ref, g_ref, w_ref, o_ref, h_ref):
    @pl.when(pl.program_id(1) == 0)
    def _():
        x = x_ref[...]
        ms = jnp.mean(x * x, axis=-1, keepdims=True)
        h_ref[...] = ((x * lax.rsqrt(ms + EPS)) * g_ref[...]).astype(BF16)

    o_ref[...] = _dot(h_ref[...], w_ref[...])


def _normproj(x, g, w, tm, tn):
    M, K = x.shape
    N = w.shape[1]
    return pl.pallas_call(
        _normproj_kernel,
        grid=(M // tm, N // tn),
        in_specs=[pl.BlockSpec((tm, K), lambda i, j: (i, 0)),
                  pl.BlockSpec((1, K), lambda i, j: (0, 0)),
                  pl.BlockSpec((K, tn), lambda i, j: (0, j))],
        out_specs=pl.BlockSpec((tm, tn), lambda i, j: (i, j)),
        out_shape=jax.ShapeDtypeStruct((M, N), F32),
        scratch_shapes=[pltpu.VMEM((tm, K), BF16)],
        compiler_params=_cparams("parallel", "arbitrary"),
        name="normproj",
    )(x, g, w)


def _rwkv_kernel(xr_ref, xk_ref, xv_ref, xwa_ref, z_ref, sp_ref, mu_ref, w2_ref, a2_ref, w0_ref, a0_ref,
                 kkw_ref, ka_ref, rk_ref, lng_ref, s0_ref, og_ref, s_ref,
                 s_scr, phi_scr, psi_scr, qt_scr, egl_scr, carry_scr,
                 r_scr, lw_scr, k_scr, v_scr, kk_scr, b_scr, y_scr, *, L):
    t = pl.program_id(1)
    HG, D = RWKV_H, RWKV_D
    hs = range(HG)
    sls = [slice(h * D, (h + 1) * D) for h in hs]

    @pl.when(t == 0)
    def _():
        s_scr[...] = s0_ref[0]
        carry_scr[...] = sp_ref[0]

    def cat(parts):
        return jnp.concatenate(parts, axis=1)

    tt = xr_ref.shape[0]
    first = _iota2((tt, 1), 0) == 0

    def mix(x_ref, lo, hi):
        x = x_ref[...]
        prev = jnp.where(first, carry_scr[:, lo:hi], pltpu.roll(x, 1, 0))
        carry_scr[:, lo:hi] = x[tt - 1:tt, :]
        return x + (prev - x) * mu_ref[:, lo:hi]

    r_ = mix(xr_ref, 0, BW)
    k_ = mix(xk_ref, BW, 2 * BW)
    v_ = mix(xv_ref, 2 * BW, 3 * BW)
    wa = mix(xwa_ref, 3 * BW, RWKV_SHIFT_W)
    lw_scr[...] = -RWKV_DECAY_SCALE * _sigmoid(w0_ref[...] + _dot(_bf(jnp.tanh(wa)), w2_ref[...]))
    a_ = _sigmoid(a0_ref[...] + _dot(_bf(wa), a2_ref[...]))
    ones_blk = (_iota2((2 * LANE, LANE), 0) % LANE // D == _iota2((2 * LANE, LANE), 1) // D).astype(BF16)

    def headsum(x):
        hi = _bf(x)
        lo = _bf(x - hi.astype(F32))
        return cat([_dot(jnp.concatenate([hi[:, j:j + LANE], lo[:, j:j + LANE]], axis=1), ones_blk)
                    for j in range(0, BW, LANE)])

    kkf = k_ * kkw_ref[...]
    kk = kkf * lax.rsqrt(jnp.maximum(headsum(kkf * kkf), 1e-24))
    r_scr[...] = r_
    k_scr[...] = k_ * (1.0 + (a_ - 1.0) * ka_ref[...])
    v_scr[...] = v_
    kk_scr[...] = kk
    b_scr[...] = kk * a_

    nchunk = tt // L
    row, col = _iota2((L, L), 0), _iota2((L, L), 1)
    tri = (row >= col).astype(BF16)
    strict = row > col
    incl2 = _iota2((L, 2 * L), 0) >= _iota2((L, 2 * L), 1) % L
    eye = (row == col).astype(F32)

    U1 = 4 if nchunk % 4 == 0 else 1
    it = range(U1 * HG)
    isl = [sls[i % HG] for i in it]

    def phase1(ci, carry):
        cidx = [ci * U1 + u for u in range(U1)]
        rws = [pl.ds(pl.multiple_of(c * L, L), L) for c in cidx]
        kkp, rg, kn, bn, kend, bend, v_c = [], [], [], [], [], [], []
        for u, c in enumerate(cidx):
            lw, k_c, b_c = lw_scr[rws[u], :], k_scr[rws[u], :], b_scr[rws[u], :]
            g = _cumsum_rows(tri, lw)
            gl = g[L - 1:L, :]
            kkp.append(kk_scr[rws[u], :] * jnp.exp(g - lw))
            rg.append(r_scr[rws[u], :] * jnp.exp(g))
            eng = jnp.exp(-g)
            kn.append(k_c * eng)
            bn.append(b_c * eng)
            ee = jnp.exp(gl - g)
            kend.append(k_c * ee)
            bend.append(b_c * ee)
            v_c.append(v_scr[rws[u], :])
            egl_scr[c] = jnp.exp(gl)
        kkp_h = [kkp[i // HG][:, isl[i]] for i in it]
        rg_h = [rg[i // HG][:, isl[i]] for i in it]
        bk = [_bf(jnp.concatenate([bn[i // HG][:, isl[i]], kn[i // HG][:, isl[i]]], axis=0)) for i in it]
        kr = [_bf(jnp.concatenate([kkp_h[i], rg_h[i]], axis=0)) for i in it]
        mnaq = [_dot_nt(kr[i], bk[i]) for i in it]
        mn = [x[:L, :] for x in mnaq]
        aq = [x[L:, :] for x in mnaq]
        N = [_bf(jnp.where(strict, m[:, L:], 0.0)) for m in mn]
        vb = [_bf(v_c[i // HG][:, isl[i]]) for i in it]
        nv = [_dot(N[i], vb[i]) for i in it]
        X = [jnp.where(strict, -m[:, :L], 0.0) for m in mn]
        Xb = [_bf(x) for x in X]
        P = [_dot(xb, xb) for xb in Xb]
        tinv = [eye + x for x in X]
        n = 2
        while n < L:
            Pb = [_bf(p) for p in P]
            if 2 * n >= L:
                tinv = [tinv[i] + _dot(_bf(tinv[i]), Pb[i]) for i in it]
            else:
                st = [_dot(_bf(jnp.concatenate([P[i], tinv[i]], axis=0)), Pb[i]) for i in it]
                P = [x[:L, :] for x in st]
                tinv = [tinv[i] + st[i][L:, :] for i in it]
            n *= 2
        gh = [_dot(_bf(tinv[i]), _bf(jnp.concatenate([kkp_h[i], nv[i]], axis=1))) for i in it]
        gm = [_bf(x[:, :D]) for x in gh]
        hm = [_bf(x[:, D:]) for x in gh]
        bend_b = [_bf(bend[i // HG][:, isl[i]]) for i in it]
        aqm = [_bf(jnp.where(incl2, a, 0.0)) for a in aq]
        hv = [jnp.concatenate([-hm[i], vb[i]], axis=0) for i in it]
        bke = [jnp.concatenate([bend_b[i], _bf(kend[i // HG][:, isl[i]])], axis=0) for i in it]
        phi = [_dot_tn(gm[i], bend_b[i]) for i in it]
        psi = [_dot_tn(hv[i], bke[i]) for i in it]
        qts = [rg_h[i] - _dot(aqm[i][:, :L], gm[i]) for i in it]
        y0s = [_dot(aqm[i], hv[i]) for i in it]
        for i in it:
            phi_scr[cidx[i // HG], i % HG] = _bf(phi[i])
            psi_scr[cidx[i // HG], i % HG] = psi[i]
        for u in range(U1):
            qt_scr[rws[u], :] = _bf(cat(qts[u * HG:(u + 1) * HG]))
            y_scr[rws[u], :] = cat(y0s[u * HG:(u + 1) * HG])
        return carry

    lax.fori_loop(0, nchunk // U1, phase1, 0)

    S = [s_scr[h] for h in hs]
    for c in range(nchunk):
        rows = slice(c * L, (c + 1) * L)
        Sb = [_bf(s_) for s_ in S]
        sphi = [_dot(Sb[h], phi_scr[c, h]) for h in hs]
        ys = [_dot_nt(qt_scr[rows, sls[h]], Sb[h]) for h in hs]
        egl = egl_scr[c]
        S = [S[h] * egl[:, sls[h]] - sphi[h] + psi_scr[c, h] for h in hs]
        y_scr[rows, :] += cat(ys)
    for h in hs:
        s_scr[h] = S[h]

    y = y_scr[...]
    rk = r_scr[...] * k_scr[...] * rk_ref[...]
    cen = y - headsum(y) * (1.0 / D)
    ln = cen * lax.rsqrt(headsum(cen * cen) * (1.0 / D) + EPS)
    out = ln * lng_ref[...] + headsum(rk) * v_scr[...]
    og_ref[...] = _bf(out * _silu(z_ref[...]))

    @pl.when(t == pl.num_programs(1) - 1)
    def _():
        s_ref[0] = s_scr[...]


def _rwkv(P, B, T, p, s0, shift_prev):
    L = min(T, CHUNK)
    tt = min(T, RWKV_TT)
    nT = T // tt
    nc = tt // L
    off = _OFF['rwkv_shift']
    rowblk = lambda w, o: pl.BlockSpec((tt, w), lambda b_, t: (b_ * nT + t, o // w))
    const = lambda shape: pl.BlockSpec(shape, lambda b_, t: (0,) * len(shape))
    st = pl.BlockSpec((1, RWKV_H, RWKV_D, RWKV_D), lambda b_, t: (b_, 0, 0, 0))
    vec = const((1, BW))
    return pl.pallas_call(
        functools.partial(_rwkv_kernel, L=L),
        grid=(B, nT),
        in_specs=[rowblk(BW, off), rowblk(BW, off + BW), rowblk(BW, off + 2 * BW), rowblk(LANE, off + 3 * BW),
                  rowblk(BW, _OFF['rwkv_z']),
                  pl.BlockSpec((1, 1, RWKV_SHIFT_W), lambda b_, t: (b_, 0, 0)), const((1, RWKV_SHIFT_W)),
                  const((LANE, BW)), const((LANE, BW)), vec, vec, vec, vec, vec, vec, st],
        out_specs=[pl.BlockSpec((tt, BW), lambda b_, t: (b_ * nT + t, 0)), st],
        out_shape=[jax.ShapeDtypeStruct((B * T, BW), BF16), jax.ShapeDtypeStruct(s0.shape, F32)],
        scratch_shapes=[pltpu.VMEM((RWKV_H, RWKV_D, RWKV_D), F32),
                        pltpu.VMEM((nc, RWKV_H, RWKV_D, RWKV_D), BF16),
                        pltpu.VMEM((nc, RWKV_H, RWKV_D, RWKV_D), F32),
                        pltpu.VMEM((tt, BW), BF16),
                        pltpu.VMEM((nc, 1, BW), F32),
                        pltpu.VMEM((1, RWKV_SHIFT_W), F32)] + [pltpu.VMEM((tt, BW), F32)] * 7,
        compiler_params=_cparams("parallel", "arbitrary"),
        name="rwkv_scan",
    )(P, P, P, P, P, shift_prev, p['rwkv_mu'], p['rwkv_w2p'], p['rwkv_a2p'], p['rwkv_w0'], p['rwkv_a0'],
      p['rwkv_k_k'], p['rwkv_k_a'], p['rwkv_r_k'], p['rwkv_ln_g'], s0)


def _gla_kernel(q_ref, k_ref, v_ref, gd_ref, z_ref, g2_ref, gb_ref, ng_ref, s0_ref, o_ref, s_ref, s_scr, g_scr, *, L):
    t = pl.program_id(1)

    @pl.when(t == 0)
    def _():
        s_scr[...] = s0_ref[0]

    gk = _dot(_bf(gd_ref[...]), g2_ref[...]) + gb_ref[...]
    g_scr[...] = _log_sigmoid(gk) * (1.0 / GLA_GATE_NORM)

    nchunk = q_ref.shape[0] // L
    row, col = _iota2((L, L), 0), _iota2((L, L), 1)
    tri = (row >= col).astype(BF16)
    causal = row >= col
    scale = GLA_DK ** -0.5

    U = 8 if nchunk % 8 == 0 else 1
    hs, us = range(GLA_H), range(U)
    ksl = [slice(h * GLA_DK, (h + 1) * GLA_DK) for h in hs]
    vsl = [slice(h * GLA_DV, (h + 1) * GLA_DV) for h in hs]
    uh = [(u, h) for u in us for h in hs]

    def chunks(i, carry):
        rws = [pl.ds(pl.multiple_of((i * U + u) * L, L), L) for u in us]
        q = [q_ref[r, :] * scale for r in rws]
        k = [k_ref[r, :] for r in rws]
        b = [_cumsum_rows(tri, g_scr[r, :]) for r in rws]
        b_end = [x[L - 1:L, :] for x in b]
        qi = [_bf(q[u] * jnp.exp(b[u])) for u in us]
        ki = [_bf(k[u] * jnp.exp(-b[u])) for u in us]
        kend = [_bf(k[u] * jnp.exp(b_end[u] - b[u])) for u in us]
        eb_end = [jnp.exp(x) for x in b_end]
        vb = {(u, h): _bf(v_ref[rws[u], vsl[h]]) for u, h in uh}
        A = {(u, h): _bf(jnp.where(causal, _dot_nt(qi[u][:, ksl[h]], ki[u][:, ksl[h]]), 0.0)) for u, h in uh}
        kv = {(u, h): _dot_tn(vb[u, h], kend[u][:, ksl[h]]) for u, h in uh}
        av = {(u, h): _dot(A[u, h], vb[u, h]) for u, h in uh}
        St = [s_scr[h] for h in hs]
        o = {}
        for u in us:
            for h in hs:
                o[u, h] = av[u, h] + _dot_nt(qi[u][:, ksl[h]], _bf(St[h]))
            St = [St[h] * eb_end[u][:, ksl[h]] + kv[u, h] for h in hs]
        ms = {x: jnp.mean(o[x] * o[x], axis=-1, keepdims=True) for x in uh}
        for u, h in uh:
            o_ref[rws[u], vsl[h]] = _bf((o[u, h] * lax.rsqrt(ms[u, h] + EPS)) * ng_ref[:, vsl[h]]
                                        * _silu(z_ref[rws[u], vsl[h]]))
        for h in hs:
            s_scr[h] = St[h]
        return carry

    lax.fori_loop(0, nchunk // U, chunks, 0)

    @pl.when(t == pl.num_programs(1) - 1)
    def _():
        s_ref[0] = s_scr[...]


def _gla(P, B, T, g2p, gb, ng, s0t):
    L = min(T, CHUNK)
    tt = min(T, SCAN_TT)
    nT = T // tt
    rowblk = lambda w, off: pl.BlockSpec((tt, w), lambda b_, t: (b_ * nT + t, off // w))
    const = lambda shape: pl.BlockSpec(shape, lambda b_, t: (0,) * len(shape))
    st = pl.BlockSpec((1, GLA_H, GLA_DV, GLA_DK), lambda b_, t: (b_, 0, 0, 0))
    return pl.pallas_call(
        functools.partial(_gla_kernel, L=L),
        grid=(B, nT),
        in_specs=[rowblk(512, _OFF['gla_q']), rowblk(512, _OFF['gla_k']), rowblk(BW, _OFF['gla_v']),
                  rowblk(LANE, _OFF['gla_gd']), rowblk(BW, _OFF['gla_z']), const((LANE, 512)), const((1, 512)),
                  const((1, BW)), st],
        out_specs=[pl.BlockSpec((tt, BW), lambda b_, t: (b_ * nT + t, 0)), st],
        out_shape=[jax.ShapeDtypeStruct((B * T, BW), BF16), jax.ShapeDtypeStruct(s0t.shape, F32)],
        scratch_shapes=[pltpu.VMEM((GLA_H, GLA_DV, GLA_DK), F32), pltpu.VMEM((tt, 512), F32)],
        compiler_params=_cparams("parallel", "arbitrary"),
        name="gla_scan",
    )(P, P, P, P, P, g2p, gb, ng, s0t)


def _ret_kernel(q_ref, k_ref, v_ref, z_ref, cos_ref, sin_ref, s0_ref, o_ref, s_ref, s_scr, *, L):
    t = pl.program_id(1)

    @pl.when(t == 0)
    def _():
        s_scr[...] = s0_ref[0]

    nchunk = q_ref.shape[0] // L
    rel = (_iota2((L, L), 0) - _iota2((L, L), 1)).astype(F32)
    tcol = _iota2((L, 1), 0).astype(F32)
    scale = RET_DK ** -0.5
    lgs = [math.log(1.0 - 2.0 ** (-5.0 - h)) for h in range(RET_H)]
    decay = [jnp.where(rel >= 0, jnp.exp(lg * jnp.maximum(rel, 0.0)), 0.0) for lg in lgs]
    rowd = [jnp.exp(lg * (tcol + 1.0)) for lg in lgs]
    cold = [jnp.exp(lg * (L - 1.0 - tcol)) for lg in lgs]

    def rope(x, cos2, sin2):
        return x * cos2 + pltpu.roll(x, RET_DK // 2, 1) * sin2

    U = 8 if nchunk % 8 == 0 else 1
    hs, us = range(RET_H), range(U)
    ksl = [slice(h * RET_DK, (h + 1) * RET_DK) for h in hs]
    vsl = [slice(h * RET_DV, (h + 1) * RET_DV) for h in hs]
    uh = [(u, h) for u in us for h in hs]

    def chunks(i, carry):
        rws = [pl.ds(pl.multiple_of((i * U + u) * L, L), L) for u in us]
        cs = [(cos_ref[r, :], sin_ref[r, :]) for r in rws]
        k = {(u, h): rope(k_ref[rws[u], ksl[h]], *cs[u]) for u, h in uh}
        qb = {(u, h): _bf(rope(q_ref[rws[u], ksl[h]], *cs[u]) * scale) for u, h in uh}
        kb = {x: _bf(k[x]) for x in uh}
        kcb = {(u, h): _bf(k[u, h] * cold[h]) for u, h in uh}
        vb = {(u, h): _bf(v_ref[rws[u], vsl[h]]) for u, h in uh}
        A = {(u, h): _bf(_dot_nt(qb[u, h], kb[u, h]) * decay[h]) for u, h in uh}
        kv = {x: _dot_tn(kcb[x], vb[x]) for x in uh}
        av = {x: _dot(A[x], vb[x]) for x in uh}
        S = [s_scr[h] for h in hs]
        o = {}
        for u in us:
            for h in hs:
                o[u, h] = av[u, h] + rowd[h] * _dot(qb[u, h], _bf(S[h]))
            S = [math.exp(lgs[h] * L) * S[h] + kv[u, h] for h in hs]
        ms = {x: jnp.mean(o[x] * o[x], axis=-1, keepdims=True) for x in uh}
        for u, h in uh:
            o_ref[rws[u], vsl[h]] = _bf(o[u, h] * lax.rsqrt(ms[u, h] + EPS) * _silu(z_ref[rws[u], vsl[h]]))
        for h in hs:
            s_scr[h] = S[h]
        return carry

    lax.fori_loop(0, nchunk // U, chunks, 0)

    @pl.when(t == pl.num_programs(1) - 1)
    def _():
        s_ref[0] = s_scr[...]


def _ret(P, B, T, cos2, sin2, s0):
    L = min(T, CHUNK)
    tt = min(T, SCAN_TT)
    nT = T // tt
    rowblk = lambda w, off: pl.BlockSpec((tt, w), lambda b_, t: (b_ * nT + t, off // w))
    tab = pl.BlockSpec((tt, RET_DK), lambda b_, t: (t, 0))
    st = pl.BlockSpec((1, RET_H, RET_DK, RET_DV), lambda b_, t: (b_, 0, 0, 0))
    return pl.pallas_call(
        functools.partial(_ret_kernel, L=L),
        grid=(B, nT),
        in_specs=[rowblk(512, _OFF['ret_q']), rowblk(512, _OFF['ret_k']), rowblk(BW, _OFF['ret_v']),
                  rowblk(BW, _OFF['ret_z']), tab, tab, st],
        out_specs=[pl.BlockSpec((tt, BW), lambda b_, t: (b_ * nT + t, 0)), st],
        out_shape=[jax.ShapeDtypeStruct((B * T, BW), BF16), jax.ShapeDtypeStruct(s0.shape, F32)],
        scratch_shapes=[pltpu.VMEM((RET_H, RET_DK, RET_DV), F32)],
        compiler_params=_cparams("parallel", "arbitrary"),
        name="ret_scan",
    )(P, P, P, P, cos2, sin2, s0)


def _mlstm_pre_kernel(xm_ref, cp_ref, cw_ref, cb_ref, wq_ref, wk_ref, wv_ref, wif_ref, bif_ref,
                      xc_ref, q_ref, k_ref, v_ref, g_ref, carry_scr):
    t = pl.program_id(1)
    tm = xm_ref.shape[0]
    nprev = ML_CONV - 1

    @pl.when(t == 0)
    def _():
        carry_scr[...] = jnp.zeros_like(carry_scr)
        carry_scr[SUBLANE - nprev:SUBLANE, :] = cp_ref[0]

    xm = xm_ref[...]
    c8 = carry_scr[...]
    rid = _iota2((SUBLANE, 1), 0)
    conv = xm * cw_ref[nprev:nprev + 1, :] + cb_ref[...]
    for j in range(1, ML_CONV):
        rolled = pltpu.roll(xm, j, 0)
        head = jnp.where(rid < j, pltpu.roll(c8, j, 0), rolled[0:SUBLANE, :])
        prev = jnp.concatenate([head, rolled[SUBLANE:, :]], axis=0) if tm > SUBLANE else head
        conv = conv + prev * cw_ref[nprev - j:nprev - j + 1, :]
    carry_scr[...] = xm[tm - SUBLANE:tm, :]
    xc = _silu(conv)
    xc_ref[...] = xc
    xcb, xmb = _bf(xc), _bf(xm)
    q = _dot(xcb, wq_ref[...])
    k = _dot(xcb, wk_ref[...])
    v = _dot(xmb, wv_ref[...])
    g_ref[...] = (_dot(_bf(q), wif_ref[0]) + _dot(_bf(k), wif_ref[1]) + _dot(_bf(v), wif_ref[2])) + bif_ref[...]
    q_ref[...] = _bf(q)
    k_ref[...] = _bf(k * (ML_D ** -0.5))
    v_ref[...] = _bf(v)


def _mlstm_pre(P, B, T, conv_prev, p):
    tm = min(T, SCAN_TT)
    nT = T // tm
    row = pl.BlockSpec((tm, BW), lambda b_, t: (b_ * nT + t, 0))
    const = lambda shape: pl.BlockSpec(shape, lambda b_, t: (0,) * len(shape))
    M = B * T
    return pl.pallas_call(
        _mlstm_pre_kernel,
        grid=(B, nT),
        in_specs=[pl.BlockSpec((tm, BW), lambda b_, t: (b_ * nT + t, _OFF['ml_x'] // BW)),
                  pl.BlockSpec((1, ML_CONV - 1, BW), lambda b_, t: (b_, 0, 0)),
                  const((ML_CONV, BW)), const((1, BW)), const((BW, BW)), const((BW, BW)), const((BW, BW)),
                  const((3, BW, LANE)), const((1, LANE))],
        out_specs=[row, row, row, row, pl.BlockSpec((tm, LANE), lambda b_, t: (b_ * nT + t, 0))],
        out_shape=[jax.ShapeDtypeStruct((M, BW), F32)] + [jax.ShapeDtypeStruct((M, BW), BF16)] * 3
                  + [jax.ShapeDtypeStruct((M, LANE), F32)],
        scratch_shapes=[pltpu.VMEM((SUBLANE, BW), F32)],
        compiler_params=_cparams("parallel", "arbitrary"),
        name="mlstm_pre",
    )(P, conv_prev, p['ml_conv_w'], p['ml_conv_b'], p['ml_wq'], p['ml_wk'], p['ml_wv'], p['ml_wif'], p['ml_bif'])


def _mlstm_kernel(q_ref, k_ref, v_ref, gc_ref, gr_ref, xc_ref, z_ref, ng_ref, sk_ref, c0_ref, n0_ref, m0_ref,
                  o_ref, c_ref, n_ref, m_ref, c_scr, n_scr, m_scr, *, L):
    t = pl.program_id(1)

    @pl.when(t == 0)
    def _():
        c_scr[...] = c0_ref[0]
        n_scr[...] = n0_ref[0]
        m_scr[...] = m0_ref[0]

    nchunk = q_ref.shape[0] // L
    row, col = _iota2((L, L), 0), _iota2((L, L), 1)
    tri = (row >= col).astype(BF16)
    triu = (row <= col).astype(BF16)
    causal = row >= col

    U = 2 if nchunk % 2 == 0 else 1
    hs, us = range(ML_H), range(U)
    sl = [slice(h * ML_D, (h + 1) * ML_D) for h in hs]
    uh = [(u, h) for u in us for h in hs]

    def chunks(i, carry):
        cidx = [i * U + u for u in us]
        rws = [pl.ds(pl.multiple_of(c * L, L), L) for c in cidx]
        gcb = [gc_ref[0, c] for c in cidx]
        grb = [gr_ref[0, c] for c in cidx]
        bc = [_cumsum_rows(tri, x) for x in gcb]
        br = [_cumsum_cols(x, triu) for x in grb]
        qb = {(u, h): q_ref[rws[u], sl[h]] for u, h in uh}
        kb = {(u, h): k_ref[rws[u], sl[h]] for u, h in uh}
        vb = {(u, h): v_ref[rws[u], sl[h]] for u, h in uh}
        q = {x: qb[x].astype(F32) for x in uh}
        k = {x: kb[x].astype(F32) for x in uh}
        qk = {x: _dot_nt(qb[x], kb[x]) for x in uh}
        b_c = {(u, h): bc[u][:, ML_H + h:ML_H + h + 1] for u, h in uh}
        dlog = {(u, h): jnp.where(causal, b_c[u, h] - br[u][ML_H + h:ML_H + h + 1, :] + grb[u][h:h + 1, :], -jnp.inf)
                for u, h in uh}
        m_loc = {x: jnp.max(dlog[x], axis=-1, keepdims=True) for x in uh}
        s_loc = {x: qk[x] * jnp.exp(dlog[x] - m_loc[x]) for x in uh}
        rs = {x: jnp.sum(s_loc[x], axis=-1, keepdims=True) for x in uh}
        b_end = {x: b_c[x][L - 1:L, :] for x in uh}
        m_le = {x: m_loc[x][L - 1:L, :] for x in uh}
        kw = {(u, h): k[u, h] * jnp.exp(b_end[u, h] - b_c[u, h] + gcb[u][:, h:h + 1] - m_le[u, h]) for u, h in uh}
        sv = {x: _dot(_bf(s_loc[x]), vb[x]) for x in uh}
        kv = {x: _dot_tn(_bf(kw[x]), vb[x]) for x in uh}
        ksum = {x: jnp.sum(kw[x], axis=0, keepdims=True) for x in uh}
        C = [c_scr[h] for h in hs]
        nvec = [n_scr[h] for h in hs]
        m_prev = [m_scr[h][:, 0:1] for h in hs]
        hh = {}
        for u in us:
            qc = [_dot(qb[u, h], _bf(C[h])) for h in hs]
            qn = [jnp.sum(q[u, h] * nvec[h], axis=-1, keepdims=True) for h in hs]
            from_state = [b_c[u, h] + m_prev[h] for h in hs]
            m_t = [jnp.maximum(from_state[h], m_loc[u, h]) for h in hs]
            a_in = [jnp.exp(m_loc[u, h] - m_t[h]) for h in hs]
            w_state = [jnp.exp(from_state[h] - m_t[h]) for h in hs]
            den = [a_in[h] * rs[u, h] + w_state[h] * qn[h] for h in hs]
            for h in hs:
                hh[u, h] = ((a_in[h] * sv[u, h] + w_state[h] * qc[h])
                            / jnp.maximum(jnp.abs(den[h]), jnp.exp(-m_t[h])))
            m_new = [m_t[h][L - 1:L, :] for h in hs]
            cd = [jnp.exp(b_end[u, h] + m_prev[h] - m_new[h]) for h in hs]
            sc = [jnp.exp(m_le[u, h] - m_new[h]) for h in hs]
            C = [cd[h] * C[h] + sc[h] * kv[u, h] for h in hs]
            nvec = [cd[h] * nvec[h] + sc[h] * ksum[u, h] for h in hs]
            m_prev = m_new
        mean = {x: jnp.mean(hh[x], axis=-1, keepdims=True) for x in uh}
        cen = {x: hh[x] - mean[x] for x in uh}
        var = {x: jnp.mean(cen[x] * cen[x], axis=-1, keepdims=True) for x in uh}
        for u, h in uh:
            y = cen[u, h] * lax.rsqrt(var[u, h] + EPS)
            o_ref[rws[u], sl[h]] = _bf((y * ng_ref[:, sl[h]] + sk_ref[:, sl[h]] * xc_ref[rws[u], sl[h]])
                                       * _silu(z_ref[rws[u], sl[h]]))
        for h in hs:
            c_scr[h] = C[h]
            n_scr[h] = nvec[h]
            m_scr[h] = jnp.broadcast_to(m_prev[h], (1, LANE))
        return carry

    for i in range(nchunk // U):
        chunks(i, 0)

    @pl.when(t == pl.num_programs(1) - 1)
    def _():
        c_ref[0] = c_scr[...]
        n_ref[0] = n_scr[...]
        m_ref[0] = m_scr[...]


def _mlstm(q, k, v, gc, gr, xc, P, ng, sk, c0, n0, m0, B, T):
    L = min(T, CHUNK)
    tt = min(T, SCAN_TT)
    nT = T // tt
    nc = tt // L
    row = pl.BlockSpec((tt, BW), lambda b_, t: (b_ * nT + t, 0))
    const = pl.BlockSpec((1, BW), lambda b_, t: (0, 0))
    stc = pl.BlockSpec((1, ML_H, ML_D, ML_D), lambda b_, t: (b_, 0, 0, 0))
    stn = pl.BlockSpec((1, ML_H, 1, ML_D), lambda b_, t: (b_, 0, 0, 0))
    stm = pl.BlockSpec((1, ML_H, 1, LANE), lambda b_, t: (b_, 0, 0, 0))
    return pl.pallas_call(
        functools.partial(_mlstm_kernel, L=L),
        grid=(B, nT),
        in_specs=[row, row, row,
                  pl.BlockSpec((1, nc, L, 8), lambda b_, t: (b_, t, 0, 0)),
                  pl.BlockSpec((1, nc, 8, L), lambda b_, t: (b_, t, 0, 0)),
                  row, pl.BlockSpec((tt, BW), lambda b_, t: (b_ * nT + t, _OFF['ml_z'] // BW)), const, const, stc, stn, stm],
        out_specs=[row, stc, stn, stm],
        out_shape=[jax.ShapeDtypeStruct((B * T, BW), BF16), jax.ShapeDtypeStruct(c0.shape, F32),
                   jax.ShapeDtypeStruct(n0.shape, F32), jax.ShapeDtypeStruct(m0.shape, F32)],
        scratch_shapes=[pltpu.VMEM((ML_H, ML_D, ML_D), F32), pltpu.VMEM((ML_H, 1, ML_D), F32),
                        pltpu.VMEM((ML_H, 1, LANE), F32)],
        compiler_params=_cparams("parallel", "arbitrary"),
        name="mlstm_scan",
    )(q, k, v, gc, gr, xc, P, ng, sk, c0, n0, m0)


def _xattn_kernel(q_ref, z_ref, mk_ref, mv_ref, o_ref):
    hs = range(XA_H)
    sl = [slice(h * XA_D, (h + 1) * XA_D) for h in hs]
    s = [_dot_nt(_bf(q_ref[:, x]), _bf(mk_ref[0, :, x])) * (XA_D ** -0.5) for x in sl]
    e = [jnp.exp(x - jnp.max(x, axis=-1, keepdims=True)) for x in s]
    prob = [_bf(x / jnp.sum(x, axis=-1, keepdims=True)) for x in e]
    o = [_dot(prob[h], _bf(mv_ref[0, :, sl[h]])) for h in hs]
    for h in hs:
        o_ref[:, sl[h]] = _bf(o[h] * _silu(z_ref[:, sl[h]]))


def _xattn(P, B, T, mk, mv):
    tt = min(T, SCAN_TT)
    nT = T // tt
    n_mem = mk.shape[1]
    mem = pl.BlockSpec((1, n_mem, BW), lambda b_, t: (b_, 0, 0))
    return pl.pallas_call(
        _xattn_kernel,
        grid=(B, nT),
        in_specs=[pl.BlockSpec((tt, BW), lambda b_, t: (b_ * nT + t, _OFF['xa_q'] // BW)),
                  pl.BlockSpec((tt, BW), lambda b_, t: (b_ * nT + t, _OFF['xa_z'] // BW)), mem, mem],
        out_specs=pl.BlockSpec((tt, BW), lambda b_, t: (b_ * nT + t, 0)),
        out_shape=jax.ShapeDtypeStruct((B * T, BW), BF16),
        compiler_params=_cparams("parallel", "parallel"),
        name="xattn",
    )(P, P, mk, mv)


def _merge_kernel(o0, o1, o2, o3, o4, g_ref, w_ref, out_ref):
    ys = [_dot(o_ref[...], w_ref[i]) for i, o_ref in enumerate((o0, o1, o2, o3, o4))]
    acc = None
    for i, y in enumerate(ys):
        term = _sigmoid(g_ref[:, i * D_MODEL:(i + 1) * D_MODEL]) * y
        acc = term if acc is None else acc + term
    out_ref[...] = _bf(acc)


def _merge(ogs, P, wb, tm):
    M = ogs[0].shape[0]
    gw = N_BRANCH * D_MODEL
    og = pl.BlockSpec((tm, BW), lambda i: (i, 0))
    return pl.pallas_call(
        _merge_kernel,
        grid=(M // tm,),
        in_specs=[og] * N_BRANCH + [pl.BlockSpec((tm, gw), lambda i: (i, _OFF['gates'] // gw)),
                                    pl.BlockSpec((N_BRANCH, BW, D_MODEL), lambda i: (0, 0, 0),
                                                 pipeline_mode=pl.Buffered(1))],
        out_specs=pl.BlockSpec((tm, D_MODEL), lambda i: (i, 0)),
        out_shape=jax.ShapeDtypeStruct((M, D_MODEL), BF16),
        compiler_params=_cparams("parallel"),
        name="merge",
    )(*ogs, P, wb)


def _resid_kernel(x_ref, m_ref, w_ref, o_ref):
    o_ref[...] = x_ref[...] + _dot(m_ref[...], w_ref[...])


def _resid_norm_kernel(x_ref, m_ref, w_ref, g_ref, o_ref):
    y = x_ref[...] + _dot(m_ref[...], w_ref[...])
    ms = jnp.mean(y * y, axis=-1, keepdims=True)
    o_ref[...] = (y * lax.rsqrt(ms + EPS)) * g_ref[...]


def _resid_out(x, merged, w, tm, final_g=None):
    M = x.shape[0]
    row = pl.BlockSpec((tm, D_MODEL), lambda i: (i, 0))
    in_specs = [row, row, pl.BlockSpec((D_MODEL, D_MODEL), lambda i: (0, 0), pipeline_mode=pl.Buffered(1))]
    args = [x, merged, w]
    if final_g is not None:
        in_specs.append(pl.BlockSpec((1, D_MODEL), lambda i: (0, 0)))
        args.append(final_g)
    return pl.pallas_call(
        _resid_kernel if final_g is None else _resid_norm_kernel,
        grid=(M // tm,),
        in_specs=in_specs,
        out_specs=row,
        out_shape=jax.ShapeDtypeStruct((M, D_MODEL), F32),
        compiler_params=_cparams("parallel"),
        name="resid_out",
    )(*args)


def _blockdiag(w):
    depth, n, c, d = w.shape
    rows = w.reshape(depth, n * c, d)
    tile = (jnp.arange(d)[:, None] == jnp.arange(n * d)[None, :] % d).astype(w.dtype)
    dense = jnp.einsum('lrd,dj->lrj', rows, tile)
    mask = (jnp.arange(n * c)[:, None] // c) == (jnp.arange(n * d)[None, :] // d)
    return jnp.where(mask, dense, 0.0).astype(BF16)


def _rope_tables(start, T):
    half = RET_DK // 2
    inv = ROPE_BASE ** (-jnp.linspace(0.0, 1.0, half, dtype=F32))
    blk = ROPE_BLK
    if T % blk or T <= blk:
        ang = (start + jnp.arange(T, dtype=F32))[:, None] * inv[None, :]
        cos, sin = jnp.cos(ang), jnp.sin(ang)
    else:
        hi = (start + blk * jnp.arange(T // blk, dtype=F32))[:, None] * inv[None, :]
        lo = jnp.arange(blk, dtype=F32)[:, None] * inv[None, :]
        ch, sh, cl, sl_ = jnp.cos(hi)[:, None, :], jnp.sin(hi)[:, None, :], jnp.cos(lo)[None], jnp.sin(lo)[None]
        cos = (ch * cl - sh * sl_).reshape(T, half)
        sin = (sh * cl + ch * sl_).reshape(T, half)
    return jnp.concatenate([cos, cos], axis=-1), jnp.concatenate([-sin, sin], axis=-1)


def _layer(x, B, T, tabs, st, mem_k, mem_v, p, final_g=None):
    M = B * T
    tm = min(M, PROJ_TM)
    P = _normproj(x, p['norm_g'], p['w_pack'], tm, PROJ_TN)
    P3 = P.reshape(B, T, NP)

    og_a, s_rwkv = _rwkv(P, B, T, p, st['rwkv'], st['rwkv_shift'])
    shift_new = P3[:, T - 1:, _OFF['rwkv_shift']:_OFF['rwkv_shift'] + RWKV_SHIFT_W]

    og_b, s_gla_t = _gla(P, B, T, p['gla_g2p'], p['gla_gb'], p['gla_norm_g'], jnp.swapaxes(st['gla'], -1, -2))
    s_gla = jnp.swapaxes(s_gla_t, -1, -2)

    og_c, s_ret = _ret(P, B, T, tabs[0], tabs[1], st['ret'])

    xc, q_m, k_m, v_m, gates = _mlstm_pre(P, B, T, st['ml_conv'], p)
    L = min(T, CHUNK)
    gcol = jnp.concatenate([gates[:, :ML_H], _log_sigmoid(gates[:, ML_H:2 * ML_H])], axis=-1).reshape(B, T // L, L, 2 * ML_H)
    grow = jnp.swapaxes(gcol, -1, -2)
    og_d, c_new, n_new, m_new = _mlstm(
        q_m, k_m, v_m, gcol, grow, xc, P, p['ml_norm_g'], p['ml_skip'], st['ml_c'], st['ml_n'][:, :, None, :],
        jnp.broadcast_to(st['ml_m'][:, :, None, None], st['ml_m'].shape + (1, LANE)), B, T)
    xm_tail = P3[:, max(T - (ML_CONV - 1), 0):, _OFF['ml_x']:_OFF['ml_x'] + BW]
    conv_new = jnp.concatenate([st['ml_conv'], xm_tail], axis=1)[:, -(ML_CONV - 1):]

    og_x = _xattn(P, B, T, mem_k.reshape(B, -1, BW), mem_v.reshape(B, -1, BW))

    merged = _merge([og_a, og_b, og_c, og_d, og_x], P, p['w_branch'], min(M, MERGE_TM))
    x_new = _resid_out(x, merged, p['w_out'], min(M, RESID_TM), final_g)
    new = {'rwkv': s_rwkv, 'rwkv_shift': shift_new, 'gla': s_gla, 'ret': s_ret, 'ml_c': c_new,
           'ml_n': n_new[:, :, 0, :], 'ml_m': m_new[:, :, 0, 0], 'ml_conv': conv_new}
    return x_new, new


_STATE_KEYS = ('rwkv', 'rwkv_shift', 'gla', 'ret', 'ml_c', 'ml_n', 'ml_m', 'ml_conv')


def kernel(x_prompt, x_sample, mem_prompt, cache_mem_k, cache_mem_v, state_rwkv, state_rwkv_shift, state_gla, state_ret, state_mlstm_c, state_mlstm_n, state_mlstm_m, state_mlstm_conv, norm_g, mem_norm_g, w_in, w_mem_kv, rwkv_mu, rwkv_w0, rwkv_w2, rwkv_a0, rwkv_a2, rwkv_k_k, rwkv_k_a, rwkv_r_k, rwkv_ln_g, gla_g2, gla_gb, gla_norm_g, ml_conv_w, ml_conv_b, ml_wq, ml_wk, ml_wv, ml_w_if, ml_b_if, ml_skip, ml_norm_g, w_branch, w_out, final_norm_g):
    Bp, Tp, D = x_prompt.shape
    Bs, Ts, _ = x_sample.shape
    depth = w_in.shape[0]
    n_mem = mem_prompt.shape[1]

    params = {
        'norm_g': norm_g[:, None, :], 'mem_norm_g': mem_norm_g[:, None, :],
        'w_mem_kv': _bf(w_mem_kv),
        'rwkv_mu': rwkv_mu[:, None, :], 'rwkv_w0': rwkv_w0[:, None, :], 'rwkv_a0': rwkv_a0[:, None, :],
        'rwkv_w2p': _bf(jnp.pad(rwkv_w2, ((0, 0), (0, LANE - RWKV_LORA), (0, 0)))),
        'rwkv_a2p': _bf(jnp.pad(rwkv_a2, ((0, 0), (LANE - RWKV_LORA, 0), (0, 0)))),
        'rwkv_k_k': rwkv_k_k[:, None, :], 'rwkv_k_a': rwkv_k_a[:, None, :], 'rwkv_r_k': rwkv_r_k[:, None, :],
        'rwkv_ln_g': rwkv_ln_g[:, None, :],
        'gla_g2p': _bf(jnp.pad(gla_g2, ((0, 0), (0, LANE - GLA_LORA), (0, 0)))), 'gla_gb': gla_gb[:, None, :],
        'gla_norm_g': gla_norm_g[:, None, :],
        'ml_conv_w': ml_conv_w, 'ml_conv_b': ml_conv_b[:, None, :],
        'ml_wq': _blockdiag(ml_wq), 'ml_wk': _blockdiag(ml_wk), 'ml_wv': _blockdiag(ml_wv),
        'ml_wif': _bf(jnp.pad(ml_w_if.reshape(depth, 3, BW, 2 * ML_H), ((0, 0), (0, 0), (0, 0), (0, LANE - 2 * ML_H)))),
        'ml_bif': jnp.pad(ml_b_if, ((0, 0), (0, LANE - 2 * ML_H)))[:, None, :],
        'ml_skip': ml_skip[:, None, :], 'ml_norm_g': ml_norm_g[:, None, :],
        'w_branch': _bf(w_branch), 'w_out': _bf(w_out),
    }
    cache = {'rwkv': state_rwkv, 'rwkv_shift': state_rwkv_shift, 'gla': state_gla, 'ret': state_ret,
             'ml_c': state_mlstm_c, 'ml_n': state_mlstm_n, 'ml_m': state_mlstm_m, 'ml_conv': state_mlstm_conv,
             'mem_k': cache_mem_k, 'mem_v': cache_mem_v}
    tabs_p = _rope_tables(0.0, Tp)
    tabs_s = _rope_tables(float(PAST_LEN), Ts)
    zero_p = {
        'rwkv': jnp.zeros((Bp, RWKV_H, RWKV_D, RWKV_D), F32), 'rwkv_shift': jnp.zeros((Bp, 1, RWKV_SHIFT_W), F32),
        'gla': jnp.zeros((Bp, GLA_H, GLA_DK, GLA_DV), F32), 'ret': jnp.zeros((Bp, RET_H, RET_DK, RET_DV), F32),
        'ml_c': jnp.zeros((Bp, ML_H, ML_D, ML_D), F32), 'ml_n': jnp.zeros((Bp, ML_H, ML_D), F32),
        'ml_m': jnp.zeros((Bp, ML_H), F32), 'ml_conv': jnp.zeros((Bp, ML_CONV - 1, BW), F32),
    }
    mem2d = mem_prompt.reshape(Bp * n_mem, D)

    yp, ys = x_prompt.reshape(Bp * Tp, D), x_sample.reshape(Bs * Ts, D)
    outs = []
    for l in range(depth):
        p = {nm: arr[l] for nm, arr in params.items()}
        p['w_pack'] = _pack_w_in(w_in, l)
        kv = _normproj(mem2d, p['mem_norm_g'], p['w_mem_kv'], min(Bp * n_mem, MEM_TM), MEM_TN)
        mk_l = kv[:, :BW].reshape(Bp, n_mem, XA_H, XA_D)
        mv_l = kv[:, BW:].reshape(Bp, n_mem, XA_H, XA_D)
        fg = final_norm_g[None, :] if l == depth - 1 else None
        yp, stp_l = _layer(yp, Bp, Tp, tabs_p, zero_p, mk_l, mv_l, p, fg)
        ys, sts_l = _layer(ys, Bs, Ts, tabs_s, {nm: cache[nm][l] for nm in _STATE_KEYS},
                           cache['mem_k'][l], cache['mem_v'][l], p, fg)
        outs.append((stp_l, mk_l, mv_l, sts_l))
    stp = {nm: jnp.stack([o[0][nm] for o in outs]) for nm in _STATE_KEYS}
    sts = {nm: jnp.stack([o[3][nm] for o in outs]) for nm in _STATE_KEYS}
    mk = jnp.stack([o[1] for o in outs])
    mv = jnp.stack([o[2] for o in outs])
    y_prompt = yp.reshape(Bp, Tp, D)
    y_sample = ys.reshape(Bs, Ts, D)
    return (y_prompt, y_sample,
            stp['rwkv'], stp['rwkv_shift'], stp['gla'], stp['ret'], stp['ml_c'], stp['ml_n'], stp['ml_m'],
            stp['ml_conv'], mk, mv,
            sts['rwkv'], sts['rwkv_shift'], sts['gla'], sts['ret'], sts['ml_c'], sts['ml_n'], sts['ml_m'],
            sts['ml_conv'])
```
